```python
import jax, jax.numpy as jnp
from jax import lax
import numpy as np

D_MODEL = 1024
BATCH = 8
SEQ = 4096
DEPTH = 2

HEAD_DIM = 64
ATTN_GROUPS = ((128, 1), (512, 4), (2048, 16))
N_GROUPS = 3
HEADS_PER_GROUP = 4
ATTN_WIDTH = N_GROUPS * HEADS_PER_GROUP * HEAD_DIM
ATTN_OUT = HEADS_PER_GROUP * HEAD_DIM
WIN_BLOCK = 128
ROPE_THETA = 10000.0

GMLP_CHUNK = 128
GMLP_GROUPS = 4
GMLP_GROUP_DIM = 128
GMLP_WIDTH = GMLP_GROUPS * GMLP_GROUP_DIM

N_EXPERTS = 32
TOP_K = 4
D_FF = D_MODEL
SWIGLU_ALPHA = 1.702
SWIGLU_LIMIT = 7.0
MOE_BLOCK = 128

N_BRANCH = 2
IN_WIDTH = 3 * ATTN_WIDTH + 2 * GMLP_WIDTH + N_BRANCH * D_MODEL
RMS_EPS = 1e-5
LN_EPS = 1e-5
NEG_INF = -1e30

kernel_name = "hybrid_dilated_attn_gmlp_moe_block"


def rmsnorm(x, g):
    xf = x.astype(jnp.float32)
    y = xf * lax.rsqrt(jnp.mean(xf * xf, axis=-1, keepdims=True) + RMS_EPS)
    return (y * g.astype(jnp.float32)).astype(x.dtype)


def layernorm(x, g, b):
    xf = x.astype(jnp.float32)
    mu = jnp.mean(xf, axis=-1, keepdims=True)
    var = jnp.mean(jnp.square(xf - mu), axis=-1, keepdims=True)
    y = (xf - mu) * lax.rsqrt(var + LN_EPS)
    return (y * g.astype(jnp.float32) + b.astype(jnp.float32)).astype(x.dtype)


def rope(x, positions):
    half = HEAD_DIM // 2
    inv_freq = ROPE_THETA ** (-jnp.arange(half, dtype=jnp.float32) / half)
    ang = positions.astype(jnp.float32)[..., None] * inv_freq
    cos = jnp.cos(ang)[:, :, None, :]
    sin = jnp.sin(ang)[:, :, None, :]
    xf = x.astype(jnp.float32)
    x1, x2 = xf[..., :half], xf[..., half:]
    out = jnp.concatenate([x1 * cos - x2 * sin, x2 * cos + x1 * sin], axis=-1)
    return out.astype(x.dtype)


def dilated_window_attention(q, k, v, window, dilation):
    B, S, H, Dh = q.shape
    r = dilation
    L = S // r
    wn = window // r
    nb = -(-L // WIN_BLOCK)
    Lp = nb * WIN_BLOCK

    def to_sub(t):
        t = t.reshape(B, L, r, H, Dh).transpose(0, 2, 1, 3, 4).reshape(B * r, L, H, Dh)
        t = jnp.pad(t, ((0, 0), (0, Lp - L), (0, 0), (0, 0)))
        return t.reshape(B * r, nb, WIN_BLOCK, H, Dh)

    qs, ks, vs = to_sub(q), to_sub(k), to_sub(v)

    def with_prev(t):
        prev = jnp.pad(t, ((0, 0), (1, 0), (0, 0), (0, 0), (0, 0)))[:, :-1]
        return jnp.concatenate([prev, t], axis=2)

    kb, vb = with_prev(ks), with_prev(vs)
    s = jnp.einsum('znqhd,znkhd->znhqk', qs.astype(jnp.float32), kb.astype(jnp.float32))
    s = s * (Dh ** -0.5)
    qi = jnp.arange(WIN_BLOCK)[:, None]
    kj = jnp.arange(2 * WIN_BLOCK)[None, :]
    dist = WIN_BLOCK + qi - kj
    band = (dist >= 0) & (dist <= wn)
    key_pos = jnp.arange(nb)[:, None, None] * WIN_BLOCK - WIN_BLOCK + kj
    valid = band[None] & (key_pos >= 0)
    s = jnp.where(valid[None, :, None], s, NEG_INF)
    m = jnp.max(s, axis=-1, keepdims=True)
    p = jnp.exp(s - m)
    den = jnp.sum(p, axis=-1, keepdims=True)
    o = jnp.einsum('znhqk,znkhd->znqhd', p, vb.astype(jnp.float32))
    o = o / den.transpose(0, 1, 3, 2, 4)
    lse = (m + jnp.log(den)).transpose(0, 1, 3, 2, 4)

    def from_sub(t):
        t = t.reshape(B, r, Lp, H, t.shape[-1])[:, :, :L]
        return t.transpose(0, 2, 1, 3, 4).reshape(B, S, H, t.shape[-1])

    return from_sub(o), from_sub(lse)


def token_mixers(h, positions, w_in, w_s, b_s, ln_g, ln_b, w_pa, w_pg, w_o):
    B, S, _ = h.shape
    proj = h @ w_in
    cuts = [ATTN_WIDTH, 2 * ATTN_WIDTH, 3 * ATTN_WIDTH,
            3 * ATTN_WIDTH + GMLP_WIDTH, 3 * ATTN_WIDTH + 2 * GMLP_WIDTH]
    q, k, v, u, vg, gates = jnp.split(proj, cuts, axis=-1)

    n_heads = N_GROUPS * HEADS_PER_GROUP
    grp = (B, S, N_GROUPS, HEADS_PER_GROUP, HEAD_DIM)
    q = rope(q.reshape(B, S, n_heads, HEAD_DIM), positions).reshape(grp)
    k = rope(k.reshape(B, S, n_heads, HEAD_DIM), positions).reshape(grp)
    v = v.reshape(grp)
    outs, lses = [], []
    for g, (win, dil) in enumerate(ATTN_GROUPS):
        o_g, l_g = dilated_window_attention(q[:, :, g], k[:, :, g], v[:, :, g], win, dil)
        outs.append(o_g)
        lses.append(l_g)
    o_all = jnp.stack(outs, axis=0)
    w_den = jax.nn.softmax(jnp.stack(lses, axis=0), axis=0)
    attn = jnp.sum(w_den * o_all, axis=0).reshape(B, S, ATTN_OUT).astype(h.dtype)

    u = jax.nn.gelu(u)
    vg = layernorm(jax.nn.gelu(vg), ln_g, ln_b)
    nc = S // GMLP_CHUNK
    vc = vg.reshape(B, nc, GMLP_CHUNK, GMLP_GROUPS, GMLP_GROUP_DIM)
    tri = jnp.tril(jnp.ones((GMLP_CHUNK, GMLP_CHUNK), dtype=bool))
    ws = jnp.where(tri[None], w_s, jnp.zeros_like(w_s))
    mixed = jnp.einsum('gts,bnsgc->bntgc', ws, vc) + b_s.T[None, None, :, :, None]
    gm = u * mixed.reshape(B, S, GMLP_WIDTH)

    g_a, g_g = jnp.split(gates, N_BRANCH, axis=-1)
    merged = jax.nn.sigmoid(g_a) * (attn @ w_pa) + jax.nn.sigmoid(g_g) * (gm @ w_pg)
    return merged @ w_o


def clamped_swiglu(hb):
    x_glu = jnp.minimum(hb[..., ::2], SWIGLU_LIMIT)
    x_lin = jnp.clip(hb[..., 1::2], -SWIGLU_LIMIT, SWIGLU_LIMIT)
    return x_glu * jax.nn.sigmoid(SWIGLU_ALPHA * x_glu) * (x_lin + 1)


def moe(h, w_router, b_router, w1, b1, w2, b2):
    B, S, D = h.shape
    N = B * S
    xt = h.reshape(N, D)
    logits = (xt @ w_router + b_router).astype(jnp.float32)
    top_vals, top_idx = lax.top_k(logits, TOP_K)
    gate = jax.nn.softmax(top_vals, axis=-1)

    A = N * TOP_K
    e_flat = top_idx.reshape(A)
    tok_flat = jnp.arange(A, dtype=jnp.int32) // TOP_K
    order = jnp.argsort(e_flat)
    e_sorted = e_flat[order]
    tok_sorted = tok_flat[order]
    gate_sorted = gate.reshape(A)[order]

    counts = jnp.zeros((N_EXPERTS,), jnp.int32).at[e_flat].add(1)
    start = jnp.cumsum(counts) - counts
    padded = (counts + MOE_BLOCK - 1) // MOE_BLOCK * MOE_BLOCK
    pad_end = jnp.cumsum(padded)
    pad_start = pad_end - padded
    rank = jnp.arange(A, dtype=jnp.int32) - start[e_sorted]
    dest = pad_start[e_sorted] + rank

    n_blocks = -(-A // MOE_BLOCK) + N_EXPERTS
    P = n_blocks * MOE_BLOCK
    buf = jnp.zeros((P, D), h.dtype).at[dest].set(xt[tok_sorted])
    block_start = jnp.arange(n_blocks, dtype=jnp.int32) * MOE_BLOCK
    block_e = jnp.minimum(jnp.searchsorted(pad_end, block_start, side='right'),
                          N_EXPERTS - 1).astype(jnp.int32)

    def expert_block(args):
        xb, e = args
        hb = xb @ w1[e] + b1[e]
        return clamped_swiglu(hb) @ w2[e] + b2[e]

    out = lax.map(expert_block, (buf.reshape(n_blocks, MOE_BLOCK, D), block_e))
    y_assign = out.reshape(P, D)[dest] * gate_sorted[:, None].astype(out.dtype)
    y = jax.ops.segment_sum(y_assign, tok_sorted, num_segments=N)
    return y.reshape(B, S, D).astype(h.dtype)


def setup_inputs(seed: int = 0) -> dict:
    key = jax.random.key(seed)
    ks = jax.random.split(key, 24)

    def nrm(k, shape, scale):
        return jax.random.normal(k, shape, jnp.float32) * scale

    x = nrm(ks[0], (BATCH, SEQ, D_MODEL), 1.0)
    c = nrm(ks[1], (BATCH, D_MODEL), 1.0)
    positions = (jax.random.randint(ks[2], (BATCH, 1), 0, 1024, dtype=jnp.int32)
                 + jnp.arange(SEQ, dtype=jnp.int32)[None, :])
    return {
        "x": x,
        "c": c,
        "positions": positions,
        "w_ada": nrm(ks[3], (DEPTH, D_MODEL, 6 * D_MODEL), 0.5 * D_MODEL ** -0.5),
        "b_ada": nrm(ks[4], (DEPTH, 6 * D_MODEL), 0.02),
        "norm1_g": 1.0 + nrm(ks[5], (DEPTH, D_MODEL), 0.02),
        "w_in": nrm(ks[6], (DEPTH, D_MODEL, IN_WIDTH), D_MODEL ** -0.5),
        "w_s": nrm(ks[7], (DEPTH, GMLP_GROUPS, GMLP_CHUNK, GMLP_CHUNK), GMLP_CHUNK ** -0.5),
        "b_s": 1.0 + nrm(ks[8], (DEPTH, GMLP_GROUPS, GMLP_CHUNK), 0.1),
        "ln_g": 1.0 + nrm(ks[9], (DEPTH, GMLP_WIDTH), 0.02),
        "ln_b": nrm(ks[10], (DEPTH, GMLP_WIDTH), 0.02),
        "w_pa": nrm(ks[11], (DEPTH, ATTN_OUT, D_MODEL), ATTN_OUT ** -0.5),
        "w_pg": nrm(ks[12], (DEPTH, GMLP_WIDTH, D_MODEL), GMLP_WIDTH ** -0.5),
        "w_o": nrm(ks[13], (DEPTH, D_MODEL, D_MODEL), D_MODEL ** -0.5),
        "norm2_g": 1.0 + nrm(ks[14], (DEPTH, D_MODEL), 0.02),
        "w_router": nrm(ks[15], (DEPTH, D_MODEL, N_EXPERTS), D_MODEL ** -0.5),
        "b_router": nrm(ks[16], (DEPTH, N_EXPERTS), 0.01),
        "w1": nrm(ks[17], (DEPTH, N_EXPERTS, D_MODEL, 2 * D_FF), D_MODEL ** -0.5),
        "b1": nrm(ks[18], (DEPTH, N_EXPERTS, 2 * D_FF), 0.01),
        "w2": nrm(ks[19], (DEPTH, N_EXPERTS, D_FF, D_MODEL), D_FF ** -0.5),
        "b2": nrm(ks[20], (DEPTH, N_EXPERTS, D_MODEL), 0.01),
        "final_g": 1.0 + nrm(ks[21], (D_MODEL,), 0.02),
    }


def reference(x, c, positions, w_ada, b_ada, norm1_g, w_in, w_s, b_s, ln_g, ln_b,
              w_pa, w_pg, w_o, norm2_g, w_router, b_router, w1, b1, w2, b2, final_g):
    h = x
    for l in range(DEPTH):
        mod = jax.nn.silu(c) @ w_ada[l] + b_ada[l]
        sh1, sc1, g1, sh2, sc2, g2 = jnp.split(mod[:, None, :], 6, axis=-1)
        a = rmsnorm(h, norm1_g[l]) * (1 + sc1) + sh1
        h = h + g1 * token_mixers(a, positions, w_in[l], w_s[l], b_s[l], ln_g[l], ln_b[l],
                                  w_pa[l], w_pg[l], w_o[l])
        f = rmsnorm(h, norm2_g[l]) * (1 + sc2) + sh2
        h = h + g2 * moe(f, w_router[l], b_router[l], w1[l], b1[l], w2[l], b2[l])
    return rmsnorm(h, final_g)
```

```python
import functools

import jax
import jax.numpy as jnp
from jax import lax
from jax.experimental import pallas as pl
from jax.experimental.pallas import tpu as pltpu

F32 = jnp.float32
BF16 = jnp.bfloat16
I32 = jnp.int32

HEAD_DIM = 64
HEADS_PER_GROUP = 4
GROUP_W = HEADS_PER_GROUP * HEAD_DIM
ATTN_GROUPS = ((128, 1), (512, 4), (2048, 16))
N_GROUPS = len(ATTN_GROUPS)
ATTN_W = N_GROUPS * GROUP_W
WIN_BLOCK = 128
ROPE_THETA = 10000.0
GMLP_CHUNK = 128
GMLP_GROUPS = 4
GMLP_W = GMLP_GROUPS * 128
N_EXPERTS = 32
TOP_K = 4
SWIGLU_ALPHA = 1.702
SWIGLU_LIMIT = 7.0
RMS_EPS = 1e-5
LN_EPS = 1e-5
NEG_INF = -1e30

TOKEN_TILE = 512
EXPERT_BLOCK = 256
ROW_TILE = 256
VMEM_LIMIT = 56 * 1024 * 1024

_ARB = lambda n: pltpu.CompilerParams(dimension_semantics=("arbitrary",) * n,
                                      vmem_limit_bytes=VMEM_LIMIT)


def _bdot(a, b):
    return jnp.dot(a, b, preferred_element_type=F32)


def _mod_kernel(c_ref, w_ref, b_ref, o_ref):
    c = c_ref[...]
    s = c * jax.nn.sigmoid(c)
    o_ref[0] = jnp.dot(s, w_ref[0], precision=lax.Precision.HIGHEST,
                       preferred_element_type=F32) + b_ref[0]


def _modulation(c, w_ada, b_ada):
    depth, d, six_d = w_ada.shape
    b = c.shape[0]
    n_col = six_d // d
    return pl.pallas_call(
        _mod_kernel,
        grid=(depth, n_col),
        in_specs=[pl.BlockSpec((b, d), lambda l, j: (0, 0)),
                  pl.BlockSpec((1, d, d), lambda l, j: (l, 0, j)),
                  pl.BlockSpec((1, 1, d), lambda l, j: (l, 0, j))],
        out_specs=pl.BlockSpec((1, b, d), lambda l, j: (l, 0, j)),
        out_shape=jax.ShapeDtypeStruct((depth, b, six_d), F32),
        compiler_params=_ARB(2),
        name="adaln_mod",
    )(c, w_ada, b_ada.reshape(depth, 1, six_d))


def _rope_kernel(pos_ref, freq_ref, sign_ref, cos_ref, sin_ref):
    ang = pos_ref[...].astype(F32) * freq_ref[...]
    cos_ref[...] = jnp.cos(ang)
    sin_ref[...] = jnp.sin(ang) * sign_ref[...]


def _rope_tables(positions):
    n = positions.size
    half = HEAD_DIM // 2
    inv_freq = ROPE_THETA ** (-jnp.arange(half, dtype=F32) / half)
    freq = jnp.tile(inv_freq, 4).reshape(1, 128)
    sign = jnp.tile(jnp.concatenate([-jnp.ones((half,), F32), jnp.ones((half,), F32)]), 2)
    sign = sign.reshape(1, 128)
    tile = 1024
    return pl.pallas_call(
        _rope_kernel,
        grid=(n // tile,),
        in_specs=[pl.BlockSpec((tile, 1), lambda i: (i, 0)),
                  pl.BlockSpec((1, 128), lambda i: (0, 0)),
                  pl.BlockSpec((1, 128), lambda i: (0, 0))],
        out_specs=[pl.BlockSpec((tile, 128), lambda i: (i, 0))] * 2,
        out_shape=[jax.ShapeDtypeStruct((n, 128), F32)] * 2,
        compiler_params=_ARB(1),
        name="rope_tables",
    )(positions.reshape(n, 1), freq, sign)


def _modulated_norm(x, g, sc, sh):
    ms = jnp.mean(x * x, axis=-1, keepdims=True)
    y = x * lax.rsqrt(ms + RMS_EPS) * g
    return y * (1.0 + sc) + sh


def _inproj_kernel(h_ref, sc_ref, sh_ref, g_ref, w_ref, cos_ref, sin_ref,
                   ws_ref, bs_ref, lng_ref, lnb_ref, qkv_ref, gm_ref):
    tm = h_ref.shape[0]
    a = _modulated_norm(h_ref[...], g_ref[...], sc_ref[0], sh_ref[0]).astype(BF16)
    cos = cos_ref[...]
    sin = sin_ref[...]
    lane = lax.broadcasted_iota(I32, (tm, 128), 1)
    first_half = (lane % HEAD_DIM) < (HEAD_DIM // 2)

    def rope(x):
        swapped = jnp.where(first_half, pltpu.roll(x, 96, 1), pltpu.roll(x, 32, 1))
        return x * cos + swapped * sin

    for g in range(N_GROUPS):
        for part in range(3):
            c0 = (g * 3 + part) * GROUP_W
            y = _bdot(a, w_ref[:, c0:c0 + GROUP_W])
            if part < 2:
                y = jnp.concatenate([rope(y[:, :128]), rope(y[:, 128:])], axis=1)
            if part == 0:
                y = y * (HEAD_DIM ** -0.5)
            qkv_ref[:, c0:c0 + GROUP_W] = y.astype(BF16)

    c0 = 3 * ATTN_W
    u = jax.nn.gelu(_bdot(a, w_ref[:, c0:c0 + GMLP_W]))
    v = jax.nn.gelu(_bdot(a, w_ref[:, c0 + GMLP_W:c0 + 2 * GMLP_W]))
    mu = jnp.mean(v, axis=-1, keepdims=True)
    var = jnp.mean(jnp.square(v - mu), axis=-1, keepdims=True)
    v = ((v - mu) * lax.rsqrt(var + LN_EPS) * lng_ref[...] + lnb_ref[...]).astype(BF16)
    row = lax.broadcasted_iota(I32, (GMLP_CHUNK, GMLP_CHUNK), 0)
    col = lax.broadcasted_iota(I32, (GMLP_CHUNK, GMLP_CHUNK), 1)
    tril = col <= row
    for g in range(GMLP_GROUPS):
        wsg = jnp.where(tril, ws_ref[g], 0.0).astype(BF16)
        bias = bs_ref[:, g:g + 1]
        cs = slice(g * 128, (g + 1) * 128)
        for c in range(tm // GMLP_CHUNK):
            rs = slice(c * GMLP_CHUNK, (c + 1) * GMLP_CHUNK)
            mixed = _bdot(wsg, v[rs, cs]) + bias
            gm_ref[rs, cs] = (u[rs, cs] * mixed).astype(BF16)


def _inproj(h, sc, sh, g, w, cos, sin, w_s, b_s_t, ln_g, ln_b, seq):
    n, d = h.shape
    tm = TOKEN_TILE
    per_b = seq // tm
    wn = w.shape[1]
    row = lambda i: (i, 0)
    bat = lambda i: (i // per_b, 0, 0)
    full2 = lambda i: (0, 0)
    return pl.pallas_call(
        _inproj_kernel,
        grid=(n // tm,),
        in_specs=[pl.BlockSpec((tm, d), row),
                  pl.BlockSpec((1, 1, d), bat),
                  pl.BlockSpec((1, 1, d), bat),
                  pl.BlockSpec((1, d), full2),
                  pl.BlockSpec((d, wn), full2),
                  pl.BlockSpec((tm, 128), row),
                  pl.BlockSpec((tm, 128), row),
                  pl.BlockSpec((GMLP_GROUPS, GMLP_CHUNK, GMLP_CHUNK), lambda i: (0, 0, 0)),
                  pl.BlockSpec((GMLP_CHUNK, GMLP_GROUPS), full2),
                  pl.BlockSpec((1, GMLP_W), full2),
                  pl.BlockSpec((1, GMLP_W), full2)],
        out_specs=[pl.BlockSpec((tm, 3 * ATTN_W), row),
                   pl.BlockSpec((tm, GMLP_W), row)],
        out_shape=[jax.ShapeDtypeStruct((n, 3 * ATTN_W), BF16),
                   jax.ShapeDtypeStruct((n, GMLP_W), BF16)],
        compiler_params=_ARB(1),
        name="inproj_rope_gmlp",
    )(h, sc, sh, g, w, cos, sin, w_s, b_s_t, ln_g, ln_b)


def _attn_kernel(q_ref, k_ref, v_ref, kp_ref, vp_ref, o_ref, lse_ref):
    qb = q_ref.shape[1]
    first_key = jnp.where(pl.program_id(2) == 0, WIN_BLOCK, 0)
    qi = lax.broadcasted_iota(I32, (WIN_BLOCK, 2 * WIN_BLOCK), 0)
    kj = lax.broadcasted_iota(I32, (WIN_BLOCK, 2 * WIN_BLOCK), 1)
    band = (kj >= qi) & (kj <= qi + WIN_BLOCK)
    for i in range(qb // WIN_BLOCK):
        rs = slice(i * WIN_BLOCK, (i + 1) * WIN_BLOCK)
        q = q_ref[0, rs, :]
        if i == 0:
            k_prev, v_prev = kp_ref[0], vp_ref[0]
            valid = band & (kj >= first_key)
        else:
            ps = slice((i - 1) * WIN_BLOCK, i * WIN_BLOCK)
            k_prev, v_prev = k_ref[0, ps, :], v_ref[0, ps, :]
            valid = band
        k_cat = jnp.concatenate([k_prev, k_ref[0, rs, :]], axis=0)
        v_cat = jnp.concatenate([v_prev, v_ref[0, rs, :]], axis=0)
        outs, lses = [], []
        for h in range(HEADS_PER_GROUP):
            hs = slice(h * HEAD_DIM, (h + 1) * HEAD_DIM)
            s = lax.dot_general(q[:, hs], k_cat[:, hs], (((1,), (1,)), ((), ())),
                                preferred_element_type=F32)
            s = jnp.where(valid, s, NEG_INF)
            m = jnp.max(s, axis=-1, keepdims=True)
            p = jnp.exp(s - m)
            den = jnp.sum(p, axis=-1, keepdims=True)
            o = _bdot(p.astype(BF16), v_cat[:, hs]) / den
            outs.append(o)
            lses.append(jnp.broadcast_to(m + jnp.log(den), (WIN_BLOCK, HEAD_DIM)))
        o_ref[0, rs, :] = jnp.concatenate(outs, axis=1).astype(o_ref.dtype)
        lse_ref[0, rs, :] = jnp.concatenate(lses, axis=1)


def _attention_group(qkv, g, batch, seq):
    r = ATTN_GROUPS[g][1]
    sub_len = seq // r
    qb = min(512, sub_len)
    n_tiles = sub_len // qb
    per_tile = qb // WIN_BLOCK
    n_parts = 3 * N_GROUPS
    view = qkv.reshape(batch, sub_len, r * 3 * ATTN_W)

    def cur(part):
        return pl.BlockSpec((1, qb, GROUP_W),
                            lambda b, j, n: (b, n, j * n_parts + g * 3 + part))

    def prev(part):
        return pl.BlockSpec((1, WIN_BLOCK, GROUP_W),
                            lambda b, j, n: (b, jnp.maximum(n * per_tile - 1, 0),
                                             j * n_parts + g * 3 + part))

    out_spec = pl.BlockSpec((1, qb, GROUP_W), lambda b, j, n: (b, n, j))
    o, lse = pl.pallas_call(
        _attn_kernel,
        grid=(batch, r, n_tiles),
        in_specs=[cur(0), cur(1), cur(2), prev(1), prev(2)],
        out_specs=[out_spec, out_spec],
        out_shape=[jax.ShapeDtypeStruct((batch, sub_len, r * GROUP_W), BF16),
                   jax.ShapeDtypeStruct((batch, sub_len, r * GROUP_W), F32)],
        compiler_params=_ARB(3),
        name=f"dilated_attn_g{g}",
    )(view, view, view, view, view)
    return o.reshape(batch * seq, GROUP_W), lse.reshape(batch * seq, GROUP_W)


def _split_bf16(x):
    hi = x.astype(BF16)
    lo = (x - hi.astype(F32)).astype(BF16)
    return hi, lo


def _merge_kernel(h_ref, sc1_ref, sh1_ref, g1_ref, sc2_ref, sh2_ref, n1_ref, n2_ref,
                  wg_ref, wpa_ref, wpg_ref, wo_ref,
                  o0_ref, o1_ref, o2_ref, l0_ref, l1_ref, l2_ref, gm_ref,
                  wr_ref, br_ref,
                  hn_ref, f_ref, idx_ref, gate_ref, rank_ref, cnt_ref, carry_ref):
    tm, d = h_ref.shape
    step = pl.program_id(0)

    @pl.when(step == 0)
    def _():
        carry_ref[...] = jnp.zeros_like(carry_ref)

    h = h_ref[...]
    a = _modulated_norm(h, n1_ref[...], sc1_ref[0], sh1_ref[0]).astype(BF16)
    gates = jax.nn.sigmoid(_bdot(a, wg_ref[...]))

    lses = [l0_ref[...], l1_ref[...], l2_ref[...]]
    outs = [o0_ref[...], o1_ref[...], o2_ref[...]]
    m = jnp.maximum(jnp.maximum(lses[0], lses[1]), lses[2])
    es = [jnp.exp(l - m) for l in lses]
    den = es[0] + es[1] + es[2]
    attn = (es[0] * outs[0].astype(F32) + es[1] * outs[1].astype(F32)
            + es[2] * outs[2].astype(F32)) / den

    merged = (gates[:, :d] * _bdot(attn.astype(BF16), wpa_ref[...])
              + gates[:, d:] * _bdot(gm_ref[...], wpg_ref[...]))
    hn = h + g1_ref[0] * _bdot(merged.astype(BF16), wo_ref[...])
    hn_ref[...] = hn

    f = _modulated_norm(hn, n2_ref[...], sc2_ref[0], sh2_ref[0])
    f_ref[...] = f

    f_hi, f_lo = _split_bf16(f)
    w_hi, w_lo = _split_bf16(wr_ref[...])
    nt = (((1,), (1,)), ((), ()))
    logits = (lax.dot_general(w_hi, f_hi, nt, preferred_element_type=F32)
              + lax.dot_general(w_hi, f_lo, nt, preferred_element_type=F32)
              + lax.dot_general(w_lo, f_hi, nt, preferred_element_type=F32)) + br_ref[...]

    n_e = logits.shape[0]
    erow = lax.broadcasted_iota(I32, (n_e, tm), 0).astype(F32)
    vals, idxs, sels = [], [], []
    cur = logits
    for _ in range(TOP_K):
        mx = jnp.max(cur, axis=0, keepdims=True)
        ix = jnp.min(jnp.where(cur == mx, erow, float(n_e)), axis=0, keepdims=True)
        sel = erow == ix
        vals.append(mx)
        idxs.append(ix)
        sels.append(sel)
        cur = jnp.where(sel, -jnp.inf, cur)
    exps = [jnp.exp(v - vals[0]) for v in vals]
    tot = exps[0] + exps[1] + exps[2] + exps[3]
    gate_ref[...] = jnp.concatenate([e / tot for e in exps], axis=0)
    idx_ref[...] = jnp.concatenate(idxs, axis=0).astype(I32)

    onehot = (sels[0] | sels[1] | sels[2] | sels[3])
    oh = jnp.where(onehot, 1.0, 0.0)
    t_r = lax.broadcasted_iota(I32, (tm, tm), 0)
    t_c = lax.broadcasted_iota(I32, (tm, tm), 1)
    upper = jnp.where(t_r < t_c, 1.0, 0.0).astype(BF16)
    prefix = _bdot(oh.astype(BF16), upper) + carry_ref[:, 0:1]
    ranks = [jnp.sum(jnp.where(s, prefix, 0.0), axis=0, keepdims=True) for s in sels]
    rank_ref[...] = jnp.concatenate(ranks, axis=0).astype(I32)
    new_carry = carry_ref[...] + jnp.sum(oh, axis=1, keepdims=True)
    carry_ref[...] = new_carry
    cnt_ref[...] = new_carry.astype(I32)


def _merge_route(h, mods, n1, n2, wg, wpa, wpg, wo, outs, lses, gm, wr, br, seq):
    n, d = h.shape
    tm = TOKEN_TILE
    per_b = seq // tm
    row = lambda i: (i, 0)
    bat = lambda i: (i // per_b, 0, 0)
    full2 = lambda i: (0, 0)
    col = lambda i: (0, i)
    sc1, sh1, g1, sc2, sh2 = mods
    in_specs = ([pl.BlockSpec((tm, d), row)]
                + [pl.BlockSpec((1, 1, d), bat)] * 5
                + [pl.BlockSpec((1, d), full2)] * 2
                + [pl.BlockSpec(wg.shape, full2), pl.BlockSpec(wpa.shape, full2),
                   pl.BlockSpec(wpg.shape, full2), pl.BlockSpec(wo.shape, full2)]
                + [pl.BlockSpec((tm, GROUP_W), row)] * 6
                + [pl.BlockSpec((tm, GMLP_W), row)]
                + [pl.BlockSpec(wr.shape, full2), pl.BlockSpec(br.shape, full2)])
    out_specs = [pl.BlockSpec((tm, d), row), pl.BlockSpec((tm, d), row),
                 pl.BlockSpec((TOP_K, tm), col), pl.BlockSpec((TOP_K, tm), col),
                 pl.BlockSpec((TOP_K, tm), col), pl.BlockSpec((N_EXPERTS, 128), full2)]
    out_shape = [jax.ShapeDtypeStruct((n, d), F32), jax.ShapeDtypeStruct((n, d), F32),
                 jax.ShapeDtypeStruct((TOP_K, n), I32), jax.ShapeDtypeStruct((TOP_K, n), F32),
                 jax.ShapeDtypeStruct((TOP_K, n), I32),
                 jax.ShapeDtypeStruct((N_EXPERTS, 128), I32)]
    return pl.pallas_call(
        _merge_kernel,
        grid=(n // tm,),
        in_specs=in_specs,
        out_specs=out_specs,
        out_shape=out_shape,
        scratch_shapes=[pltpu.VMEM((N_EXPERTS, 128), F32)],
        compiler_params=_ARB(1),
        name="merge_proj_route",
    )(h, sc1, sh1, g1, sc2, sh2, n1, n2, wg, wpa, wpg, wo, *outs, *lses, gm, wr, br)


def _row_copy(src, dst, sem):
    return pltpu.make_async_copy(src, dst, sem)


def _dispatch_kernel(zb_ref, nz_ref, dest_ref, f_ref, buf_ref, zero_ref, sem, zsem):
    t_rows = f_ref.shape[0]

    @pl.when(pl.program_id(0) == 0)
    def _():
        zero_ref[...] = jnp.zeros_like(zero_ref)

        def zero_copy(j):
            start = pl.multiple_of(zb_ref[j] * EXPERT_BLOCK, EXPERT_BLOCK)
            return _row_copy(zero_ref, buf_ref.at[pl.ds(start, EXPERT_BLOCK)], zsem)

        lax.fori_loop(0, nz_ref[0], lambda j, c: (zero_copy(j).start(), c)[1], 0)
        lax.fori_loop(0, nz_ref[0], lambda j, c: (zero_copy(j).wait(), c)[1], 0)

    def issue(t, carry):
        for k in range(TOP_K):
            slot = dest_ref[0, k, t]
            _row_copy(f_ref.at[pl.ds(t, 1)], buf_ref.at[pl.ds(slot, 1)], sem).start()
        return carry

    lax.fori_loop(0, t_rows, issue, 0)
    for _ in range(TOP_K):
        _row_copy(f_ref, buf_ref.at[pl.ds(0, t_rows)], sem).wait()


def _dispatch(zero_blocks, n_zero, dest_tiles, f, n_blocks):
    n, d = f.shape
    t = ROW_TILE
    return pl.pallas_call(
        _dispatch_kernel,
        grid_spec=pltpu.PrefetchScalarGridSpec(
            num_scalar_prefetch=2, grid=(n // t,),
            in_specs=[pl.BlockSpec((1, TOP_K, t), lambda i, zb, nz: (i, 0, 0),
                                   memory_space=pltpu.SMEM),
                      pl.BlockSpec((t, d), lambda i, zb, nz: (i, 0))],
            out_specs=pl.BlockSpec(memory_space=pl.ANY),
            scratch_shapes=[pltpu.VMEM((EXPERT_BLOCK, d), F32),
                            pltpu.SemaphoreType.DMA(()), pltpu.SemaphoreType.DMA(())]),
        out_shape=jax.ShapeDtypeStruct((n_blocks * EXPERT_BLOCK, d), F32),
        compiler_params=_ARB(1),
        name="dispatch_rows",
    )(zero_blocks, n_zero, dest_tiles, f)


def _expert_kernel(be_ref, nv_ref, x_ref, w1g_ref, w1l_ref, b1g_ref, b1l_ref,
                   w2_ref, b2_ref, o_ref):
    del be_ref
    live = pl.program_id(0) < nv_ref[0]

    @pl.when(live)
    def _():
        x = x_ref[...].astype(BF16)
        x_glu = jnp.minimum(_bdot(x, w1g_ref[0]) + b1g_ref[0], SWIGLU_LIMIT)
        x_lin = jnp.clip(_bdot(x, w1l_ref[0]) + b1l_ref[0], -SWIGLU_LIMIT, SWIGLU_LIMIT)
        act = x_glu * jax.nn.sigmoid(SWIGLU_ALPHA * x_glu) * (x_lin + 1.0)
        o_ref[...] = _bdot(act.astype(BF16), w2_ref[0]) + b2_ref[0]

    @pl.when(jnp.logical_not(live))
    def _():
        o_ref[...] = jnp.zeros_like(o_ref)


def _experts(block_e, n_valid, buf, w1g, w1l, b1g, b1l, w2, b2, n_blocks):
    d = buf.shape[1]
    ff = w1g.shape[2]
    xmap = lambda i, be, nv: (jnp.where(i < nv[0], i, 0), 0)
    wmap = lambda i, be, nv: (be[i], 0, 0)
    return pl.pallas_call(
        _expert_kernel,
        grid_spec=pltpu.PrefetchScalarGridSpec(
            num_scalar_prefetch=2, grid=(n_blocks,),
            in_specs=[pl.BlockSpec((EXPERT_BLOCK, d), xmap),
                      pl.BlockSpec((1, d, ff), wmap), pl.BlockSpec((1, d, ff), wmap),
                      pl.BlockSpec((1, 1, ff), wmap), pl.BlockSpec((1, 1, ff), wmap),
                      pl.BlockSpec((1, ff, d), wmap), pl.BlockSpec((1, 1, d), wmap)],
            out_specs=pl.BlockSpec((EXPERT_BLOCK, d), lambda i, be, nv: (i, 0))),
        out_shape=jax.ShapeDtypeStruct((n_blocks * EXPERT_BLOCK, d), F32),
        compiler_params=_ARB(1),
        name="expert_ffn",
    )(block_e, n_valid, buf, w1g, w1l, b1g, b1l, w2, b2)


def _combine_kernel(dest_ref, gate_ref, h_ref, g2_ref, fin_ref, obuf_ref, out_ref,
                    rows_ref, sem, *, final_norm):
    t_rows = h_ref.shape[0]

    def issue(t, carry):
        for k in range(TOP_K):
            slot = dest_ref[0, k, t]
            _row_copy(obuf_ref.at[pl.ds(slot, 1)], rows_ref.at[k, pl.ds(t, 1)], sem).start()
        return carry

    lax.fori_loop(0, t_rows, issue, 0)
    for k in range(TOP_K):
        _row_copy(obuf_ref.at[pl.ds(0, t_rows)], rows_ref.at[k], sem).wait()

    gate = gate_ref[...]
    y = gate[:, 0:1] * rows_ref[0]
    for k in range(1, TOP_K):
        y = y + gate[:, k:k + 1] * rows_ref[k]
    hn = h_ref[...] + g2_ref[0] * y
    if final_norm:
        ms = jnp.mean(hn * hn, axis=-1, keepdims=True)
        hn = hn * lax.rsqrt(ms + RMS_EPS) * fin_ref[...]
    out_ref[...] = hn


def _combine(dest_tiles, gate_t, h, g2, final_g, obuf, seq, final_norm):
    n, d = h.shape
    t = ROW_TILE
    per_b = seq // t
    return pl.pallas_call(
        functools.partial(_combine_kernel, final_norm=final_norm),
        grid=(n // t,),
        in_specs=[pl.BlockSpec((1, TOP_K, t), lambda i: (i, 0, 0), memory_space=pltpu.SMEM),
                  pl.BlockSpec((t, TOP_K), lambda i: (i, 0)),
                  pl.BlockSpec((t, d), lambda i: (i, 0)),
                  pl.BlockSpec((1, 1, d), lambda i: (i // per_b, 0, 0)),
                  pl.BlockSpec((1, d), lambda i: (0, 0)),
                  pl.BlockSpec(memory_space=pl.ANY)],
        out_specs=pl.BlockSpec((t, d), lambda i: (i, 0)),
        out_shape=jax.ShapeDtypeStruct((n, d), F32),
        scratch_shapes=[pltpu.VMEM((TOP_K, t, d), F32), pltpu.SemaphoreType.DMA(())],
        compiler_params=_ARB(1),
        name="combine_rows",
    )(dest_tiles, gate_t, h, g2, final_g, obuf)


def _slot_plan(counts, idx, rank, n_blocks):
    blk = EXPERT_BLOCK
    padded = (counts + blk - 1) // blk * blk
    pad_end = jnp.cumsum(padded)
    pad_start = pad_end - padded
    dest = pad_start[idx] + rank
    n_valid = pad_end[-1] // blk
    block_start = jnp.arange(n_blocks, dtype=I32) * blk
    block_e = jnp.minimum(jnp.searchsorted(pad_end, block_start, side="right"),
                          N_EXPERTS - 1).astype(I32)
    last_e = block_e[jnp.maximum(n_valid - 1, 0)]
    block_e = jnp.where(jnp.arange(n_blocks) < n_valid, block_e, last_e)
    tails = jnp.where(counts > 0, pad_end // blk - 1, -1)
    spare = jnp.arange(n_blocks - N_EXPERTS, n_blocks)
    cand = jnp.concatenate([tails, jnp.where(spare >= n_valid, spare, -1)]).astype(I32)
    zero_blocks = cand[jnp.argsort(cand < 0, stable=True)]
    n_zero = jnp.sum(cand >= 0).reshape(1).astype(I32)
    return (dest.astype(I32), block_e, n_valid.reshape(1).astype(I32),
            jnp.maximum(zero_blocks, 0), n_zero)


def _row_tiles(x):
    k, n = x.shape
    return x.reshape(k, n // ROW_TILE, ROW_TILE).transpose(1, 0, 2)


def kernel(x, c, positions, w_ada, b_ada, norm1_g, w_in, w_s, b_s, ln_g, ln_b, w_pa, w_pg,
           w_o, norm2_g, w_router, b_router, w1, b1, w2, b2, final_g):
    batch, seq, d = x.shape
    depth = w_ada.shape[0]
    n = batch * seq
    n_blocks = (n * TOP_K) // EXPERT_BLOCK + N_EXPERTS

    mod = _modulation(c, w_ada, b_ada)
    cos, sin = _rope_tables(positions)

    qkv_cols = jnp.concatenate(
        [jnp.arange(GROUP_W) + part * ATTN_W + g * GROUP_W
         for g in range(N_GROUPS) for part in range(3)])
    n_mix = 3 * ATTN_W + 2 * GMLP_W
    mix_cols = jnp.concatenate([qkv_cols, jnp.arange(3 * ATTN_W, n_mix)])

    h = x.reshape(n, d)
    for l in range(depth):
        ml = mod[l].reshape(batch, 6, 1, d)
        sh1, sc1, g1, sh2, sc2, g2 = [ml[:, i] for i in range(6)]
        w_mix = w_in[l][:, mix_cols].astype(BF16)
        w_gate = w_in[l, :, n_mix:].astype(BF16)

        qkv, gm = _inproj(h, sc1, sh1, norm1_g[l].reshape(1, d), w_mix, cos, sin,
                          w_s[l], b_s[l].T, ln_g[l].reshape(1, -1), ln_b[l].reshape(1, -1), seq)
        outs, lses = zip(*[_attention_group(qkv, g, batch, seq) for g in range(N_GROUPS)])

        h, f, idx, gate, rank, counts = _merge_route(
            h, (sc1, sh1, g1, sc2, sh2), norm1_g[l].reshape(1, d), norm2_g[l].reshape(1, d),
            w_gate, w_pa[l].astype(BF16), w_pg[l].astype(BF16), w_o[l].astype(BF16),
            outs, lses, gm, w_router[l].T, b_router[l].reshape(-1, 1), seq)

        dest, block_e, n_valid, zero_blocks, n_zero = _slot_plan(counts[:, 0], idx, rank, n_blocks)
        dest_tiles = _row_tiles(dest)
        buf = _dispatch(zero_blocks, n_zero, dest_tiles, f, n_blocks)

        w1g = w1[l, :, :, 0::2].astype(BF16)
        w1l = w1[l, :, :, 1::2].astype(BF16)
        b1g = b1[l, :, None, 0::2]
        b1l = b1[l, :, None, 1::2]
        obuf = _experts(block_e, n_valid, buf, w1g, w1l, b1g, b1l,
                        w2[l].astype(BF16), b2[l][:, None, :], n_blocks)

        h = _combine(dest_tiles, gate.T, h, g2, final_g.reshape(1, d), obuf, seq,
                     final_norm=(l == depth - 1))
    return h.reshape(batch, seq, d)
```

```python
import functools

import jax
import jax.numpy as jnp
from jax import lax
from jax.experimental import pallas as pl
from jax.experimental.pallas import tpu as pltpu

F32 = jnp.float32
BF16 = jnp.bfloat16
I32 = jnp.int32

HEAD_DIM = 64
HEADS_PER_GROUP = 4
GROUP_W = HEADS_PER_GROUP * HEAD_DIM
ATTN_GROUPS = ((128, 1), (512, 4), (2048, 16))
N_GROUPS = len(ATTN_GROUPS)
ATTN_W = N_GROUPS * GROUP_W
WIN_BLOCK = 128
ROPE_THETA = 10000.0
GMLP_CHUNK = 128
GMLP_GROUPS = 4
GMLP_W = GMLP_GROUPS * 128
N_EXPERTS = 32
TOP_K = 4
SWIGLU_ALPHA = 1.702
SWIGLU_LIMIT = 7.0
RMS_EPS = 1e-5
LN_EPS = 1e-5
NEG_INF = -1e30

TOKEN_TILE = 512
EXPERT_BLOCK = 256
ROW_TILE = 256
VMEM_LIMIT = 56 * 1024 * 1024

_ARB = lambda n: pltpu.CompilerParams(dimension_semantics=("arbitrary",) * n,
                                      vmem_limit_bytes=VMEM_LIMIT)


def _bdot(a, b):
    return jnp.dot(a, b, preferred_element_type=F32)


def _mod_kernel(c_ref, w_ref, b_ref, o_ref):
    c = c_ref[...]
    s = c * jax.nn.sigmoid(c)
    o_ref[0] = jnp.dot(s, w_ref[0], precision=lax.Precision.HIGHEST,
                       preferred_element_type=F32) + b_ref[0]


def _modulation(c, w_ada, b_ada):
    depth, d, six_d = w_ada.shape
    b = c.shape[0]
    n_col = six_d // d
    return pl.pallas_call(
        _mod_kernel,
        grid=(depth, n_col),
        in_specs=[pl.BlockSpec((b, d), lambda l, j: (0, 0)),
                  pl.BlockSpec((1, d, d), lambda l, j: (l, 0, j)),
                  pl.BlockSpec((1, 1, d), lambda l, j: (l, 0, j))],
        out_specs=pl.BlockSpec((1, b, d), lambda l, j: (l, 0, j)),
        out_shape=jax.ShapeDtypeStruct((depth, b, six_d), F32),
        compiler_params=_ARB(2),
        name="adaln_mod",
    )(c, w_ada, b_ada.reshape(depth, 1, six_d))


def _rope_kernel(pos_ref, freq_ref, sign_ref, cos_ref, sin_ref):
    ang = pos_ref[...].astype(F32) * freq_ref[...]
    cos_ref[...] = jnp.cos(ang)
    sin_ref[...] = jnp.sin(ang) * sign_ref[...]


def _rope_tables(positions):
    n = positions.size
    half = HEAD_DIM // 2
    inv_freq = ROPE_THETA ** (-jnp.arange(half, dtype=F32) / half)
    freq = jnp.tile(inv_freq, 4).reshape(1, 128)
    sign = jnp.tile(jnp.concatenate([-jnp.ones((half,), F32), jnp.ones((half,), F32)]), 2)
    sign = sign.reshape(1, 128)
    tile = 1024
    return pl.pallas_call(
        _rope_kernel,
        grid=(n // tile,),
        in_specs=[pl.BlockSpec((tile, 1), lambda i: (i, 0)),
                  pl.BlockSpec((1, 128), lambda i: (0, 0)),
                  pl.BlockSpec((1, 128), lambda i: (0, 0))],
        out_specs=[pl.BlockSpec((tile, 128), lambda i: (i, 0))] * 2,
        out_shape=[jax.ShapeDtypeStruct((n, 128), F32)] * 2,
        compiler_params=_ARB(1),
        name="rope_tables",
    )(positions.reshape(n, 1), freq, sign)


def _modulated_norm(x, g, sc, sh):
    ms = jnp.mean(x * x, axis=-1, keepdims=True)
    y = x * lax.rsqrt(ms + RMS_EPS) * g
    return y * (1.0 + sc) + sh


def _inproj_kernel(h_ref, sc_ref, sh_ref, g_ref, w_ref, cos_ref, sin_ref,
                   ws_ref, bs_ref, lng_ref, lnb_ref, qkv0_ref, qkv1_ref, qkv2_ref, gm_ref,
                   slab_ref):
    tm = h_ref.shape[0]
    a = _modulated_norm(h_ref[...], g_ref[...], sc_ref[0], sh_ref[0]).astype(BF16)
    cos = cos_ref[...]
    sin = sin_ref[...]
    lane = lax.broadcasted_iota(I32, (tm, 128), 1)
    first_half = (lane % HEAD_DIM) < (HEAD_DIM // 2)

    def rope(x):
        swapped = jnp.where(first_half, pltpu.roll(x, 96, 1), pltpu.roll(x, 32, 1))
        return x * cos + swapped * sin

    for g, out_ref in enumerate((qkv0_ref, qkv1_ref, qkv2_ref)):
        r = ATTN_GROUPS[g][1]
        for part in range(3):
            c0 = part * ATTN_W + g * GROUP_W
            y = _bdot(a, w_ref[:, c0:c0 + GROUP_W])
            halves = [y[:, :128], y[:, 128:]]
            if part < 2:
                halves = [rope(x) for x in halves]
            if part == 0:
                halves = [x * (HEAD_DIM ** -0.5) for x in halves]
            for c, x in enumerate(halves):
                o0 = part * GROUP_W + c * 128
                if r == 1:
                    out_ref[0, 0, :, o0:o0 + 128] = x.astype(BF16)
                else:
                    slab_ref[...] = x
                    for j in range(r):
                        rows = slab_ref[pl.ds(j, tm // r, stride=r), :]
                        out_ref[0, j, :, o0:o0 + 128] = rows.astype(BF16)

    c0 = 3 * ATTN_W
    u = jax.nn.gelu(_bdot(a, w_ref[:, c0:c0 + GMLP_W]))
    v = jax.nn.gelu(_bdot(a, w_ref[:, c0 + GMLP_W:c0 + 2 * GMLP_W]))
    mu = jnp.mean(v, axis=-1, keepdims=True)
    var = jnp.mean(jnp.square(v - mu), axis=-1, keepdims=True)
    v = ((v - mu) * lax.rsqrt(var + LN_EPS) * lng_ref[...] + lnb_ref[...]).astype(BF16)
    row = lax.broadcasted_iota(I32, (GMLP_CHUNK, GMLP_CHUNK), 0)
    col = lax.broadcasted_iota(I32, (GMLP_CHUNK, GMLP_CHUNK), 1)
    tril = col <= row
    for g in range(GMLP_GROUPS):
        wsg = jnp.where(tril, ws_ref[g], 0.0).astype(BF16)
        bias = bs_ref[:, g:g + 1]
        cs = slice(g * 128, (g + 1) * 128)
        for c in range(tm // GMLP_CHUNK):
            rs = slice(c * GMLP_CHUNK, (c + 1) * GMLP_CHUNK)
            mixed = _bdot(wsg, v[rs, cs]) + bias
            gm_ref[rs, cs] = (u[rs, cs] * mixed).astype(BF16)


def _inproj(h, sc, sh, g, w, cos, sin, w_s, b_s_t, ln_g, ln_b, seq):
    n, d = h.shape
    tm = TOKEN_TILE
    per_b = seq // tm
    batch = n // seq
    wn = w.shape[1]
    row = lambda i: (i, 0)
    bat = lambda i: (i // per_b, 0, 0)
    full2 = lambda i: (0, 0)
    res = lambda i: (i // per_b, 0, i % per_b, 0)
    qkv_specs = [pl.BlockSpec((1, r, tm // r, 3 * GROUP_W), res) for _, r in ATTN_GROUPS]
    qkv_shapes = [jax.ShapeDtypeStruct((batch, r, seq // r, 3 * GROUP_W), BF16)
                  for _, r in ATTN_GROUPS]
    return pl.pallas_call(
        _inproj_kernel,
        grid=(n // tm,),
        in_specs=[pl.BlockSpec((tm, d), row),
                  pl.BlockSpec((1, 1, d), bat),
                  pl.BlockSpec((1, 1, d), bat),
                  pl.BlockSpec((1, d), full2),
                  pl.BlockSpec((d, wn), full2),
                  pl.BlockSpec((tm, 128), row),
                  pl.BlockSpec((tm, 128), row),
                  pl.BlockSpec((GMLP_GROUPS, GMLP_CHUNK, GMLP_CHUNK), lambda i: (0, 0, 0)),
                  pl.BlockSpec((GMLP_CHUNK, GMLP_GROUPS), full2),
                  pl.BlockSpec((1, GMLP_W), full2),
                  pl.BlockSpec((1, GMLP_W), full2)],
        out_specs=qkv_specs + [pl.BlockSpec((tm, GMLP_W), row)],
        out_shape=qkv_shapes + [jax.ShapeDtypeStruct((n, GMLP_W), BF16)],
        scratch_shapes=[pltpu.VMEM((tm, 128), F32)],
        compiler_params=_ARB(1),
        name="inproj_rope_gmlp",
    )(h, sc, sh, g, w, cos, sin, w_s, b_s_t, ln_g, ln_b)


def _attn_kernel(q_ref, k_ref, v_ref, kp_ref, vp_ref, o_ref, lse_ref):
    q_ref, k_ref, v_ref, kp_ref, vp_ref, o_ref, lse_ref = (
        x.at[0] for x in (q_ref, k_ref, v_ref, kp_ref, vp_ref, o_ref, lse_ref))
    qb = q_ref.shape[1]
    first_key = jnp.where(pl.program_id(2) == 0, WIN_BLOCK, 0)
    qi = lax.broadcasted_iota(I32, (WIN_BLOCK, 2 * WIN_BLOCK), 0)
    kj = lax.broadcasted_iota(I32, (WIN_BLOCK, 2 * WIN_BLOCK), 1)
    band = (kj >= qi) & (kj <= qi + WIN_BLOCK)
    for i in range(qb // WIN_BLOCK):
        rs = slice(i * WIN_BLOCK, (i + 1) * WIN_BLOCK)
        q = q_ref[0, rs, :]
        if i == 0:
            k_prev, v_prev = kp_ref[0], vp_ref[0]
            valid = band & (kj >= first_key)
        else:
            ps = slice((i - 1) * WIN_BLOCK, i * WIN_BLOCK)
            k_prev, v_prev = k_ref[0, ps, :], v_ref[0, ps, :]
            valid = band
        k_cat = jnp.concatenate([k_prev, k_ref[0, rs, :]], axis=0)
        v_cat = jnp.concatenate([v_prev, v_ref[0, rs, :]], axis=0)
        outs, lses = [], []
        for h in range(HEADS_PER_GROUP):
            hs = slice(h * HEAD_DIM, (h + 1) * HEAD_DIM)
            s = lax.dot_general(q[:, hs], k_cat[:, hs], (((1,), (1,)), ((), ())),
                                preferred_element_type=F32)
            s = jnp.where(valid, s, NEG_INF)
            m = jnp.max(s, axis=-1, keepdims=True)
            p = jnp.exp(s - m)
            den = jnp.sum(p, axis=-1, keepdims=True)
            o = _bdot(p.astype(BF16), v_cat[:, hs]) / den
            outs.append(o)
            lses.append(jnp.broadcast_to(m + jnp.log(den), (WIN_BLOCK, HEAD_DIM)))
        o_ref[0, rs, :] = jnp.concatenate(outs, axis=1).astype(o_ref.dtype)
        lse_ref[0, rs, :] = jnp.concatenate(lses, axis=1)


def _attention_group(qkv, g):
    batch, r, sub_len, _ = qkv.shape
    qb = min(512, sub_len)
    n_tiles = sub_len // qb
    per_tile = qb // WIN_BLOCK

    def cur(part):
        return pl.BlockSpec((1, 1, qb, GROUP_W), lambda b, j, n: (b, j, n, part))

    def prev(part):
        return pl.BlockSpec((1, 1, WIN_BLOCK, GROUP_W),
                            lambda b, j, n: (b, j, jnp.maximum(n * per_tile - 1, 0), part))

    out_spec = pl.BlockSpec((1, 1, qb, GROUP_W), lambda b, j, n: (b, j, n, 0))
    return pl.pallas_call(
        _attn_kernel,
        grid=(batch, r, n_tiles),
        in_specs=[cur(0), cur(1), cur(2), prev(1), prev(2)],
        out_specs=[out_spec, out_spec],
        out_shape=[jax.ShapeDtypeStruct((batch, r, sub_len, GROUP_W), BF16),
                   jax.ShapeDtypeStruct((batch, r, sub_len, GROUP_W), F32)],
        compiler_params=_ARB(3),
        name=f"dilated_attn_g{g}",
    )(qkv, qkv, qkv, qkv, qkv)


def _split_bf16(x):
    hi = x.astype(BF16)
    lo = (x - hi.astype(F32)).astype(BF16)
    return hi, lo


def _merge_kernel(h_ref, sc1_ref, sh1_ref, g1_ref, sc2_ref, sh2_ref, n1_ref, n2_ref,
                  wg_ref, wpa_ref, wpg_ref, wo_ref,
                  o0_ref, o1_ref, o2_ref, l0_ref, l1_ref, l2_ref, gm_ref,
                  wr_ref, br_ref,
                  hn_ref, f_ref, idx_ref, gate_ref, rank_ref, cnt_ref, carry_ref, slab_ref):
    tm, d = h_ref.shape
    step = pl.program_id(0)

    @pl.when(step == 0)
    def _():
        carry_ref[...] = jnp.zeros_like(carry_ref)

    h = h_ref[...]
    a = _modulated_norm(h, n1_ref[...], sc1_ref[0], sh1_ref[0]).astype(BF16)
    gates = jax.nn.sigmoid(_bdot(a, wg_ref[...]))

    def token_major(ref):
        r = ref.shape[1]
        if r == 1:
            return ref[0, 0].astype(F32)
        halves = []
        for c in range(GROUP_W // 128):
            for j in range(r):
                slab_ref[pl.ds(j, tm // r, stride=r), :] = (
                    ref[0, j, :, c * 128:(c + 1) * 128].astype(F32))
            halves.append(slab_ref[...])
        return jnp.concatenate(halves, axis=1)

    lses = [token_major(x) for x in (l0_ref, l1_ref, l2_ref)]
    outs = [token_major(x) for x in (o0_ref, o1_ref, o2_ref)]
    m = jnp.maximum(jnp.maximum(lses[0], lses[1]), lses[2])
    es = [jnp.exp(l - m) for l in lses]
    den = es[0] + es[1] + es[2]
    attn = (es[0] * outs[0] + es[1] * outs[1] + es[2] * outs[2]) / den

    merged = (gates[:, :d] * _bdot(attn.astype(BF16), wpa_ref[...])
              + gates[:, d:] * _bdot(gm_ref[...], wpg_ref[...]))
    hn = h + g1_ref[0] * _bdot(merged.astype(BF16), wo_ref[...])
    hn_ref[...] = hn

    f = _modulated_norm(hn, n2_ref[...], sc2_ref[0], sh2_ref[0])
    f_ref[...] = f

    f_hi, f_lo = _split_bf16(f)
    w_hi, w_lo = _split_bf16(wr_ref[...])
    nt = (((1,), (1,)), ((), ()))
    logits = (lax.dot_general(w_hi, f_hi, nt, preferred_element_type=F32)
              + lax.dot_general(w_hi, f_lo, nt, preferred_element_type=F32)
              + lax.dot_general(w_lo, f_hi, nt, preferred_element_type=F32)) + br_ref[...]

    n_e = logits.shape[0]
    erow = lax.broadcasted_iota(I32, (n_e, tm), 0).astype(F32)
    vals, idxs, sels = [], [], []
    cur = logits
    for _ in range(TOP_K):
        mx = jnp.max(cur, axis=0, keepdims=True)
        ix = jnp.min(jnp.where(cur == mx, erow, float(n_e)), axis=0, keepdims=True)
        sel = erow == ix
        vals.append(mx)
        idxs.append(ix)
        sels.append(sel)
        cur = jnp.where(sel, -jnp.inf, cur)
    exps = [jnp.exp(v - vals[0]) for v in vals]
    tot = exps[0] + exps[1] + exps[2] + exps[3]
    gate_ref[...] = jnp.concatenate([e / tot for e in exps], axis=0)
    idx_ref[...] = jnp.concatenate(idxs, axis=0).astype(I32)

    onehot = (sels[0] | sels[1] | sels[2] | sels[3])
    oh = jnp.where(onehot, 1.0, 0.0)
    t_r = lax.broadcasted_iota(I32, (tm, tm), 0)
    t_c = lax.broadcasted_iota(I32, (tm, tm), 1)
    upper = jnp.where(t_r < t_c, 1.0, 0.0).astype(BF16)
    prefix = _bdot(oh.astype(BF16), upper) + carry_ref[:, 0:1]
    ranks = [jnp.sum(jnp.where(s, prefix, 0.0), axis=0, keepdims=True) for s in sels]
    rank_ref[...] = jnp.concatenate(ranks, axis=0).astype(I32)
    new_carry = carry_ref[...] + jnp.sum(oh, axis=1, keepdims=True)
    carry_ref[...] = new_carry
    cnt_ref[...] = new_carry.astype(I32)


def _merge_route(h, mods, n1, n2, wg, wpa, wpg, wo, outs, lses, gm, wr, br, seq):
    n, d = h.shape
    tm = TOKEN_TILE
    per_b = seq // tm
    row = lambda i: (i, 0)
    bat = lambda i: (i // per_b, 0, 0)
    full2 = lambda i: (0, 0)
    col = lambda i: (0, i)
    sc1, sh1, g1, sc2, sh2 = mods
    res = lambda i: (i // per_b, 0, i % per_b, 0)
    attn_specs = [pl.BlockSpec((1, r, tm // r, GROUP_W), res) for _, r in ATTN_GROUPS]
    in_specs = ([pl.BlockSpec((tm, d), row)]
                + [pl.BlockSpec((1, 1, d), bat)] * 5
                + [pl.BlockSpec((1, d), full2)] * 2
                + [pl.BlockSpec(wg.shape, full2), pl.BlockSpec(wpa.shape, full2),
                   pl.BlockSpec(wpg.shape, full2), pl.BlockSpec(wo.shape, full2)]
                + attn_specs * 2
                + [pl.BlockSpec((tm, GMLP_W), row)]
                + [pl.BlockSpec(wr.shape, full2), pl.BlockSpec(br.shape, full2)])
    out_specs = [pl.BlockSpec((tm, d), row), pl.BlockSpec((tm, d), row),
                 pl.BlockSpec((TOP_K, tm), col), pl.BlockSpec((TOP_K, tm), col),
                 pl.BlockSpec((TOP_K, tm), col), pl.BlockSpec((N_EXPERTS, 128), full2)]
    out_shape = [jax.ShapeDtypeStruct((n, d), F32), jax.ShapeDtypeStruct((n, d), F32),
                 jax.ShapeDtypeStruct((TOP_K, n), I32), jax.ShapeDtypeStruct((TOP_K, n), F32),
                 jax.ShapeDtypeStruct((TOP_K, n), I32),
                 jax.ShapeDtypeStruct((N_EXPERTS, 128), I32)]
    return pl.pallas_call(
        _merge_kernel,
        grid=(n // tm,),
        in_specs=in_specs,
        out_specs=out_specs,
        out_shape=out_shape,
        scratch_shapes=[pltpu.VMEM((N_EXPERTS, 128), F32), pltpu.VMEM((tm, 128), F32)],
        compiler_params=_ARB(1),
        name="merge_proj_route",
    )(h, sc1, sh1, g1, sc2, sh2, n1, n2, wg, wpa, wpg, wo, *outs, *lses, gm, wr, br)


def _row_copy(src, dst, sem):
    return pltpu.make_async_copy(src, dst, sem)


def _dispatch_kernel(zb_ref, nz_ref, dest_ref, f_ref, buf_ref, zero_ref, sem, zsem):
    t_rows = f_ref.shape[0]

    @pl.when(pl.program_id(0) == 0)
    def _():
        zero_ref[...] = jnp.zeros_like(zero_ref)

        def zero_copy(j):
            start = pl.multiple_of(zb_ref[j] * EXPERT_BLOCK, EXPERT_BLOCK)
            return _row_copy(zero_ref, buf_ref.at[pl.ds(start, EXPERT_BLOCK)], zsem)

        lax.fori_loop(0, nz_ref[0], lambda j, c: (zero_copy(j).start(), c)[1], 0)
        lax.fori_loop(0, nz_ref[0], lambda j, c: (zero_copy(j).wait(), c)[1], 0)

    def issue(t, carry):
        for k in range(TOP_K):
            slot = dest_ref[0, k, t]
            _row_copy(f_ref.at[pl.ds(t, 1)], buf_ref.at[pl.ds(slot, 1)], sem).start()
        return carry

    lax.fori_loop(0, t_rows, issue, 0)
    for _ in range(TOP_K):
        _row_copy(f_ref, buf_ref.at[pl.ds(0, t_rows)], sem).wait()


def _dispatch(zero_blocks, n_zero, dest_tiles, f, n_blocks):
    n, d = f.shape
    t = ROW_TILE
    return pl.pallas_call(
        _dispatch_kernel,
        grid_spec=pltpu.PrefetchScalarGridSpec(
            num_scalar_prefetch=2, grid=(n // t,),
            in_specs=[pl.BlockSpec((1, TOP_K, t), lambda i, zb, nz: (i, 0, 0),
                                   memory_space=pltpu.SMEM),
                      pl.BlockSpec((t, d), lambda i, zb, nz: (i, 0))],
            out_specs=pl.BlockSpec(memory_space=pl.ANY),
            scratch_shapes=[pltpu.VMEM((EXPERT_BLOCK, d), F32),
                            pltpu.SemaphoreType.DMA(()), pltpu.SemaphoreType.DMA(())]),
        out_shape=jax.ShapeDtypeStruct((n_blocks * EXPERT_BLOCK, d), F32),
        compiler_params=_ARB(1),
        name="dispatch_rows",
    )(zero_blocks, n_zero, dest_tiles, f)


def _expert_kernel(be_ref, first_ref, nv_ref, x_ref, w1_ref, b1_ref, w2_ref, b2_ref, o_ref,
                   w1b_ref, w2p_ref, slab_ref):
    del be_ref
    step = pl.program_id(0)
    live = step < nv_ref[0]
    ff, d = w2p_ref.shape
    half = 64

    @pl.when(live & (first_ref[step] == 1))
    def _():
        w1b_ref[...] = w1_ref[0, 0].astype(BF16)
        for c in range(d // 128):
            cs = slice(c * 128, (c + 1) * 128)
            for k in range(ff // 128):
                r0 = k * 128
                slab_ref[pl.ds(r0, half, stride=2), :] = w2_ref[0, 0, r0:r0 + half, cs]
                slab_ref[pl.ds(r0 + 1, half, stride=2), :] = w2_ref[0, 0, r0 + half:r0 + 128, cs]
            w2p_ref[:, cs] = slab_ref[...].astype(BF16)

    @pl.when(live)
    def _():
        x = x_ref[...].astype(BF16)
        hb = _bdot(x, w1b_ref[...]) + b1_ref[0, 0]
        lane = lax.broadcasted_iota(I32, (x.shape[0], 128), 1)
        even = (lane & 1) == 0
        glu, lin = [], []
        for k in range(ff // 128):
            a = hb[:, 256 * k:256 * k + 128]
            b = hb[:, 256 * k + 128:256 * k + 256]
            glu.append(jnp.where(even, a, pltpu.roll(b, 1, 1)))
            lin.append(jnp.where(even, pltpu.roll(a, 127, 1), b))
        x_glu = jnp.minimum(jnp.concatenate(glu, axis=1), SWIGLU_LIMIT)
        x_lin = jnp.clip(jnp.concatenate(lin, axis=1), -SWIGLU_LIMIT, SWIGLU_LIMIT)
        act = x_glu * jax.nn.sigmoid(SWIGLU_ALPHA * x_glu) * (x_lin + 1.0)
        o_ref[...] = _bdot(act.astype(BF16), w2p_ref[...]) + b2_ref[0, 0]

    @pl.when(jnp.logical_not(live))
    def _():
        o_ref[...] = jnp.zeros_like(o_ref)


def _experts(layer, block_e, first, n_valid, buf, w1, b1, w2, b2, n_blocks):
    d = buf.shape[1]
    ff = w2.shape[2]
    xmap = lambda i, be, fi, nv: (jnp.where(i < nv[0], i, 0), 0)
    wmap = lambda i, be, fi, nv: (layer, be[i], 0, 0)
    return pl.pallas_call(
        _expert_kernel,
        grid_spec=pltpu.PrefetchScalarGridSpec(
            num_scalar_prefetch=3, grid=(n_blocks,),
            in_specs=[pl.BlockSpec((EXPERT_BLOCK, d), xmap),
                      pl.BlockSpec((1, 1, d, 2 * ff), wmap),
                      pl.BlockSpec((1, 1, 1, 2 * ff), wmap),
                      pl.BlockSpec((1, 1, ff, d), wmap),
                      pl.BlockSpec((1, 1, 1, d), wmap)],
            out_specs=pl.BlockSpec((EXPERT_BLOCK, d), lambda i, be, fi, nv: (i, 0)),
            scratch_shapes=[pltpu.VMEM((d, 2 * ff), BF16), pltpu.VMEM((ff, d), BF16),
                            pltpu.VMEM((ff, 128), F32)]),
        out_shape=jax.ShapeDtypeStruct((n_blocks * EXPERT_BLOCK, d), F32),
        compiler_params=_ARB(1),
        name="expert_ffn",
    )(block_e, first, n_valid, buf, w1, b1, w2, b2)


def _combine_kernel(dest_ref, gate_ref, h_ref, g2_ref, fin_ref, obuf_ref, out_ref,
                    rows_ref, sem, *, final_norm):
    t_rows = h_ref.shape[0]

    def issue(t, carry):
        for k in range(TOP_K):
            slot = dest_ref[0, k, t]
            _row_copy(obuf_ref.at[pl.ds(slot, 1)], rows_ref.at[k, pl.ds(t, 1)], sem).start()
        return carry

    lax.fori_loop(0, t_rows, issue, 0)
    for k in range(TOP_K):
        _row_copy(obuf_ref.at[pl.ds(0, t_rows)], rows_ref.at[k], sem).wait()

    gate = gate_ref[...]
    y = gate[:, 0:1] * rows_ref[0]
    for k in range(1, TOP_K):
        y = y + gate[:, k:k + 1] * rows_ref[k]
    hn = h_ref[...] + g2_ref[0] * y
    if final_norm:
        ms = jnp.mean(hn * hn, axis=-1, keepdims=True)
        hn = hn * lax.rsqrt(ms + RMS_EPS) * fin_ref[...]
    out_ref[...] = hn


def _combine(dest_tiles, gate_t, h, g2, final_g, obuf, seq, final_norm):
    n, d = h.shape
    t = ROW_TILE
    per_b = seq // t
    return pl.pallas_call(
        functools.partial(_combine_kernel, final_norm=final_norm),
        grid=(n // t,),
        in_specs=[pl.BlockSpec((1, TOP_K, t), lambda i: (i, 0, 0), memory_space=pltpu.SMEM),
                  pl.BlockSpec((t, TOP_K), lambda i: (i, 0)),
                  pl.BlockSpec((t, d), lambda i: (i, 0)),
                  pl.BlockSpec((1, 1, d), lambda i: (i // per_b, 0, 0)),
                  pl.BlockSpec((1, d), lambda i: (0, 0)),
                  pl.BlockSpec(memory_space=pl.ANY)],
        out_specs=pl.BlockSpec((t, d), lambda i: (i, 0)),
        out_shape=jax.ShapeDtypeStruct((n, d), F32),
        scratch_shapes=[pltpu.VMEM((TOP_K, t, d), F32), pltpu.SemaphoreType.DMA(())],
        compiler_params=_ARB(1),
        name="combine_rows",
    )(dest_tiles, gate_t, h, g2, final_g, obuf)


def _slot_plan(counts, idx, rank, n_blocks):
    blk = EXPERT_BLOCK
    padded = (counts + blk - 1) // blk * blk
    pad_end = jnp.cumsum(padded)
    pad_start = pad_end - padded
    dest = pad_start[idx] + rank
    n_valid = pad_end[-1] // blk
    block_start = jnp.arange(n_blocks, dtype=I32) * blk
    block_e = jnp.sum(pad_end[None, :] <= block_start[:, None], axis=1)
    last_e = jnp.sum(pad_end <= (n_valid - 1) * blk)
    block_e = jnp.where(jnp.arange(n_blocks) < n_valid, block_e, last_e).astype(I32)
    first = jnp.concatenate([jnp.ones((1,), I32),
                             (block_e[1:] != block_e[:-1]).astype(I32)])
    tails = jnp.where(counts > 0, pad_end // blk - 1, -1)
    spare = jnp.arange(n_blocks - N_EXPERTS, n_blocks)
    cand = jnp.concatenate([tails, jnp.where(spare >= n_valid, spare, -1)]).astype(I32)
    zero_blocks = cand[jnp.argsort(cand < 0, stable=True)]
    n_zero = jnp.sum(cand >= 0).reshape(1).astype(I32)
    return (dest.astype(I32), block_e, first, n_valid.reshape(1).astype(I32),
            jnp.maximum(zero_blocks, 0), n_zero)


def _row_tiles(x):
    k, n = x.shape
    return x.reshape(k, n // ROW_TILE, ROW_TILE).transpose(1, 0, 2)


def kernel(x, c, positions, w_ada, b_ada, norm1_g, w_in, w_s, b_s, ln_g, ln_b, w_pa, w_pg,
           w_o, norm2_g, w_router, b_router, w1, b1, w2, b2, final_g):
    batch, seq, d = x.shape
    depth = w_ada.shape[0]
    n = batch * seq
    n_blocks = (n * TOP_K) // EXPERT_BLOCK + N_EXPERTS

    mod = _modulation(c, w_ada, b_ada)
    cos, sin = _rope_tables(positions)

    n_mix = 3 * ATTN_W + 2 * GMLP_W
    b1r = b1.reshape(depth, N_EXPERTS, 1, -1)
    b2r = b2.reshape(depth, N_EXPERTS, 1, -1)

    h = x.reshape(n, d)
    for l in range(depth):
        ml = mod[l].reshape(batch, 6, 1, d)
        sh1, sc1, g1, sh2, sc2, g2 = [ml[:, i] for i in range(6)]
        w_mix = w_in[l, :, :n_mix].astype(BF16)
        w_gate = w_in[l, :, n_mix:].astype(BF16)

        *qkvs, gm = _inproj(h, sc1, sh1, norm1_g[l].reshape(1, d), w_mix, cos, sin,
                            w_s[l], b_s[l].T, ln_g[l].reshape(1, -1), ln_b[l].reshape(1, -1), seq)
        outs, lses = zip(*[_attention_group(qkvs[g], g) for g in range(N_GROUPS)])

        h, f, idx, gate, rank, counts = _merge_route(
            h, (sc1, sh1, g1, sc2, sh2), norm1_g[l].reshape(1, d), norm2_g[l].reshape(1, d),
            w_gate, w_pa[l].astype(BF16), w_pg[l].astype(BF16), w_o[l].astype(BF16),
            outs, lses, gm, w_router[l].T, b_router[l].reshape(-1, 1), seq)

        dest, block_e, first, n_valid, zero_blocks, n_zero = _slot_plan(
            counts[:, 0], idx, rank, n_blocks)
        dest_tiles = _row_tiles(dest)
        buf = _dispatch(zero_blocks, n_zero, dest_tiles, f, n_blocks)
        obuf = _experts(l, block_e, first, n_valid, buf, w1, b1r, w2, b2r, n_blocks)

        h = _combine(dest_tiles, gate.T, h, g2, final_g.reshape(1, d), obuf, seq,
                     final_norm=(l == depth - 1))
    return h.reshape(batch, seq, d)
```

```python
import functools

import jax
import jax.numpy as jnp
from jax import lax
from jax.experimental import pallas as pl
from jax.experimental.pallas import tpu as pltpu

F32 = jnp.float32
BF16 = jnp.bfloat16
I32 = jnp.int32

HEAD_DIM = 64
HEADS_PER_GROUP = 4
GROUP_W = HEADS_PER_GROUP * HEAD_DIM
ATTN_GROUPS = ((128, 1), (512, 4), (2048, 16))
N_GROUPS = len(ATTN_GROUPS)
ATTN_W = N_GROUPS * GROUP_W
WIN_BLOCK = 128
ROPE_THETA = 10000.0
GMLP_CHUNK = 128
GMLP_GROUPS = 4
GMLP_W = GMLP_GROUPS * 128
N_EXPERTS = 32
TOP_K = 4
SWIGLU_ALPHA = 1.702
SWIGLU_LIMIT = 7.0
RMS_EPS = 1e-5
LN_EPS = 1e-5
NEG_INF = -1e30

TOKEN_TILE = 512
EXPERT_BLOCK = 512
ROW_TILE = 256
VMEM_LIMIT = 56 * 1024 * 1024

_ARB = lambda n: pltpu.CompilerParams(dimension_semantics=("arbitrary",) * n,
                                      vmem_limit_bytes=VMEM_LIMIT)


def _bdot(a, b):
    return jnp.dot(a, b, preferred_element_type=F32)


def _split3(x):
    hi = x.astype(BF16)
    r1 = x - hi.astype(F32)
    mid = r1.astype(BF16)
    lo = (r1 - mid.astype(F32)).astype(BF16)
    return hi, mid, lo


def _dot_f32(a, b):
    a0, a1, a2 = _split3(a)
    b0, b1, b2 = _split3(b)
    small = _bdot(a0, b2) + _bdot(a1, b1) + _bdot(a2, b0)
    mid = _bdot(a0, b1) + _bdot(a1, b0)
    return (small + mid) + _bdot(a0, b0)


def _mod_kernel(c_ref, w_ref, b_ref, o_ref):
    c = c_ref[...]
    s = c * jax.nn.sigmoid(c)
    o_ref[0] = _dot_f32(s, w_ref[0]) + b_ref[0]


def _modulation(c, w_ada, b_ada):
    depth, d, six_d = w_ada.shape
    b = c.shape[0]
    n_col = six_d // d
    return pl.pallas_call(
        _mod_kernel,
        grid=(depth, n_col),
        in_specs=[pl.BlockSpec((b, d), lambda l, j: (0, 0)),
                  pl.BlockSpec((1, d, d), lambda l, j: (l, 0, j)),
                  pl.BlockSpec((1, 1, d), lambda l, j: (l, 0, j))],
        out_specs=pl.BlockSpec((1, b, d), lambda l, j: (l, 0, j)),
        out_shape=jax.ShapeDtypeStruct((depth, b, six_d), F32),
        compiler_params=_ARB(2),
        name="adaln_mod",
    )(c, w_ada, b_ada.reshape(depth, 1, six_d))


def _rope_kernel(pos_ref, freq_ref, sign_ref, cos_ref, sin_ref):
    ang = pos_ref[...].astype(F32) * freq_ref[...]
    cos_ref[...] = jnp.cos(ang)
    sin_ref[...] = jnp.sin(ang) * sign_ref[...]


def _rope_tables(positions):
    n = positions.size
    half = HEAD_DIM // 2
    inv_freq = ROPE_THETA ** (-jnp.arange(half, dtype=F32) / half)
    freq = jnp.tile(inv_freq, 4).reshape(1, 128)
    sign = jnp.tile(jnp.concatenate([-jnp.ones((half,), F32), jnp.ones((half,), F32)]), 2)
    sign = sign.reshape(1, 128)
    tile = 1024
    return pl.pallas_call(
        _rope_kernel,
        grid=(n // tile,),
        in_specs=[pl.BlockSpec((tile, 1), lambda i: (i, 0)),
                  pl.BlockSpec((1, 128), lambda i: (0, 0)),
                  pl.BlockSpec((1, 128), lambda i: (0, 0))],
        out_specs=[pl.BlockSpec((tile, 128), lambda i: (i, 0))] * 2,
        out_shape=[jax.ShapeDtypeStruct((n, 128), F32)] * 2,
        compiler_params=_ARB(1),
        name="rope_tables",
    )(positions.reshape(n, 1), freq, sign)


def _modulated_norm(x, g, sc, sh):
    ms = jnp.mean(x * x, axis=-1, keepdims=True)
    y = x * lax.rsqrt(ms + RMS_EPS) * g
    return y * (1.0 + sc) + sh


def _inproj_kernel(h_ref, sc_ref, sh_ref, g_ref, w_ref, cos_ref, sin_ref,
                   ws_ref, bs_ref, lng_ref, lnb_ref, qkv0_ref, qkv1_ref, qkv2_ref, gm_ref,
                   slab_ref):
    tm = h_ref.shape[0]
    a = _modulated_norm(h_ref[...], g_ref[...], sc_ref[0], sh_ref[0]).astype(BF16)
    cos = cos_ref[...]
    sin = sin_ref[...]
    lane = lax.broadcasted_iota(I32, (tm, 128), 1)
    first_half = (lane % HEAD_DIM) < (HEAD_DIM // 2)

    def rope(x):
        swapped = jnp.where(first_half, pltpu.roll(x, 96, 1), pltpu.roll(x, 32, 1))
        return x * cos + swapped * sin

    for g, out_ref in enumerate((qkv0_ref, qkv1_ref, qkv2_ref)):
        r = ATTN_GROUPS[g][1]
        for part in range(3):
            c0 = part * ATTN_W + g * GROUP_W
            y = _bdot(a, w_ref[:, c0:c0 + GROUP_W])
            halves = [y[:, :128], y[:, 128:]]
            if part < 2:
                halves = [rope(x) for x in halves]
            if part == 0:
                halves = [x * (HEAD_DIM ** -0.5) for x in halves]
            for c, x in enumerate(halves):
                o0 = part * GROUP_W + c * 128
                if r == 1:
                    out_ref[0, 0, :, o0:o0 + 128] = x.astype(BF16)
                else:
                    slab_ref[...] = x
                    for j in range(r):
                        rows = slab_ref[pl.ds(j, tm // r, stride=r), :]
                        out_ref[0, j, :, o0:o0 + 128] = rows.astype(BF16)

    c0 = 3 * ATTN_W
    u = jax.nn.gelu(_bdot(a, w_ref[:, c0:c0 + GMLP_W]))
    v = jax.nn.gelu(_bdot(a, w_ref[:, c0 + GMLP_W:c0 + 2 * GMLP_W]))
    mu = jnp.mean(v, axis=-1, keepdims=True)
    var = jnp.mean(jnp.square(v - mu), axis=-1, keepdims=True)
    v = ((v - mu) * lax.rsqrt(var + LN_EPS) * lng_ref[...] + lnb_ref[...]).astype(BF16)
    row = lax.broadcasted_iota(I32, (GMLP_CHUNK, GMLP_CHUNK), 0)
    col = lax.broadcasted_iota(I32, (GMLP_CHUNK, GMLP_CHUNK), 1)
    tril = col <= row
    for g in range(GMLP_GROUPS):
        wsg = jnp.where(tril, ws_ref[g], 0.0).astype(BF16)
        bias = bs_ref[:, g:g + 1]
        cs = slice(g * 128, (g + 1) * 128)
        for c in range(tm // GMLP_CHUNK):
            rs = slice(c * GMLP_CHUNK, (c + 1) * GMLP_CHUNK)
            mixed = _bdot(wsg, v[rs, cs]) + bias
            gm_ref[rs, cs] = (u[rs, cs] * mixed).astype(BF16)


def _inproj(h, sc, sh, g, w, cos, sin, w_s, b_s_t, ln_g, ln_b, seq):
    n, d = h.shape
    tm = TOKEN_TILE
    per_b = seq // tm
    batch = n // seq
    wn = w.shape[1]
    row = lambda i: (i, 0)
    bat = lambda i: (i // per_b, 0, 0)
    full2 = lambda i: (0, 0)
    res = lambda i: (i // per_b, 0, i % per_b, 0)
    qkv_specs = [pl.BlockSpec((1, r, tm // r, 3 * GROUP_W), res) for _, r in ATTN_GROUPS]
    qkv_shapes = [jax.ShapeDtypeStruct((batch, r, seq // r, 3 * GROUP_W), BF16)
                  for _, r in ATTN_GROUPS]
    return pl.pallas_call(
        _inproj_kernel,
        grid=(n // tm,),
        in_specs=[pl.BlockSpec((tm, d), row),
                  pl.BlockSpec((1, 1, d), bat),
                  pl.BlockSpec((1, 1, d), bat),
                  pl.BlockSpec((1, d), full2),
                  pl.BlockSpec((d, wn), full2),
                  pl.BlockSpec((tm, 128), row),
                  pl.BlockSpec((tm, 128), row),
                  pl.BlockSpec((GMLP_GROUPS, GMLP_CHUNK, GMLP_CHUNK), lambda i: (0, 0, 0)),
                  pl.BlockSpec((GMLP_CHUNK, GMLP_GROUPS), full2),
                  pl.BlockSpec((1, GMLP_W), full2),
                  pl.BlockSpec((1, GMLP_W), full2)],
        out_specs=qkv_specs + [pl.BlockSpec((tm, GMLP_W), row)],
        out_shape=qkv_shapes + [jax.ShapeDtypeStruct((n, GMLP_W), BF16)],
        scratch_shapes=[pltpu.VMEM((tm, 128), F32)],
        compiler_params=_ARB(1),
        name="inproj_rope_gmlp",
    )(h, sc, sh, g, w, cos, sin, w_s, b_s_t, ln_g, ln_b)


def _attn_kernel(q_ref, k_ref, v_ref, kp_ref, vp_ref, o_ref, lse_ref):
    q_ref, k_ref, v_ref, kp_ref, vp_ref, o_ref, lse_ref = (
        x.at[0] for x in (q_ref, k_ref, v_ref, kp_ref, vp_ref, o_ref, lse_ref))
    qb = q_ref.shape[1]
    first_key = jnp.where(pl.program_id(2) == 0, WIN_BLOCK, 0)
    qi = lax.broadcasted_iota(I32, (WIN_BLOCK, 2 * WIN_BLOCK), 0)
    kj = lax.broadcasted_iota(I32, (WIN_BLOCK, 2 * WIN_BLOCK), 1)
    band = (kj >= qi) & (kj <= qi + WIN_BLOCK)
    for i in range(qb // WIN_BLOCK):
        rs = slice(i * WIN_BLOCK, (i + 1) * WIN_BLOCK)
        q = q_ref[0, rs, :]
        if i == 0:
            k_prev, v_prev = kp_ref[0], vp_ref[0]
            valid = band & (kj >= first_key)
        else:
            ps = slice((i - 1) * WIN_BLOCK, i * WIN_BLOCK)
            k_prev, v_prev = k_ref[0, ps, :], v_ref[0, ps, :]
            valid = band
        k_cat = jnp.concatenate([k_prev, k_ref[0, rs, :]], axis=0)
        v_cat = jnp.concatenate([v_prev, v_ref[0, rs, :]], axis=0)
        outs, lses = [], []
        for h in range(HEADS_PER_GROUP):
            hs = slice(h * HEAD_DIM, (h + 1) * HEAD_DIM)
            s = lax.dot_general(q[:, hs], k_cat[:, hs], (((1,), (1,)), ((), ())),
                                preferred_element_type=F32)
            s = jnp.where(valid, s, NEG_INF)
            m = jnp.max(s, axis=-1, keepdims=True)
            p = jnp.exp(s - m)
            den = jnp.sum(p, axis=-1, keepdims=True)
            o = _bdot(p.astype(BF16), v_cat[:, hs]) / den
            outs.append(o)
            lses.append(jnp.broadcast_to(m + jnp.log(den), (WIN_BLOCK, HEAD_DIM)))
        o_ref[0, rs, :] = jnp.concatenate(outs, axis=1).astype(o_ref.dtype)
        lse_ref[0, rs, :] = jnp.concatenate(lses, axis=1)


def _attention_group(qkv, g):
    batch, r, sub_len, _ = qkv.shape
    qb = min(512, sub_len)
    n_tiles = sub_len // qb
    per_tile = qb // WIN_BLOCK

    def cur(part):
        return pl.BlockSpec((1, 1, qb, GROUP_W), lambda b, j, n: (b, j, n, part))

    def prev(part):
        return pl.BlockSpec((1, 1, WIN_BLOCK, GROUP_W),
                            lambda b, j, n: (b, j, jnp.maximum(n * per_tile - 1, 0), part))

    out_spec = pl.BlockSpec((1, 1, qb, GROUP_W), lambda b, j, n: (b, j, n, 0))
    return pl.pallas_call(
        _attn_kernel,
        grid=(batch, r, n_tiles),
        in_specs=[cur(0), cur(1), cur(2), prev(1), prev(2)],
        out_specs=[out_spec, out_spec],
        out_shape=[jax.ShapeDtypeStruct((batch, r, sub_len, GROUP_W), BF16),
                   jax.ShapeDtypeStruct((batch, r, sub_len, GROUP_W), F32)],
        compiler_params=_ARB(3),
        name=f"dilated_attn_g{g}",
    )(qkv, qkv, qkv, qkv, qkv)


def _split_bf16(x):
    hi = x.astype(BF16)
    lo = (x - hi.astype(F32)).astype(BF16)
    return hi, lo


def _merge_kernel(h_ref, sc1_ref, sh1_ref, g1_ref, sc2_ref, sh2_ref, n1_ref, n2_ref,
                  wg_ref, wpa_ref, wpg_ref, wo_ref,
                  o0_ref, o1_ref, o2_ref, l0_ref, l1_ref, l2_ref, gm_ref,
                  wr_ref, br_ref,
                  hn_ref, f_ref, idx_ref, gate_ref, rank_ref, cnt_ref, carry_ref, slab_ref):
    tm, d = h_ref.shape
    step = pl.program_id(0)

    @pl.when(step == 0)
    def _():
        carry_ref[...] = jnp.zeros_like(carry_ref)

    h = h_ref[...]
    a = _modulated_norm(h, n1_ref[...], sc1_ref[0], sh1_ref[0]).astype(BF16)
    gates = jax.nn.sigmoid(_bdot(a, wg_ref[...]))

    def token_major(ref):
        r = ref.shape[1]
        if r == 1:
            return ref[0, 0].astype(F32)
        halves = []
        for c in range(GROUP_W // 128):
            for j in range(r):
                slab_ref[pl.ds(j, tm // r, stride=r), :] = (
                    ref[0, j, :, c * 128:(c + 1) * 128].astype(F32))
            halves.append(slab_ref[...])
        return jnp.concatenate(halves, axis=1)

    lses = [token_major(x) for x in (l0_ref, l1_ref, l2_ref)]
    outs = [token_major(x) for x in (o0_ref, o1_ref, o2_ref)]
    m = jnp.maximum(jnp.maximum(lses[0], lses[1]), lses[2])
    es = [jnp.exp(l - m) for l in lses]
    den = es[0] + es[1] + es[2]
    attn = (es[0] * outs[0] + es[1] * outs[1] + es[2] * outs[2]) / den

    merged = (gates[:, :d] * _bdot(attn.astype(BF16), wpa_ref[...])
              + gates[:, d:] * _bdot(gm_ref[...], wpg_ref[...]))
    hn = h + g1_ref[0] * _bdot(merged.astype(BF16), wo_ref[...])
    hn_ref[...] = hn

    f = _modulated_norm(hn, n2_ref[...], sc2_ref[0], sh2_ref[0])
    f_ref[...] = f

    f_hi, f_lo = _split_bf16(f)
    w_hi, w_lo = _split_bf16(wr_ref[...])
    nt = (((1,), (1,)), ((), ()))
    logits = (lax.dot_general(w_hi, f_hi, nt, preferred_element_type=F32)
              + lax.dot_general(w_hi, f_lo, nt, preferred_element_type=F32)
              + lax.dot_general(w_lo, f_hi, nt, preferred_element_type=F32)) + br_ref[...]

    n_e = logits.shape[0]
    erow = lax.broadcasted_iota(I32, (n_e, tm), 0).astype(F32)
    vals, idxs, sels = [], [], []
    cur = logits
    for _ in range(TOP_K):
        mx = jnp.max(cur, axis=0, keepdims=True)
        ix = jnp.min(jnp.where(cur == mx, erow, float(n_e)), axis=0, keepdims=True)
        sel = erow == ix
        vals.append(mx)
        idxs.append(ix)
        sels.append(sel)
        cur = jnp.where(sel, -jnp.inf, cur)
    exps = [jnp.exp(v - vals[0]) for v in vals]
    tot = exps[0] + exps[1] + exps[2] + exps[3]
    gate_ref[...] = jnp.concatenate([e / tot for e in exps], axis=0)
    idx_ref[...] = jnp.concatenate(idxs, axis=0).astype(I32)

    onehot = (sels[0] | sels[1] | sels[2] | sels[3])
    oh = jnp.where(onehot, 1.0, 0.0)
    t_r = lax.broadcasted_iota(I32, (tm, tm), 0)
    t_c = lax.broadcasted_iota(I32, (tm, tm), 1)
    upper = jnp.where(t_r < t_c, 1.0, 0.0).astype(BF16)
    prefix = _bdot(oh.astype(BF16), upper) + carry_ref[:, 0:1]
    ranks = [jnp.sum(jnp.where(s, prefix, 0.0), axis=0, keepdims=True) for s in sels]
    rank_ref[...] = jnp.concatenate(ranks, axis=0).astype(I32)
    new_carry = carry_ref[...] + jnp.sum(oh, axis=1, keepdims=True)
    carry_ref[...] = new_carry
    cnt_ref[...] = new_carry.astype(I32)


def _merge_route(h, mods, n1, n2, wg, wpa, wpg, wo, outs, lses, gm, wr, br, seq):
    n, d = h.shape
    tm = TOKEN_TILE
    per_b = seq // tm
    row = lambda i: (i, 0)
    bat = lambda i: (i // per_b, 0, 0)
    full2 = lambda i: (0, 0)
    col = lambda i: (0, i)
    sc1, sh1, g1, sc2, sh2 = mods
    res = lambda i: (i // per_b, 0, i % per_b, 0)
    attn_specs = [pl.BlockSpec((1, r, tm // r, GROUP_W), res) for _, r in ATTN_GROUPS]
    in_specs = ([pl.BlockSpec((tm, d), row)]
                + [pl.BlockSpec((1, 1, d), bat)] * 5
                + [pl.BlockSpec((1, d), full2)] * 2
                + [pl.BlockSpec(wg.shape, full2), pl.BlockSpec(wpa.shape, full2),
                   pl.BlockSpec(wpg.shape, full2), pl.BlockSpec(wo.shape, full2)]
                + attn_specs * 2
                + [pl.BlockSpec((tm, GMLP_W), row)]
                + [pl.BlockSpec(wr.shape, full2), pl.BlockSpec(br.shape, full2)])
    out_specs = [pl.BlockSpec((tm, d), row), pl.BlockSpec((tm, d), row),
                 pl.BlockSpec((TOP_K, tm), col), pl.BlockSpec((TOP_K, tm), col),
                 pl.BlockSpec((TOP_K, tm), col), pl.BlockSpec((N_EXPERTS, 128), full2)]
    out_shape = [jax.ShapeDtypeStruct((n, d), F32), jax.ShapeDtypeStruct((n, d), F32),
                 jax.ShapeDtypeStruct((TOP_K, n), I32), jax.ShapeDtypeStruct((TOP_K, n), F32),
                 jax.ShapeDtypeStruct((TOP_K, n), I32),
                 jax.ShapeDtypeStruct((N_EXPERTS, 128), I32)]
    return pl.pallas_call(
        _merge_kernel,
        grid=(n // tm,),
        in_specs=in_specs,
        out_specs=out_specs,
        out_shape=out_shape,
        scratch_shapes=[pltpu.VMEM((N_EXPERTS, 128), F32), pltpu.VMEM((tm, 128), F32)],
        compiler_params=_ARB(1),
        name="merge_proj_route",
    )(h, sc1, sh1, g1, sc2, sh2, n1, n2, wg, wpa, wpg, wo, *outs, *lses, gm, wr, br)


def _row_copy(src, dst, sem):
    return pltpu.make_async_copy(src, dst, sem)


def _dispatch_kernel(zb_ref, nz_ref, dest_ref, f_ref, buf_ref, zero_ref, sem, zsem):
    t_rows = f_ref.shape[0]

    @pl.when(pl.program_id(0) == 0)
    def _():
        zero_ref[...] = jnp.zeros_like(zero_ref)

        def zero_copy(j):
            start = pl.multiple_of(zb_ref[j] * EXPERT_BLOCK, EXPERT_BLOCK)
            return _row_copy(zero_ref, buf_ref.at[pl.ds(start, EXPERT_BLOCK)], zsem)

        lax.fori_loop(0, nz_ref[0], lambda j, c: (zero_copy(j).start(), c)[1], 0)
        lax.fori_loop(0, nz_ref[0], lambda j, c: (zero_copy(j).wait(), c)[1], 0)

    def issue(t, carry):
        for k in range(TOP_K):
            slot = dest_ref[0, k, t]
            _row_copy(f_ref.at[pl.ds(t, 1)], buf_ref.at[pl.ds(slot, 1)], sem).start()
        return carry

    lax.fori_loop(0, t_rows, issue, 0)
    for _ in range(TOP_K):
        _row_copy(f_ref, buf_ref.at[pl.ds(0, t_rows)], sem).wait()


def _dispatch(zero_blocks, n_zero, dest_tiles, f, n_blocks):
    n, d = f.shape
    t = ROW_TILE
    return pl.pallas_call(
        _dispatch_kernel,
        grid_spec=pltpu.PrefetchScalarGridSpec(
            num_scalar_prefetch=2, grid=(n // t,),
            in_specs=[pl.BlockSpec((1, TOP_K, t), lambda i, zb, nz: (i, 0, 0),
                                   memory_space=pltpu.SMEM),
                      pl.BlockSpec((t, d), lambda i, zb, nz: (i, 0))],
            out_specs=pl.BlockSpec(memory_space=pl.ANY),
            scratch_shapes=[pltpu.VMEM((EXPERT_BLOCK, d), F32),
                            pltpu.SemaphoreType.DMA(()), pltpu.SemaphoreType.DMA(())]),
        out_shape=jax.ShapeDtypeStruct((n_blocks * EXPERT_BLOCK, d), F32),
        compiler_params=_ARB(1),
        name="dispatch_rows",
    )(zero_blocks, n_zero, dest_tiles, f)


def _expert_kernel(be_ref, first_ref, nv_ref, x_ref, w1_ref, b1_ref, w2_ref, b2_ref, o_ref,
                   w1b_ref, w2p_ref, slab_ref):
    del be_ref
    step = pl.program_id(0)
    live = step < nv_ref[0]
    ff, d = w2p_ref.shape
    half = 64

    @pl.when(live & (first_ref[step] == 1))
    def _():
        w1b_ref[...] = w1_ref[0, 0].astype(BF16)
        for c in range(d // 128):
            cs = slice(c * 128, (c + 1) * 128)
            for k in range(ff // 128):
                r0 = k * 128
                slab_ref[pl.ds(r0, half, stride=2), :] = w2_ref[0, 0, r0:r0 + half, cs]
                slab_ref[pl.ds(r0 + 1, half, stride=2), :] = w2_ref[0, 0, r0 + half:r0 + 128, cs]
            w2p_ref[:, cs] = slab_ref[...].astype(BF16)

    @pl.when(live)
    def _():
        x = x_ref[...].astype(BF16)
        hb = _bdot(x, w1b_ref[...]) + b1_ref[0, 0]
        lane = lax.broadcasted_iota(I32, (x.shape[0], 128), 1)
        even = (lane & 1) == 0
        glu, lin = [], []
        for k in range(ff // 128):
            a = hb[:, 256 * k:256 * k + 128]
            b = hb[:, 256 * k + 128:256 * k + 256]
            glu.append(jnp.where(even, a, pltpu.roll(b, 1, 1)))
            lin.append(jnp.where(even, pltpu.roll(a, 127, 1), b))
        x_glu = jnp.minimum(jnp.concatenate(glu, axis=1), SWIGLU_LIMIT)
        x_lin = jnp.clip(jnp.concatenate(lin, axis=1), -SWIGLU_LIMIT, SWIGLU_LIMIT)
        act = x_glu * jax.nn.sigmoid(SWIGLU_ALPHA * x_glu) * (x_lin + 1.0)
        o_ref[...] = _bdot(act.astype(BF16), w2p_ref[...]) + b2_ref[0, 0]

    @pl.when(jnp.logical_not(live))
    def _():
        o_ref[...] = jnp.zeros_like(o_ref)


def _experts(layer, block_e, first, n_valid, buf, w1, b1, w2, b2, n_blocks):
    d = buf.shape[1]
    ff = w2.shape[2]
    xmap = lambda i, be, fi, nv: (jnp.where(i < nv[0], i, 0), 0)
    wmap = lambda i, be, fi, nv: (layer, be[i], 0, 0)
    return pl.pallas_call(
        _expert_kernel,
        grid_spec=pltpu.PrefetchScalarGridSpec(
            num_scalar_prefetch=3, grid=(n_blocks,),
            in_specs=[pl.BlockSpec((EXPERT_BLOCK, d), xmap),
                      pl.BlockSpec((1, 1, d, 2 * ff), wmap),
                      pl.BlockSpec((1, 1, 1, 2 * ff), wmap),
                      pl.BlockSpec((1, 1, ff, d), wmap),
                      pl.BlockSpec((1, 1, 1, d), wmap)],
            out_specs=pl.BlockSpec((EXPERT_BLOCK, d), lambda i, be, fi, nv: (i, 0)),
            scratch_shapes=[pltpu.VMEM((d, 2 * ff), BF16), pltpu.VMEM((ff, d), BF16),
                            pltpu.VMEM((ff, 128), F32)]),
        out_shape=jax.ShapeDtypeStruct((n_blocks * EXPERT_BLOCK, d), F32),
        compiler_params=_ARB(1),
        name="expert_ffn",
    )(block_e, first, n_valid, buf, w1, b1, w2, b2)


def _combine_kernel(dest_ref, gate_ref, h_ref, g2_ref, fin_ref, obuf_ref, out_ref,
                    rows_ref, sem, *, final_norm):
    t_rows = h_ref.shape[0]

    def issue(t, carry):
        for k in range(TOP_K):
            slot = dest_ref[0, k, t]
            _row_copy(obuf_ref.at[pl.ds(slot, 1)], rows_ref.at[k, pl.ds(t, 1)], sem).start()
        return carry

    lax.fori_loop(0, t_rows, issue, 0)
    for k in range(TOP_K):
        _row_copy(obuf_ref.at[pl.ds(0, t_rows)], rows_ref.at[k], sem).wait()

    gate = gate_ref[...]
    y = gate[:, 0:1] * rows_ref[0]
    for k in range(1, TOP_K):
        y = y + gate[:, k:k + 1] * rows_ref[k]
    hn = h_ref[...] + g2_ref[0] * y
    if final_norm:
        ms = jnp.mean(hn * hn, axis=-1, keepdims=True)
        hn = hn * lax.rsqrt(ms + RMS_EPS) * fin_ref[...]
    out_ref[...] = hn


def _combine(dest_tiles, gate_t, h, g2, final_g, obuf, seq, final_norm):
    n, d = h.shape
    t = ROW_TILE
    per_b = seq // t
    return pl.pallas_call(
        functools.partial(_combine_kernel, final_norm=final_norm),
        grid=(n // t,),
        in_specs=[pl.BlockSpec((1, TOP_K, t), lambda i: (i, 0, 0), memory_space=pltpu.SMEM),
                  pl.BlockSpec((t, TOP_K), lambda i: (i, 0)),
                  pl.BlockSpec((t, d), lambda i: (i, 0)),
                  pl.BlockSpec((1, 1, d), lambda i: (i // per_b, 0, 0)),
                  pl.BlockSpec((1, d), lambda i: (0, 0)),
                  pl.BlockSpec(memory_space=pl.ANY)],
        out_specs=pl.BlockSpec((t, d), lambda i: (i, 0)),
        out_shape=jax.ShapeDtypeStruct((n, d), F32),
        scratch_shapes=[pltpu.VMEM((TOP_K, t, d), F32), pltpu.SemaphoreType.DMA(())],
        compiler_params=_ARB(1),
        name="combine_rows",
    )(dest_tiles, gate_t, h, g2, final_g, obuf)


def _slot_plan(counts, idx, rank, n_blocks):
    blk = EXPERT_BLOCK
    padded = (counts + blk - 1) // blk * blk
    pad_end = jnp.cumsum(padded)
    pad_start = pad_end - padded
    experts = jnp.arange(N_EXPERTS, dtype=I32)
    dest = rank + jnp.sum(jnp.where(idx[..., None] == experts, pad_start, 0), axis=-1)
    n_valid = pad_end[-1] // blk
    block_start = jnp.arange(n_blocks, dtype=I32) * blk
    block_e = jnp.sum(pad_end[None, :] <= block_start[:, None], axis=1)
    last_e = jnp.sum(pad_end <= (n_valid - 1) * blk)
    block_e = jnp.where(jnp.arange(n_blocks) < n_valid, block_e, last_e).astype(I32)
    first = jnp.concatenate([jnp.ones((1,), I32),
                             (block_e[1:] != block_e[:-1]).astype(I32)])
    tails = jnp.where(counts > 0, pad_end // blk - 1, -1)
    spare = jnp.arange(n_blocks - N_EXPERTS, n_blocks)
    cand = jnp.concatenate([tails, jnp.where(spare >= n_valid, spare, -1)]).astype(I32)
    zero_blocks = cand[jnp.argsort(cand < 0, stable=True)]
    n_zero = jnp.sum(cand >= 0).reshape(1).astype(I32)
    return (dest.astype(I32), block_e, first, n_valid.reshape(1).astype(I32),
            jnp.maximum(zero_blocks, 0), n_zero)


def _row_tiles(x):
    k, n = x.shape
    return x.reshape(k, n // ROW_TILE, ROW_TILE).transpose(1, 0, 2)


def kernel(x, c, positions, w_ada, b_ada, norm1_g, w_in, w_s, b_s, ln_g, ln_b, w_pa, w_pg,
           w_o, norm2_g, w_router, b_router, w1, b1, w2, b2, final_g):
    batch, seq, d = x.shape
    depth = w_ada.shape[0]
    n = batch * seq
    n_blocks = (n * TOP_K) // EXPERT_BLOCK + N_EXPERTS

    mod = _modulation(c, w_ada, b_ada)
    cos, sin = _rope_tables(positions)

    n_mix = 3 * ATTN_W + 2 * GMLP_W
    b1r = b1.reshape(depth, N_EXPERTS, 1, -1)
    b2r = b2.reshape(depth, N_EXPERTS, 1, -1)

    h = x.reshape(n, d)
    for l in range(depth):
        ml = mod[l].reshape(batch, 6, 1, d)
        sh1, sc1, g1, sh2, sc2, g2 = [ml[:, i] for i in range(6)]
        w_mix = w_in[l, :, :n_mix].astype(BF16)
        w_gate = w_in[l, :, n_mix:].astype(BF16)

        *qkvs, gm = _inproj(h, sc1, sh1, norm1_g[l].reshape(1, d), w_mix, cos, sin,
                            w_s[l], b_s[l].T, ln_g[l].reshape(1, -1), ln_b[l].reshape(1, -1), seq)
        outs, lses = zip(*[_attention_group(qkvs[g], g) for g in range(N_GROUPS)])

        h, f, idx, gate, rank, counts = _merge_route(
            h, (sc1, sh1, g1, sc2, sh2), norm1_g[l].reshape(1, d), norm2_g[l].reshape(1, d),
            w_gate, w_pa[l].astype(BF16), w_pg[l].astype(BF16), w_o[l].astype(BF16),
            outs, lses, gm, w_router[l].T, b_router[l].reshape(-1, 1), seq)

        dest, block_e, first, n_valid, zero_blocks, n_zero = _slot_plan(
            counts[:, 0], idx, rank, n_blocks)
        dest_tiles = _row_tiles(dest)
        buf = _dispatch(zero_blocks, n_zero, dest_tiles, f, n_blocks)
        obuf = _experts(l, block_e, first, n_valid, buf, w1, b1r, w2, b2r, n_blocks)

        h = _combine(dest_tiles, gate.T, h, g2, final_g.reshape(1, d), obuf, seq,
                     final_norm=(l == depth - 1))
    return h.reshape(batch, seq, d)
```

```python
import functools

import jax
import jax.numpy as jnp
from jax import lax
from jax.experimental import pallas as pl
from jax.experimental.pallas import tpu as pltpu

F32 = jnp.float32
BF16 = jnp.bfloat16
I32 = jnp.int32

HEAD_DIM = 64
HEADS_PER_GROUP = 4
GROUP_W = HEADS_PER_GROUP * HEAD_DIM
ATTN_GROUPS = ((128, 1), (512, 4), (2048, 16))
N_GROUPS = len(ATTN_GROUPS)
ATTN_W = N_GROUPS * GROUP_W
WIN_BLOCK = 128
ROPE_THETA = 10000.0
GMLP_CHUNK = 128
GMLP_GROUPS = 4
GMLP_W = GMLP_GROUPS * 128
N_EXPERTS = 32
TOP_K = 4
SWIGLU_ALPHA = 1.702
SWIGLU_LIMIT = 7.0
RMS_EPS = 1e-5
LN_EPS = 1e-5
NEG_INF = -1e30

TOKEN_TILE = 512
EXPERT_BLOCK = 512
ROW_ALIGN = 8
SORT_CHUNK = 256
VMEM_LIMIT = 56 * 1024 * 1024

_ARB = lambda n: pltpu.CompilerParams(dimension_semantics=("arbitrary",) * n,
                                      vmem_limit_bytes=VMEM_LIMIT)


def _bdot(a, b):
    return jnp.dot(a, b, preferred_element_type=F32)


def _split3(x):
    hi = x.astype(BF16)
    r1 = x - hi.astype(F32)
    mid = r1.astype(BF16)
    lo = (r1 - mid.astype(F32)).astype(BF16)
    return hi, mid, lo


def _dot_f32(a, b):
    a0, a1, a2 = _split3(a)
    b0, b1, b2 = _split3(b)
    small = _bdot(a0, b2) + _bdot(a1, b1) + _bdot(a2, b0)
    mid = _bdot(a0, b1) + _bdot(a1, b0)
    return (small + mid) + _bdot(a0, b0)


def _mod_kernel(c_ref, w_ref, b_ref, o_ref):
    c = c_ref[...]
    s = c * jax.nn.sigmoid(c)
    o_ref[0] = _dot_f32(s, w_ref[0]) + b_ref[0]


def _modulation(c, w_ada, b_ada):
    depth, d, six_d = w_ada.shape
    b = c.shape[0]
    n_col = six_d // d
    return pl.pallas_call(
        _mod_kernel,
        grid=(depth, n_col),
        in_specs=[pl.BlockSpec((b, d), lambda l, j: (0, 0)),
                  pl.BlockSpec((1, d, d), lambda l, j: (l, 0, j)),
                  pl.BlockSpec((1, 1, d), lambda l, j: (l, 0, j))],
        out_specs=pl.BlockSpec((1, b, d), lambda l, j: (l, 0, j)),
        out_shape=jax.ShapeDtypeStruct((depth, b, six_d), F32),
        compiler_params=_ARB(2),
        name="adaln_mod",
    )(c, w_ada, b_ada.reshape(depth, 1, six_d))


def _rope_kernel(pos_ref, freq_ref, sign_ref, cos_ref, sin_ref):
    ang = pos_ref[...].astype(F32) * freq_ref[...]
    cos_ref[...] = jnp.cos(ang)
    sin_ref[...] = jnp.sin(ang) * sign_ref[...]


def _rope_tables(positions):
    n = positions.size
    half = HEAD_DIM // 2
    inv_freq = ROPE_THETA ** (-jnp.arange(half, dtype=F32) / half)
    freq = jnp.tile(inv_freq, 4).reshape(1, 128)
    sign = jnp.tile(jnp.concatenate([-jnp.ones((half,), F32), jnp.ones((half,), F32)]), 2)
    sign = sign.reshape(1, 128)
    tile = 1024
    return pl.pallas_call(
        _rope_kernel,
        grid=(n // tile,),
        in_specs=[pl.BlockSpec((tile, 1), lambda i: (i, 0)),
                  pl.BlockSpec((1, 128), lambda i: (0, 0)),
                  pl.BlockSpec((1, 128), lambda i: (0, 0))],
        out_specs=[pl.BlockSpec((tile, 128), lambda i: (i, 0))] * 2,
        out_shape=[jax.ShapeDtypeStruct((n, 128), F32)] * 2,
        compiler_params=_ARB(1),
        name="rope_tables",
    )(positions.reshape(n, 1), freq, sign)


def _modulated_norm(x, g, sc, sh):
    ms = jnp.mean(x * x, axis=-1, keepdims=True)
    y = x * lax.rsqrt(ms + RMS_EPS) * g
    return y * (1.0 + sc) + sh


def _inproj_kernel(h_ref, sc_ref, sh_ref, g_ref, w_ref, cos_ref, sin_ref,
                   ws_ref, bs_ref, lng_ref, lnb_ref, qkv0_ref, qkv1_ref, qkv2_ref, gm_ref,
                   slab_ref):
    tm = h_ref.shape[0]
    a = _modulated_norm(h_ref[...], g_ref[...], sc_ref[0], sh_ref[0]).astype(BF16)
    cos = cos_ref[...]
    sin = sin_ref[...]
    lane = lax.broadcasted_iota(I32, (tm, 128), 1)
    first_half = (lane % HEAD_DIM) < (HEAD_DIM // 2)

    def rope(x):
        swapped = jnp.where(first_half, pltpu.roll(x, 96, 1), pltpu.roll(x, 32, 1))
        return x * cos + swapped * sin

    for g, out_ref in enumerate((qkv0_ref, qkv1_ref, qkv2_ref)):
        r = ATTN_GROUPS[g][1]
        for part in range(3):
            c0 = part * ATTN_W + g * GROUP_W
            y = _bdot(a, w_ref[:, c0:c0 + GROUP_W])
            halves = [y[:, :128], y[:, 128:]]
            if part < 2:
                halves = [rope(x) for x in halves]
            if part == 0:
                halves = [x * (HEAD_DIM ** -0.5) for x in halves]
            for c, x in enumerate(halves):
                o0 = part * GROUP_W + c * 128
                if r == 1:
                    out_ref[0, 0, :, o0:o0 + 128] = x.astype(BF16)
                else:
                    slab_ref[...] = x
                    for j in range(r):
                        rows = slab_ref[pl.ds(j, tm // r, stride=r), :]
                        out_ref[0, j, :, o0:o0 + 128] = rows.astype(BF16)

    c0 = 3 * ATTN_W
    u = jax.nn.gelu(_bdot(a, w_ref[:, c0:c0 + GMLP_W]))
    v = jax.nn.gelu(_bdot(a, w_ref[:, c0 + GMLP_W:c0 + 2 * GMLP_W]))
    mu = jnp.mean(v, axis=-1, keepdims=True)
    var = jnp.mean(jnp.square(v - mu), axis=-1, keepdims=True)
    v = ((v - mu) * lax.rsqrt(var + LN_EPS) * lng_ref[...] + lnb_ref[...]).astype(BF16)
    row = lax.broadcasted_iota(I32, (GMLP_CHUNK, GMLP_CHUNK), 0)
    col = lax.broadcasted_iota(I32, (GMLP_CHUNK, GMLP_CHUNK), 1)
    tril = col <= row
    for g in range(GMLP_GROUPS):
        wsg = jnp.where(tril, ws_ref[g], 0.0).astype(BF16)
        bias = bs_ref[:, g:g + 1]
        cs = slice(g * 128, (g + 1) * 128)
        for c in range(tm // GMLP_CHUNK):
            rs = slice(c * GMLP_CHUNK, (c + 1) * GMLP_CHUNK)
            mixed = _bdot(wsg, v[rs, cs]) + bias
            gm_ref[rs, cs] = (u[rs, cs] * mixed).astype(BF16)


def _inproj(h, sc, sh, g, w, cos, sin, w_s, b_s_t, ln_g, ln_b, seq):
    n, d = h.shape
    tm = TOKEN_TILE
    per_b = seq // tm
    batch = n // seq
    wn = w.shape[1]
    row = lambda i: (i, 0)
    bat = lambda i: (i // per_b, 0, 0)
    full2 = lambda i: (0, 0)
    res = lambda i: (i // per_b, 0, i % per_b, 0)
    qkv_specs = [pl.BlockSpec((1, r, tm // r, 3 * GROUP_W), res) for _, r in ATTN_GROUPS]
    qkv_shapes = [jax.ShapeDtypeStruct((batch, r, seq // r, 3 * GROUP_W), BF16)
                  for _, r in ATTN_GROUPS]
    return pl.pallas_call(
        _inproj_kernel,
        grid=(n // tm,),
        in_specs=[pl.BlockSpec((tm, d), row),
                  pl.BlockSpec((1, 1, d), bat),
                  pl.BlockSpec((1, 1, d), bat),
                  pl.BlockSpec((1, d), full2),
                  pl.BlockSpec((d, wn), full2),
                  pl.BlockSpec((tm, 128), row),
                  pl.BlockSpec((tm, 128), row),
                  pl.BlockSpec((GMLP_GROUPS, GMLP_CHUNK, GMLP_CHUNK), lambda i: (0, 0, 0)),
                  pl.BlockSpec((GMLP_CHUNK, GMLP_GROUPS), full2),
                  pl.BlockSpec((1, GMLP_W), full2),
                  pl.BlockSpec((1, GMLP_W), full2)],
        out_specs=qkv_specs + [pl.BlockSpec((tm, GMLP_W), row)],
        out_shape=qkv_shapes + [jax.ShapeDtypeStruct((n, GMLP_W), BF16)],
        scratch_shapes=[pltpu.VMEM((tm, 128), F32)],
        compiler_params=_ARB(1),
        name="inproj_rope_gmlp",
    )(h, sc, sh, g, w, cos, sin, w_s, b_s_t, ln_g, ln_b)


def _attn_kernel(q_ref, k_ref, v_ref, kp_ref, vp_ref, o_ref, lse_ref):
    q_ref, k_ref, v_ref, kp_ref, vp_ref, o_ref, lse_ref = (
        x.at[0] for x in (q_ref, k_ref, v_ref, kp_ref, vp_ref, o_ref, lse_ref))
    qb = q_ref.shape[1]
    first_key = jnp.where(pl.program_id(2) == 0, WIN_BLOCK, 0)
    qi = lax.broadcasted_iota(I32, (WIN_BLOCK, 2 * WIN_BLOCK), 0)
    kj = lax.broadcasted_iota(I32, (WIN_BLOCK, 2 * WIN_BLOCK), 1)
    band = (kj >= qi) & (kj <= qi + WIN_BLOCK)
    for i in range(qb // WIN_BLOCK):
        rs = slice(i * WIN_BLOCK, (i + 1) * WIN_BLOCK)
        q = q_ref[0, rs, :]
        if i == 0:
            k_prev, v_prev = kp_ref[0], vp_ref[0]
            valid = band & (kj >= first_key)
        else:
            ps = slice((i - 1) * WIN_BLOCK, i * WIN_BLOCK)
            k_prev, v_prev = k_ref[0, ps, :], v_ref[0, ps, :]
            valid = band
        k_cat = jnp.concatenate([k_prev, k_ref[0, rs, :]], axis=0)
        v_cat = jnp.concatenate([v_prev, v_ref[0, rs, :]], axis=0)
        outs, lses = [], []
        for h in range(HEADS_PER_GROUP):
            hs = slice(h * HEAD_DIM, (h + 1) * HEAD_DIM)
            s = lax.dot_general(q[:, hs], k_cat[:, hs], (((1,), (1,)), ((), ())),
                                preferred_element_type=F32)
            s = jnp.where(valid, s, NEG_INF)
            m = jnp.max(s, axis=-1, keepdims=True)
            p = jnp.exp(s - m)
            den = jnp.sum(p, axis=-1, keepdims=True)
            o = _bdot(p.astype(BF16), v_cat[:, hs]) / den
            outs.append(o)
            lses.append(jnp.broadcast_to(m + jnp.log(den), (WIN_BLOCK, HEAD_DIM)))
        o_ref[0, rs, :] = jnp.concatenate(outs, axis=1).astype(o_ref.dtype)
        lse_ref[0, rs, :] = jnp.concatenate(lses, axis=1)


def _attention_group(qkv, g):
    batch, r, sub_len, _ = qkv.shape
    qb = min(512, sub_len)
    n_tiles = sub_len // qb
    per_tile = qb // WIN_BLOCK

    def cur(part):
        return pl.BlockSpec((1, 1, qb, GROUP_W), lambda b, j, n: (b, j, n, part))

    def prev(part):
        return pl.BlockSpec((1, 1, WIN_BLOCK, GROUP_W),
                            lambda b, j, n: (b, j, jnp.maximum(n * per_tile - 1, 0), part))

    out_spec = pl.BlockSpec((1, 1, qb, GROUP_W), lambda b, j, n: (b, j, n, 0))
    return pl.pallas_call(
        _attn_kernel,
        grid=(batch, r, n_tiles),
        in_specs=[cur(0), cur(1), cur(2), prev(1), prev(2)],
        out_specs=[out_spec, out_spec],
        out_shape=[jax.ShapeDtypeStruct((batch, r, sub_len, GROUP_W), BF16),
                   jax.ShapeDtypeStruct((batch, r, sub_len, GROUP_W), F32)],
        compiler_params=_ARB(3),
        name=f"dilated_attn_g{g}",
    )(qkv, qkv, qkv, qkv, qkv)


def _split_bf16(x):
    hi = x.astype(BF16)
    lo = (x - hi.astype(F32)).astype(BF16)
    return hi, lo


def _merge_kernel(h_ref, sc1_ref, sh1_ref, g1_ref, sc2_ref, sh2_ref, n1_ref, n2_ref,
                  wg_ref, wpa_ref, wpg_ref, wo_ref,
                  o0_ref, o1_ref, o2_ref, l0_ref, l1_ref, l2_ref, gm_ref,
                  wr_ref, br_ref,
                  hn_ref, f_ref, slot_ref, gate_ref, cnt_ref, slab_ref):
    tm, d = h_ref.shape
    h = h_ref[...]
    a = _modulated_norm(h, n1_ref[...], sc1_ref[0], sh1_ref[0]).astype(BF16)
    gates = jax.nn.sigmoid(_bdot(a, wg_ref[...]))

    def token_major(ref):
        r = ref.shape[1]
        if r == 1:
            return ref[0, 0].astype(F32)
        halves = []
        for c in range(GROUP_W // 128):
            for j in range(r):
                slab_ref[pl.ds(j, tm // r, stride=r), :] = (
                    ref[0, j, :, c * 128:(c + 1) * 128].astype(F32))
            halves.append(slab_ref[...])
        return jnp.concatenate(halves, axis=1)

    lses = [token_major(x) for x in (l0_ref, l1_ref, l2_ref)]
    outs = [token_major(x) for x in (o0_ref, o1_ref, o2_ref)]
    m = jnp.maximum(jnp.maximum(lses[0], lses[1]), lses[2])
    es = [jnp.exp(l - m) for l in lses]
    den = es[0] + es[1] + es[2]
    attn = (es[0] * outs[0] + es[1] * outs[1] + es[2] * outs[2]) / den

    merged = (gates[:, :d] * _bdot(attn.astype(BF16), wpa_ref[...])
              + gates[:, d:] * _bdot(gm_ref[...], wpg_ref[...]))
    hn = h + g1_ref[0] * _bdot(merged.astype(BF16), wo_ref[...])
    hn_ref[...] = hn

    f = _modulated_norm(hn, n2_ref[...], sc2_ref[0], sh2_ref[0])
    f_ref[...] = f.astype(BF16)

    f_hi, f_lo = _split_bf16(f)
    w_hi, w_lo = _split_bf16(wr_ref[...])
    nt = (((1,), (1,)), ((), ()))
    logits = (lax.dot_general(w_hi, f_hi, nt, preferred_element_type=F32)
              + lax.dot_general(w_hi, f_lo, nt, preferred_element_type=F32)
              + lax.dot_general(w_lo, f_hi, nt, preferred_element_type=F32)) + br_ref[...]

    n_e = logits.shape[0]
    erow = lax.broadcasted_iota(I32, (n_e, tm), 0).astype(F32)
    vals, idxs, sels = [], [], []
    cur = logits
    for _ in range(TOP_K):
        mx = jnp.max(cur, axis=0, keepdims=True)
        ix = jnp.min(jnp.where(cur == mx, erow, float(n_e)), axis=0, keepdims=True)
        sel = erow == ix
        vals.append(mx)
        idxs.append(ix)
        sels.append(sel)
        cur = jnp.where(sel, -jnp.inf, cur)
    exps = [jnp.exp(v - vals[0]) for v in vals]
    tot = exps[0] + exps[1] + exps[2] + exps[3]
    gate_ref[...] = jnp.concatenate([e / tot for e in exps], axis=0)

    onehot = (sels[0] | sels[1] | sels[2] | sels[3])
    oh = jnp.where(onehot, 1.0, 0.0)
    t_r = lax.broadcasted_iota(I32, (tm, tm), 0)
    t_c = lax.broadcasted_iota(I32, (tm, tm), 1)
    upper = jnp.where(t_r < t_c, 1.0, 0.0).astype(BF16)
    prefix = _bdot(oh.astype(BF16), upper)
    cnt = jnp.sum(oh, axis=1, keepdims=True)
    cnt_al = jnp.floor((cnt + (ROW_ALIGN - 1)) * (1.0 / ROW_ALIGN)) * ROW_ALIGN
    e_r = lax.broadcasted_iota(I32, (n_e, n_e), 0)
    e_c = lax.broadcasted_iota(I32, (n_e, n_e), 1)
    lower = jnp.where(e_c < e_r, 1.0, 0.0).astype(BF16)
    start = _bdot(lower, jnp.broadcast_to(cnt_al, (n_e, 128)).astype(BF16))[:, 0:1]
    local = prefix + start
    slots = [jnp.sum(jnp.where(s, local, 0.0), axis=0, keepdims=True) for s in sels]
    slot_ref[...] = jnp.concatenate(slots, axis=0).astype(I32)
    cnt_ref[...] = jnp.broadcast_to(cnt, cnt_ref.shape).astype(I32)


def _merge_route(h, mods, n1, n2, wg, wpa, wpg, wo, outs, lses, gm, wr, br, seq):
    n, d = h.shape
    tm = TOKEN_TILE
    per_b = seq // tm
    row = lambda i: (i, 0)
    bat = lambda i: (i // per_b, 0, 0)
    full2 = lambda i: (0, 0)
    col = lambda i: (0, i)
    sc1, sh1, g1, sc2, sh2 = mods
    res = lambda i: (i // per_b, 0, i % per_b, 0)
    attn_specs = [pl.BlockSpec((1, r, tm // r, GROUP_W), res) for _, r in ATTN_GROUPS]
    in_specs = ([pl.BlockSpec((tm, d), row)]
                + [pl.BlockSpec((1, 1, d), bat)] * 5
                + [pl.BlockSpec((1, d), full2)] * 2
                + [pl.BlockSpec(wg.shape, full2), pl.BlockSpec(wpa.shape, full2),
                   pl.BlockSpec(wpg.shape, full2), pl.BlockSpec(wo.shape, full2)]
                + attn_specs * 2
                + [pl.BlockSpec((tm, GMLP_W), row)]
                + [pl.BlockSpec(wr.shape, full2), pl.BlockSpec(br.shape, full2)])
    out_specs = [pl.BlockSpec((tm, d), row), pl.BlockSpec((tm, d), row),
                 pl.BlockSpec((TOP_K, tm), col), pl.BlockSpec((TOP_K, tm), col),
                 pl.BlockSpec((N_EXPERTS, 128), row)]
    out_shape = [jax.ShapeDtypeStruct((n, d), F32), jax.ShapeDtypeStruct((n, d), BF16),
                 jax.ShapeDtypeStruct((TOP_K, n), I32), jax.ShapeDtypeStruct((TOP_K, n), F32),
                 jax.ShapeDtypeStruct((n // tm * N_EXPERTS, 128), I32)]
    return pl.pallas_call(
        _merge_kernel,
        grid=(n // tm,),
        in_specs=in_specs,
        out_specs=out_specs,
        out_shape=out_shape,
        scratch_shapes=[pltpu.VMEM((tm, 128), F32)],
        compiler_params=_ARB(1),
        name="merge_proj_route",
    )(h, sc1, sh1, g1, sc2, sh2, n1, n2, wg, wpa, wpg, wo, *outs, *lses, gm, wr, br)


def _row_copy(src, dst, sem):
    return pltpu.make_async_copy(src, dst, sem)


def _run_pieces(step, off_ref, start_ref, pieces_ref, piece_copy):
    def per_expert(e, total):
        base = step * N_EXPERTS + e
        local0, global0, n = start_ref[base], off_ref[base], pieces_ref[base]

        def start_piece(i, carry):
            piece_copy(pl.multiple_of(local0 + i * ROW_ALIGN, ROW_ALIGN),
                       pl.multiple_of(global0 + i * ROW_ALIGN, ROW_ALIGN)).start()
            return carry

        lax.fori_loop(0, n, start_piece, 0)
        return total + n

    total = lax.fori_loop(0, N_EXPERTS, per_expert, 0)
    lax.fori_loop(0, total, lambda i, c: (piece_copy(0, 0).wait(), c)[1], 0)


def _slot_hits(slot_ref, chunk):
    tm = slot_ref.shape[1]
    s = lax.broadcasted_iota(I32, (SORT_CHUNK, tm), 0) + chunk * SORT_CHUNK
    return [slot_ref[k:k + 1, :] == s for k in range(TOP_K)]


def _dispatch_kernel(zb_ref, nz_ref, off_ref, start_ref, pieces_ref, slot_ref, f_ref, buf_ref,
                     sorted_ref, zero_ref, sem, zsem):
    @pl.when(pl.program_id(0) == 0)
    def _():
        zero_ref[...] = jnp.zeros_like(zero_ref)

        def zero_copy(j):
            start = pl.multiple_of(zb_ref[j] * EXPERT_BLOCK, EXPERT_BLOCK)
            return _row_copy(zero_ref, buf_ref.at[pl.ds(start, EXPERT_BLOCK)], zsem)

        lax.fori_loop(0, nz_ref[0], lambda j, c: (zero_copy(j).start(), c)[1], 0)
        lax.fori_loop(0, nz_ref[0], lambda j, c: (zero_copy(j).wait(), c)[1], 0)

    f = f_ref[...]
    for c in range(sorted_ref.shape[0] // SORT_CHUNK):
        hits = _slot_hits(slot_ref, c)
        onehot = jnp.where(hits[0] | hits[1] | hits[2] | hits[3], 1.0, 0.0).astype(BF16)
        sorted_ref[c * SORT_CHUNK:(c + 1) * SORT_CHUNK, :] = _bdot(onehot, f)

    def piece_copy(local_row, global_row):
        return _row_copy(sorted_ref.at[pl.ds(local_row, ROW_ALIGN)],
                         buf_ref.at[pl.ds(global_row, ROW_ALIGN)], sem)

    _run_pieces(pl.program_id(0), off_ref, start_ref, pieces_ref, piece_copy)


def _dispatch(plan, slots, f, n_blocks):
    n, d = f.shape
    tm = TOKEN_TILE
    return pl.pallas_call(
        _dispatch_kernel,
        grid_spec=pltpu.PrefetchScalarGridSpec(
            num_scalar_prefetch=5, grid=(n // tm,),
            in_specs=[pl.BlockSpec((TOP_K, tm), lambda i, *_: (0, i)),
                      pl.BlockSpec((tm, d), lambda i, *_: (i, 0))],
            out_specs=pl.BlockSpec(memory_space=pl.ANY),
            scratch_shapes=[pltpu.VMEM((_local_rows(tm), d), F32),
                            pltpu.VMEM((EXPERT_BLOCK, d), F32),
                            pltpu.SemaphoreType.DMA(()), pltpu.SemaphoreType.DMA(())]),
        out_shape=jax.ShapeDtypeStruct((n_blocks * EXPERT_BLOCK, d), F32),
        compiler_params=_ARB(1),
        name="dispatch_rows",
    )(plan["zero_blocks"], plan["n_zero"], plan["run_off"], plan["run_start"],
      plan["run_pieces"], slots, f)


def _expert_kernel(be_ref, first_ref, nv_ref, x_ref, w1_ref, b1_ref, w2_ref, b2_ref, o_ref,
                   w1b_ref, w2p_ref, slab_ref):
    del be_ref
    step = pl.program_id(0)
    live = step < nv_ref[0]
    ff, d = w2p_ref.shape
    half = 64

    @pl.when(live & (first_ref[step] == 1))
    def _():
        w1b_ref[...] = w1_ref[0, 0].astype(BF16)
        for c in range(d // 128):
            cs = slice(c * 128, (c + 1) * 128)
            for k in range(ff // 128):
                r0 = k * 128
                slab_ref[pl.ds(r0, half, stride=2), :] = w2_ref[0, 0, r0:r0 + half, cs]
                slab_ref[pl.ds(r0 + 1, half, stride=2), :] = w2_ref[0, 0, r0 + half:r0 + 128, cs]
            w2p_ref[:, cs] = slab_ref[...].astype(BF16)

    @pl.when(live)
    def _():
        x = x_ref[...].astype(BF16)
        hb = _bdot(x, w1b_ref[...]) + b1_ref[0, 0]
        lane = lax.broadcasted_iota(I32, (x.shape[0], 128), 1)
        even = (lane & 1) == 0
        glu, lin = [], []
        for k in range(ff // 128):
            a = hb[:, 256 * k:256 * k + 128]
            b = hb[:, 256 * k + 128:256 * k + 256]
            glu.append(jnp.where(even, a, pltpu.roll(b, 1, 1)))
            lin.append(jnp.where(even, pltpu.roll(a, 127, 1), b))
        x_glu = jnp.minimum(jnp.concatenate(glu, axis=1), SWIGLU_LIMIT)
        x_lin = jnp.clip(jnp.concatenate(lin, axis=1), -SWIGLU_LIMIT, SWIGLU_LIMIT)
        act = x_glu * jax.nn.sigmoid(SWIGLU_ALPHA * x_glu) * (x_lin + 1.0)
        o_ref[...] = _bdot(act.astype(BF16), w2p_ref[...]) + b2_ref[0, 0]

    @pl.when(jnp.logical_not(live))
    def _():
        o_ref[...] = jnp.zeros_like(o_ref)


def _experts(layer, block_e, first, n_valid, buf, w1, b1, w2, b2, n_blocks):
    d = buf.shape[1]
    ff = w2.shape[2]
    xmap = lambda i, be, fi, nv: (jnp.where(i < nv[0], i, 0), 0)
    wmap = lambda i, be, fi, nv: (layer, be[i], 0, 0)
    return pl.pallas_call(
        _expert_kernel,
        grid_spec=pltpu.PrefetchScalarGridSpec(
            num_scalar_prefetch=3, grid=(n_blocks,),
            in_specs=[pl.BlockSpec((EXPERT_BLOCK, d), xmap),
                      pl.BlockSpec((1, 1, d, 2 * ff), wmap),
                      pl.BlockSpec((1, 1, 1, 2 * ff), wmap),
                      pl.BlockSpec((1, 1, ff, d), wmap),
                      pl.BlockSpec((1, 1, 1, d), wmap)],
            out_specs=pl.BlockSpec((EXPERT_BLOCK, d), lambda i, be, fi, nv: (i, 0)),
            scratch_shapes=[pltpu.VMEM((d, 2 * ff), BF16), pltpu.VMEM((ff, d), BF16),
                            pltpu.VMEM((ff, 128), F32)]),
        out_shape=jax.ShapeDtypeStruct((n_blocks * EXPERT_BLOCK, d), F32),
        compiler_params=_ARB(1),
        name="expert_ffn",
    )(block_e, first, n_valid, buf, w1, b1, w2, b2)


def _combine_kernel(off_ref, start_ref, pieces_ref, slot_ref, gate_ref, h_ref, g2_ref, fin_ref,
                    obuf_ref, out_ref, rows_ref, sem, *, final_norm):
    @pl.when(pl.program_id(0) == 0)
    def _():
        rows_ref[...] = jnp.zeros_like(rows_ref)

    def piece_copy(local_row, global_row):
        return _row_copy(obuf_ref.at[pl.ds(global_row, ROW_ALIGN)],
                         rows_ref.at[pl.ds(local_row, ROW_ALIGN)], sem)

    _run_pieces(pl.program_id(0), off_ref, start_ref, pieces_ref, piece_copy)

    y = jnp.zeros(h_ref.shape, F32)
    for c in range(rows_ref.shape[0] // SORT_CHUNK):
        hits = _slot_hits(slot_ref, c)
        weight = jnp.where(hits[0], gate_ref[0:1, :], 0.0)
        for k in range(1, TOP_K):
            weight = weight + jnp.where(hits[k], gate_ref[k:k + 1, :], 0.0)
        rows = rows_ref[c * SORT_CHUNK:(c + 1) * SORT_CHUNK, :].astype(BF16)
        y = y + lax.dot_general(weight.astype(BF16), rows, (((0,), (0,)), ((), ())),
                                preferred_element_type=F32)
    hn = h_ref[...] + g2_ref[0] * y
    if final_norm:
        ms = jnp.mean(hn * hn, axis=-1, keepdims=True)
        hn = hn * lax.rsqrt(ms + RMS_EPS) * fin_ref[...]
    out_ref[...] = hn


def _combine(plan, slots, gate, h, g2, final_g, obuf, seq, final_norm):
    n, d = h.shape
    tm = TOKEN_TILE
    per_b = seq // tm
    return pl.pallas_call(
        functools.partial(_combine_kernel, final_norm=final_norm),
        grid_spec=pltpu.PrefetchScalarGridSpec(
            num_scalar_prefetch=3, grid=(n // tm,),
            in_specs=[pl.BlockSpec((TOP_K, tm), lambda i, *_: (0, i)),
                      pl.BlockSpec((TOP_K, tm), lambda i, *_: (0, i)),
                      pl.BlockSpec((tm, d), lambda i, *_: (i, 0)),
                      pl.BlockSpec((1, 1, d), lambda i, *_: (i // per_b, 0, 0)),
                      pl.BlockSpec((1, d), lambda i, *_: (0, 0)),
                      pl.BlockSpec(memory_space=pl.ANY)],
            out_specs=pl.BlockSpec((tm, d), lambda i, *_: (i, 0)),
            scratch_shapes=[pltpu.VMEM((_local_rows(tm), d), F32),
                            pltpu.SemaphoreType.DMA(())]),
        out_shape=jax.ShapeDtypeStruct((n, d), F32),
        compiler_params=_ARB(1),
        name="combine_rows",
    )(plan["run_off"], plan["run_start"], plan["run_pieces"], slots, gate, h, g2, final_g, obuf)


def _local_rows(tm):
    rows = tm * TOP_K + N_EXPERTS * (ROW_ALIGN - 1)
    return -(-rows // SORT_CHUNK) * SORT_CHUNK


def _num_blocks(n):
    rows = n * TOP_K + (n // TOKEN_TILE) * N_EXPERTS * (ROW_ALIGN - 1)
    return -(-rows // EXPERT_BLOCK) + N_EXPERTS


def _slot_plan(tile_counts, n):
    blk = EXPERT_BLOCK
    n_blocks = _num_blocks(n)
    cnt = (tile_counts + ROW_ALIGN - 1) // ROW_ALIGN * ROW_ALIGN
    total = jnp.sum(cnt, axis=0)
    padded = (total + blk - 1) // blk * blk
    pad_end = jnp.cumsum(padded)
    pad_start = pad_end - padded
    run_off = pad_start[None, :] + jnp.cumsum(cnt, axis=0) - cnt
    run_start = jnp.cumsum(cnt, axis=1) - cnt
    n_valid = pad_end[-1] // blk
    block_start = jnp.arange(n_blocks, dtype=I32) * blk
    block_e = jnp.sum(pad_end[None, :] <= block_start[:, None], axis=1)
    last_e = jnp.sum(pad_end <= (n_valid - 1) * blk)
    block_e = jnp.where(jnp.arange(n_blocks) < n_valid, block_e, last_e).astype(I32)
    first = jnp.concatenate([jnp.ones((1,), I32),
                             (block_e[1:] != block_e[:-1]).astype(I32)])
    tails = jnp.where(total > 0, pad_end // blk - 1, -1)
    spare = jnp.arange((n * TOP_K) // blk, n_blocks)
    cand = jnp.concatenate([tails, jnp.where(spare >= n_valid, spare, -1)]).astype(I32)
    zero_blocks = cand[jnp.argsort(cand < 0, stable=True)]
    flat = lambda a: a.reshape(-1).astype(I32)
    return dict(block_e=block_e, first=first, n_valid=n_valid.reshape(1).astype(I32),
                zero_blocks=jnp.maximum(zero_blocks, 0),
                n_zero=jnp.sum(cand >= 0).reshape(1).astype(I32),
                run_off=flat(run_off), run_start=flat(run_start),
                run_pieces=flat(cnt // ROW_ALIGN))


def kernel(x, c, positions, w_ada, b_ada, norm1_g, w_in, w_s, b_s, ln_g, ln_b, w_pa, w_pg,
           w_o, norm2_g, w_router, b_router, w1, b1, w2, b2, final_g):
    batch, seq, d = x.shape
    depth = w_ada.shape[0]
    n = batch * seq
    n_blocks = _num_blocks(n)

    mod = _modulation(c, w_ada, b_ada)
    cos, sin = _rope_tables(positions)

    n_mix = 3 * ATTN_W + 2 * GMLP_W
    b1r = b1.reshape(depth, N_EXPERTS, 1, -1)
    b2r = b2.reshape(depth, N_EXPERTS, 1, -1)

    h = x.reshape(n, d)
    for l in range(depth):
        ml = mod[l].reshape(batch, 6, 1, d)
        sh1, sc1, g1, sh2, sc2, g2 = [ml[:, i] for i in range(6)]
        w_mix = w_in[l, :, :n_mix].astype(BF16)
        w_gate = w_in[l, :, n_mix:].astype(BF16)

        *qkvs, gm = _inproj(h, sc1, sh1, norm1_g[l].reshape(1, d), w_mix, cos, sin,
                            w_s[l], b_s[l].T, ln_g[l].reshape(1, -1), ln_b[l].reshape(1, -1), seq)
        outs, lses = zip(*[_attention_group(qkvs[g], g) for g in range(N_GROUPS)])

        h, f, slots, gate, counts = _merge_route(
            h, (sc1, sh1, g1, sc2, sh2), norm1_g[l].reshape(1, d), norm2_g[l].reshape(1, d),
            w_gate, w_pa[l].astype(BF16), w_pg[l].astype(BF16), w_o[l].astype(BF16),
            outs, lses, gm, w_router[l].T, b_router[l].reshape(-1, 1), seq)

        plan = _slot_plan(counts[:, 0].reshape(-1, N_EXPERTS), n)
        buf = _dispatch(plan, slots, f, n_blocks)
        obuf = _experts(l, plan["block_e"], plan["first"], plan["n_valid"], buf,
                        w1, b1r, w2, b2r, n_blocks)

        h = _combine(plan, slots, gate, h, g2, final_g.reshape(1, d), obuf, seq,
                     final_norm=(l == depth - 1))
    return h.reshape(batch, seq, d)
```

```python
import functools

import jax
import jax.numpy as jnp
from jax import lax
from jax.experimental import pallas as pl
from jax.experimental.pallas import tpu as pltpu

F32 = jnp.float32
BF16 = jnp.bfloat16
I32 = jnp.int32

HEAD_DIM = 64
HEADS_PER_GROUP = 4
GROUP_W = HEADS_PER_GROUP * HEAD_DIM
ATTN_GROUPS = ((128, 1), (512, 4), (2048, 16))
N_GROUPS = len(ATTN_GROUPS)
ATTN_W = N_GROUPS * GROUP_W
WIN_BLOCK = 128
ROPE_THETA = 10000.0
GMLP_CHUNK = 128
GMLP_GROUPS = 4
GMLP_W = GMLP_GROUPS * 128
N_EXPERTS = 32
TOP_K = 4
SWIGLU_ALPHA = 1.702
SWIGLU_LIMIT = 7.0
RMS_EPS = 1e-5
LN_EPS = 1e-5
NEG_INF = -1e30

TOKEN_TILE = 512
EXPERT_BLOCK = 512
ROW_ALIGN = 8
BIG_PIECE = 32
_BIG_SHIFT = (BIG_PIECE // ROW_ALIGN).bit_length() - 1
SORT_CHUNK = 256
VMEM_LIMIT = 56 * 1024 * 1024

_ARB = lambda n: pltpu.CompilerParams(dimension_semantics=("arbitrary",) * n,
                                      vmem_limit_bytes=VMEM_LIMIT)


def _bdot(a, b):
    return jnp.dot(a, b, preferred_element_type=F32)


def _split3(x):
    hi = x.astype(BF16)
    r1 = x - hi.astype(F32)
    mid = r1.astype(BF16)
    lo = (r1 - mid.astype(F32)).astype(BF16)
    return hi, mid, lo


def _dot_f32(a, b):
    a0, a1, a2 = _split3(a)
    b0, b1, b2 = _split3(b)
    small = _bdot(a0, b2) + _bdot(a1, b1) + _bdot(a2, b0)
    mid = _bdot(a0, b1) + _bdot(a1, b0)
    return (small + mid) + _bdot(a0, b0)


def _mod_kernel(c_ref, w_ref, b_ref, o_ref):
    c = c_ref[...]
    s = c * jax.nn.sigmoid(c)
    o_ref[0] = _dot_f32(s, w_ref[0]) + b_ref[0]


def _modulation(c, w_ada, b_ada):
    depth, d, six_d = w_ada.shape
    b = c.shape[0]
    n_col = six_d // d
    return pl.pallas_call(
        _mod_kernel,
        grid=(depth, n_col),
        in_specs=[pl.BlockSpec((b, d), lambda l, j: (0, 0)),
                  pl.BlockSpec((1, d, d), lambda l, j: (l, 0, j)),
                  pl.BlockSpec((1, 1, d), lambda l, j: (l, 0, j))],
        out_specs=pl.BlockSpec((1, b, d), lambda l, j: (l, 0, j)),
        out_shape=jax.ShapeDtypeStruct((depth, b, six_d), F32),
        compiler_params=_ARB(2),
        name="adaln_mod",
    )(c, w_ada, b_ada.reshape(depth, 1, six_d))


def _rope_kernel(pos_ref, freq_ref, sign_ref, cos_ref, sin_ref):
    ang = pos_ref[...].astype(F32) * freq_ref[...]
    cos_ref[...] = jnp.cos(ang)
    sin_ref[...] = jnp.sin(ang) * sign_ref[...]


def _rope_tables(positions):
    n = positions.size
    half = HEAD_DIM // 2
    inv_freq = ROPE_THETA ** (-jnp.arange(half, dtype=F32) / half)
    freq = jnp.tile(inv_freq, 4).reshape(1, 128)
    sign = jnp.tile(jnp.concatenate([-jnp.ones((half,), F32), jnp.ones((half,), F32)]), 2)
    sign = sign.reshape(1, 128)
    tile = 1024
    return pl.pallas_call(
        _rope_kernel,
        grid=(n // tile,),
        in_specs=[pl.BlockSpec((tile, 1), lambda i: (i, 0)),
                  pl.BlockSpec((1, 128), lambda i: (0, 0)),
                  pl.BlockSpec((1, 128), lambda i: (0, 0))],
        out_specs=[pl.BlockSpec((tile, 128), lambda i: (i, 0))] * 2,
        out_shape=[jax.ShapeDtypeStruct((n, 128), F32)] * 2,
        compiler_params=_ARB(1),
        name="rope_tables",
    )(positions.reshape(n, 1), freq, sign)


def _modulated_norm(x, g, sc, sh):
    ms = jnp.mean(x * x, axis=-1, keepdims=True)
    y = x * lax.rsqrt(ms + RMS_EPS) * g
    return y * (1.0 + sc) + sh


def _inproj_kernel(h_ref, sc_ref, sh_ref, g_ref, w_ref, cos_ref, sin_ref,
                   ws_ref, bs_ref, lng_ref, lnb_ref, qkv0_ref, qkv1_ref, qkv2_ref, gm_ref,
                   slab_ref):
    tm = h_ref.shape[0]
    a = _modulated_norm(h_ref[...], g_ref[...], sc_ref[0], sh_ref[0]).astype(BF16)
    cos = cos_ref[...]
    sin = sin_ref[...]
    lane = lax.broadcasted_iota(I32, (tm, 128), 1)
    first_half = (lane % HEAD_DIM) < (HEAD_DIM // 2)

    def rope(x):
        swapped = jnp.where(first_half, pltpu.roll(x, 96, 1), pltpu.roll(x, 32, 1))
        return x * cos + swapped * sin

    for g, out_ref in enumerate((qkv0_ref, qkv1_ref, qkv2_ref)):
        r = ATTN_GROUPS[g][1]
        for part in range(3):
            c0 = part * ATTN_W + g * GROUP_W
            y = _bdot(a, w_ref[:, c0:c0 + GROUP_W])
            halves = [y[:, :128], y[:, 128:]]
            if part < 2:
                halves = [rope(x) for x in halves]
            if part == 0:
                halves = [x * (HEAD_DIM ** -0.5) for x in halves]
            for c, x in enumerate(halves):
                o0 = part * GROUP_W + c * 128
                if r == 1:
                    out_ref[0, 0, :, o0:o0 + 128] = x.astype(BF16)
                else:
                    slab_ref[...] = x
                    for j in range(r):
                        rows = slab_ref[pl.ds(j, tm // r, stride=r), :]
                        out_ref[0, j, :, o0:o0 + 128] = rows.astype(BF16)

    c0 = 3 * ATTN_W
    u = jax.nn.gelu(_bdot(a, w_ref[:, c0:c0 + GMLP_W]))
    v = jax.nn.gelu(_bdot(a, w_ref[:, c0 + GMLP_W:c0 + 2 * GMLP_W]))
    mu = jnp.mean(v, axis=-1, keepdims=True)
    var = jnp.mean(jnp.square(v - mu), axis=-1, keepdims=True)
    v = ((v - mu) * lax.rsqrt(var + LN_EPS) * lng_ref[...] + lnb_ref[...]).astype(BF16)
    row = lax.broadcasted_iota(I32, (GMLP_CHUNK, GMLP_CHUNK), 0)
    col = lax.broadcasted_iota(I32, (GMLP_CHUNK, GMLP_CHUNK), 1)
    tril = col <= row
    for g in range(GMLP_GROUPS):
        wsg = jnp.where(tril, ws_ref[g], 0.0).astype(BF16)
        bias = bs_ref[:, g:g + 1]
        cs = slice(g * 128, (g + 1) * 128)
        for c in range(tm // GMLP_CHUNK):
            rs = slice(c * GMLP_CHUNK, (c + 1) * GMLP_CHUNK)
            mixed = _bdot(wsg, v[rs, cs]) + bias
            gm_ref[rs, cs] = (u[rs, cs] * mixed).astype(BF16)


def _inproj(h, sc, sh, g, w, cos, sin, w_s, b_s_t, ln_g, ln_b, seq):
    n, d = h.shape
    tm = TOKEN_TILE
    per_b = seq // tm
    batch = n // seq
    wn = w.shape[1]
    row = lambda i: (i, 0)
    bat = lambda i: (i // per_b, 0, 0)
    full2 = lambda i: (0, 0)
    res = lambda i: (i // per_b, 0, i % per_b, 0)
    qkv_specs = [pl.BlockSpec((1, r, tm // r, 3 * GROUP_W), res) for _, r in ATTN_GROUPS]
    qkv_shapes = [jax.ShapeDtypeStruct((batch, r, seq // r, 3 * GROUP_W), BF16)
                  for _, r in ATTN_GROUPS]
    return pl.pallas_call(
        _inproj_kernel,
        grid=(n // tm,),
        in_specs=[pl.BlockSpec((tm, d), row),
                  pl.BlockSpec((1, 1, d), bat),
                  pl.BlockSpec((1, 1, d), bat),
                  pl.BlockSpec((1, d), full2),
                  pl.BlockSpec((d, wn), full2),
                  pl.BlockSpec((tm, 128), row),
                  pl.BlockSpec((tm, 128), row),
                  pl.BlockSpec((GMLP_GROUPS, GMLP_CHUNK, GMLP_CHUNK), lambda i: (0, 0, 0)),
                  pl.BlockSpec((GMLP_CHUNK, GMLP_GROUPS), full2),
                  pl.BlockSpec((1, GMLP_W), full2),
                  pl.BlockSpec((1, GMLP_W), full2)],
        out_specs=qkv_specs + [pl.BlockSpec((tm, GMLP_W), row)],
        out_shape=qkv_shapes + [jax.ShapeDtypeStruct((n, GMLP_W), BF16)],
        scratch_shapes=[pltpu.VMEM((tm, 128), F32)],
        compiler_params=_ARB(1),
        name="inproj_rope_gmlp",
    )(h, sc, sh, g, w, cos, sin, w_s, b_s_t, ln_g, ln_b)


def _attn_kernel(q_ref, k_ref, v_ref, kp_ref, vp_ref, o_ref, lse_ref):
    q_ref, k_ref, v_ref, kp_ref, vp_ref, o_ref, lse_ref = (
        x.at[0] for x in (q_ref, k_ref, v_ref, kp_ref, vp_ref, o_ref, lse_ref))
    qb = q_ref.shape[1]
    first_key = jnp.where(pl.program_id(2) == 0, WIN_BLOCK, 0)
    qi = lax.broadcasted_iota(I32, (WIN_BLOCK, 2 * WIN_BLOCK), 0)
    kj = lax.broadcasted_iota(I32, (WIN_BLOCK, 2 * WIN_BLOCK), 1)
    band = (kj >= qi) & (kj <= qi + WIN_BLOCK)
    for i in range(qb // WIN_BLOCK):
        rs = slice(i * WIN_BLOCK, (i + 1) * WIN_BLOCK)
        q = q_ref[0, rs, :]
        if i == 0:
            k_prev, v_prev = kp_ref[0], vp_ref[0]
            valid = band & (kj >= first_key)
        else:
            ps = slice((i - 1) * WIN_BLOCK, i * WIN_BLOCK)
            k_prev, v_prev = k_ref[0, ps, :], v_ref[0, ps, :]
            valid = band
        k_cat = jnp.concatenate([k_prev, k_ref[0, rs, :]], axis=0)
        v_cat = jnp.concatenate([v_prev, v_ref[0, rs, :]], axis=0)
        outs, lses = [], []
        for h in range(HEADS_PER_GROUP):
            hs = slice(h * HEAD_DIM, (h + 1) * HEAD_DIM)
            s = lax.dot_general(q[:, hs], k_cat[:, hs], (((1,), (1,)), ((), ())),
                                preferred_element_type=F32)
            s = jnp.where(valid, s, NEG_INF)
            m = jnp.max(s, axis=-1, keepdims=True)
            p = jnp.exp(s - m)
            den = jnp.sum(p, axis=-1, keepdims=True)
            o = _bdot(p.astype(BF16), v_cat[:, hs]) / den
            outs.append(o)
            lses.append(jnp.broadcast_to(m + jnp.log(den), (WIN_BLOCK, HEAD_DIM)))
        o_ref[0, rs, :] = jnp.concatenate(outs, axis=1).astype(o_ref.dtype)
        lse_ref[0, rs, :] = jnp.concatenate(lses, axis=1)


def _attention_group(qkv, g):
    batch, r, sub_len, _ = qkv.shape
    qb = min(512, sub_len)
    n_tiles = sub_len // qb
    per_tile = qb // WIN_BLOCK

    def cur(part):
        return pl.BlockSpec((1, 1, qb, GROUP_W), lambda b, j, n: (b, j, n, part))

    def prev(part):
        return pl.BlockSpec((1, 1, WIN_BLOCK, GROUP_W),
                            lambda b, j, n: (b, j, jnp.maximum(n * per_tile - 1, 0), part))

    out_spec = pl.BlockSpec((1, 1, qb, GROUP_W), lambda b, j, n: (b, j, n, 0))
    return pl.pallas_call(
        _attn_kernel,
        grid=(batch, r, n_tiles),
        in_specs=[cur(0), cur(1), cur(2), prev(1), prev(2)],
        out_specs=[out_spec, out_spec],
        out_shape=[jax.ShapeDtypeStruct((batch, r, sub_len, GROUP_W), BF16),
                   jax.ShapeDtypeStruct((batch, r, sub_len, GROUP_W), F32)],
        compiler_params=_ARB(3),
        name=f"dilated_attn_g{g}",
    )(qkv, qkv, qkv, qkv, qkv)


def _split_bf16(x):
    hi = x.astype(BF16)
    lo = (x - hi.astype(F32)).astype(BF16)
    return hi, lo


def _merge_kernel(h_ref, sc1_ref, sh1_ref, g1_ref, sc2_ref, sh2_ref, n1_ref, n2_ref,
                  wg_ref, wpa_ref, wpg_ref, wo_ref,
                  o0_ref, o1_ref, o2_ref, l0_ref, l1_ref, l2_ref, gm_ref,
                  wr_ref, br_ref,
                  hn_ref, f_ref, slot_ref, gate_ref, cnt_ref, slab_ref):
    tm, d = h_ref.shape
    h = h_ref[...]
    a = _modulated_norm(h, n1_ref[...], sc1_ref[0], sh1_ref[0]).astype(BF16)
    gates = jax.nn.sigmoid(_bdot(a, wg_ref[...]))

    def token_major(ref):
        r = ref.shape[1]
        if r == 1:
            return ref[0, 0].astype(F32)
        halves = []
        for c in range(GROUP_W // 128):
            for j in range(r):
                slab_ref[pl.ds(j, tm // r, stride=r), :] = (
                    ref[0, j, :, c * 128:(c + 1) * 128].astype(F32))
            halves.append(slab_ref[...])
        return jnp.concatenate(halves, axis=1)

    lses = [token_major(x) for x in (l0_ref, l1_ref, l2_ref)]
    outs = [token_major(x) for x in (o0_ref, o1_ref, o2_ref)]
    m = jnp.maximum(jnp.maximum(lses[0], lses[1]), lses[2])
    es = [jnp.exp(l - m) for l in lses]
    den = es[0] + es[1] + es[2]
    attn = (es[0] * outs[0] + es[1] * outs[1] + es[2] * outs[2]) / den

    merged = (gates[:, :d] * _bdot(attn.astype(BF16), wpa_ref[...])
              + gates[:, d:] * _bdot(gm_ref[...], wpg_ref[...]))
    hn = h + g1_ref[0] * _bdot(merged.astype(BF16), wo_ref[...])
    hn_ref[...] = hn

    f = _modulated_norm(hn, n2_ref[...], sc2_ref[0], sh2_ref[0])
    f_ref[...] = f.astype(BF16)

    f_hi, f_lo = _split_bf16(f)
    w_hi, w_lo = _split_bf16(wr_ref[...])
    nt = (((1,), (1,)), ((), ()))
    logits = (lax.dot_general(w_hi, f_hi, nt, preferred_element_type=F32)
              + lax.dot_general(w_hi, f_lo, nt, preferred_element_type=F32)
              + lax.dot_general(w_lo, f_hi, nt, preferred_element_type=F32)) + br_ref[...]

    n_e = logits.shape[0]
    erow = lax.broadcasted_iota(I32, (n_e, tm), 0).astype(F32)
    vals, idxs, sels = [], [], []
    cur = logits
    for _ in range(TOP_K):
        mx = jnp.max(cur, axis=0, keepdims=True)
        ix = jnp.min(jnp.where(cur == mx, erow, float(n_e)), axis=0, keepdims=True)
        sel = erow == ix
        vals.append(mx)
        idxs.append(ix)
        sels.append(sel)
        cur = jnp.where(sel, -jnp.inf, cur)
    exps = [jnp.exp(v - vals[0]) for v in vals]
    tot = exps[0] + exps[1] + exps[2] + exps[3]
    gate_ref[...] = jnp.concatenate([e / tot for e in exps], axis=0)

    onehot = (sels[0] | sels[1] | sels[2] | sels[3])
    oh = jnp.where(onehot, 1.0, 0.0)
    t_r = lax.broadcasted_iota(I32, (tm, tm), 0)
    t_c = lax.broadcasted_iota(I32, (tm, tm), 1)
    upper = jnp.where(t_r < t_c, 1.0, 0.0).astype(BF16)
    prefix = _bdot(oh.astype(BF16), upper)
    cnt = jnp.sum(oh, axis=1, keepdims=True)
    cnt_al = jnp.floor((cnt + (ROW_ALIGN - 1)) * (1.0 / ROW_ALIGN)) * ROW_ALIGN
    e_r = lax.broadcasted_iota(I32, (n_e, n_e), 0)
    e_c = lax.broadcasted_iota(I32, (n_e, n_e), 1)
    lower = jnp.where(e_c < e_r, 1.0, 0.0).astype(BF16)
    start = _bdot(lower, jnp.broadcast_to(cnt_al, (n_e, 128)).astype(BF16))[:, 0:1]
    local = prefix + start
    slots = [jnp.sum(jnp.where(s, local, 0.0), axis=0, keepdims=True) for s in sels]
    slot_ref[...] = jnp.concatenate(slots, axis=0).astype(I32)
    cnt_ref[...] = jnp.broadcast_to(cnt, cnt_ref.shape).astype(I32)


def _merge_route(h, mods, n1, n2, wg, wpa, wpg, wo, outs, lses, gm, wr, br, seq):
    n, d = h.shape
    tm = TOKEN_TILE
    per_b = seq // tm
    row = lambda i: (i, 0)
    bat = lambda i: (i // per_b, 0, 0)
    full2 = lambda i: (0, 0)
    col = lambda i: (0, i)
    sc1, sh1, g1, sc2, sh2 = mods
    res = lambda i: (i // per_b, 0, i % per_b, 0)
    attn_specs = [pl.BlockSpec((1, r, tm // r, GROUP_W), res) for _, r in ATTN_GROUPS]
    in_specs = ([pl.BlockSpec((tm, d), row)]
                + [pl.BlockSpec((1, 1, d), bat)] * 5
                + [pl.BlockSpec((1, d), full2)] * 2
                + [pl.BlockSpec(wg.shape, full2), pl.BlockSpec(wpa.shape, full2),
                   pl.BlockSpec(wpg.shape, full2), pl.BlockSpec(wo.shape, full2)]
                + attn_specs * 2
                + [pl.BlockSpec((tm, GMLP_W), row)]
                + [pl.BlockSpec(wr.shape, full2), pl.BlockSpec(br.shape, full2)])
    out_specs = [pl.BlockSpec((tm, d), row), pl.BlockSpec((tm, d), row),
                 pl.BlockSpec((TOP_K, tm), col), pl.BlockSpec((TOP_K, tm), col),
                 pl.BlockSpec((N_EXPERTS, 128), row)]
    out_shape = [jax.ShapeDtypeStruct((n, d), F32), jax.ShapeDtypeStruct((n, d), BF16),
                 jax.ShapeDtypeStruct((TOP_K, n), I32), jax.ShapeDtypeStruct((TOP_K, n), F32),
                 jax.ShapeDtypeStruct((n // tm * N_EXPERTS, 128), I32)]
    return pl.pallas_call(
        _merge_kernel,
        grid=(n // tm,),
        in_specs=in_specs,
        out_specs=out_specs,
        out_shape=out_shape,
        scratch_shapes=[pltpu.VMEM((tm, 128), F32)],
        compiler_params=_ARB(1),
        name="merge_proj_route",
    )(h, sc1, sh1, g1, sc2, sh2, n1, n2, wg, wpa, wpg, wo, *outs, *lses, gm, wr, br)


def _row_copy(src, dst, sem):
    return pltpu.make_async_copy(src, dst, sem)


def _start_runs(tile, off_ref, start_ref, pieces_ref, copy):
    def per_expert(e, total):
        base = tile * N_EXPERTS + e
        local0, global0, n = start_ref[base], off_ref[base], pieces_ref[base]
        n_big = lax.shift_right_logical(n, _BIG_SHIFT)

        def big(i, carry):
            row = pl.multiple_of(i * BIG_PIECE, ROW_ALIGN)
            copy(pl.multiple_of(local0 + row, ROW_ALIGN),
                 pl.multiple_of(global0 + row, ROW_ALIGN), BIG_PIECE).start()
            return carry

        def small(i, carry):
            row = pl.multiple_of(n_big * BIG_PIECE + i * ROW_ALIGN, ROW_ALIGN)
            copy(pl.multiple_of(local0 + row, ROW_ALIGN),
                 pl.multiple_of(global0 + row, ROW_ALIGN), ROW_ALIGN).start()
            return carry

        lax.fori_loop(0, n_big, big, 0)
        lax.fori_loop(0, n - lax.shift_left(n_big, _BIG_SHIFT), small, 0)
        return total + n

    return lax.fori_loop(0, N_EXPERTS, per_expert, 0)


def _wait_runs(total, copy):
    n_big = lax.shift_right_logical(total, _BIG_SHIFT)
    lax.fori_loop(0, n_big, lambda i, c: (copy(0, 0, BIG_PIECE).wait(), c)[1], 0)
    lax.fori_loop(0, total - lax.shift_left(n_big, _BIG_SHIFT),
                  lambda i, c: (copy(0, 0, ROW_ALIGN).wait(), c)[1], 0)


def _slot_hits(slot_ref, chunk):
    tm = slot_ref.shape[1]
    s = lax.broadcasted_iota(I32, (SORT_CHUNK, tm), 0) + chunk * SORT_CHUNK
    return [slot_ref[k:k + 1, :] == s for k in range(TOP_K)]


def _dispatch_kernel(zb_ref, nz_ref, off_ref, start_ref, pieces_ref, slot_ref, f_ref, buf_ref,
                     sorted_ref, zero_ref, pending_ref, sems, zsem):
    step = pl.program_id(0)
    buf_slot = lax.rem(step, 2)
    sorted_now = sorted_ref.at[buf_slot]

    @pl.when(step == 0)
    def _():
        zero_ref[...] = jnp.zeros_like(zero_ref)

        def zero_copy(j):
            start = pl.multiple_of(zb_ref[j] * EXPERT_BLOCK, EXPERT_BLOCK)
            return _row_copy(zero_ref, buf_ref.at[pl.ds(start, EXPERT_BLOCK)], zsem)

        lax.fori_loop(0, nz_ref[0], lambda j, c: (zero_copy(j).start(), c)[1], 0)
        lax.fori_loop(0, nz_ref[0], lambda j, c: (zero_copy(j).wait(), c)[1], 0)

    f = f_ref[...]
    for c in range(sorted_now.shape[0] // SORT_CHUNK):
        hits = _slot_hits(slot_ref, c)
        onehot = jnp.where(hits[0] | hits[1] | hits[2] | hits[3], 1.0, 0.0).astype(BF16)
        sorted_now[c * SORT_CHUNK:(c + 1) * SORT_CHUNK, :] = _bdot(onehot, f)

    def copy_from(slot):
        def copy(local_row, global_row, rows):
            return _row_copy(sorted_ref.at[slot, pl.ds(local_row, rows)],
                             buf_ref.at[pl.ds(global_row, rows)], sems.at[slot])
        return copy

    total = _start_runs(step, off_ref, start_ref, pieces_ref, copy_from(buf_slot))

    @pl.when(step > 0)
    def _():
        _wait_runs(pending_ref[0], copy_from(1 - buf_slot))

    pending_ref[0] = total

    @pl.when(step == pl.num_programs(0) - 1)
    def _():
        _wait_runs(total, copy_from(buf_slot))


def _dispatch(plan, slots, f, n_blocks):
    n, d = f.shape
    tm = TOKEN_TILE
    return pl.pallas_call(
        _dispatch_kernel,
        grid_spec=pltpu.PrefetchScalarGridSpec(
            num_scalar_prefetch=5, grid=(n // tm,),
            in_specs=[pl.BlockSpec((TOP_K, tm), lambda i, *_: (0, i)),
                      pl.BlockSpec((tm, d), lambda i, *_: (i, 0))],
            out_specs=pl.BlockSpec(memory_space=pl.ANY),
            scratch_shapes=[pltpu.VMEM((2, _local_rows(tm), d), F32),
                            pltpu.VMEM((EXPERT_BLOCK, d), F32),
                            pltpu.SMEM((1,), I32),
                            pltpu.SemaphoreType.DMA((2,)), pltpu.SemaphoreType.DMA(())]),
        out_shape=jax.ShapeDtypeStruct((n_blocks * EXPERT_BLOCK, d), F32),
        compiler_params=_ARB(1),
        name="dispatch_rows",
    )(plan["zero_blocks"], plan["n_zero"], plan["run_off"], plan["run_start"],
      plan["run_pieces"], slots, f)


def _expert_kernel(be_ref, first_ref, nv_ref, x_ref, w1_ref, b1_ref, w2_ref, b2_ref, o_ref,
                   w1b_ref, w2p_ref, slab_ref):
    del be_ref
    step = pl.program_id(0)
    live = step < nv_ref[0]
    ff, d = w2p_ref.shape
    half = 64

    @pl.when(live & (first_ref[step] == 1))
    def _():
        w1b_ref[...] = w1_ref[0, 0].astype(BF16)
        for c in range(d // 128):
            cs = slice(c * 128, (c + 1) * 128)
            for k in range(ff // 128):
                r0 = k * 128
                slab_ref[pl.ds(r0, half, stride=2), :] = w2_ref[0, 0, r0:r0 + half, cs]
                slab_ref[pl.ds(r0 + 1, half, stride=2), :] = w2_ref[0, 0, r0 + half:r0 + 128, cs]
            w2p_ref[:, cs] = slab_ref[...].astype(BF16)

    @pl.when(live)
    def _():
        x = x_ref[...].astype(BF16)
        hb = _bdot(x, w1b_ref[...]) + b1_ref[0, 0]
        lane = lax.broadcasted_iota(I32, (x.shape[0], 128), 1)
        even = (lane & 1) == 0
        glu, lin = [], []
        for k in range(ff // 128):
            a = hb[:, 256 * k:256 * k + 128]
            b = hb[:, 256 * k + 128:256 * k + 256]
            glu.append(jnp.where(even, a, pltpu.roll(b, 1, 1)))
            lin.append(jnp.where(even, pltpu.roll(a, 127, 1), b))
        x_glu = jnp.minimum(jnp.concatenate(glu, axis=1), SWIGLU_LIMIT)
        x_lin = jnp.clip(jnp.concatenate(lin, axis=1), -SWIGLU_LIMIT, SWIGLU_LIMIT)
        act = x_glu * jax.nn.sigmoid(SWIGLU_ALPHA * x_glu) * (x_lin + 1.0)
        o_ref[...] = _bdot(act.astype(BF16), w2p_ref[...]) + b2_ref[0, 0]

    @pl.when(jnp.logical_not(live))
    def _():
        o_ref[...] = jnp.zeros_like(o_ref)


def _experts(layer, block_e, first, n_valid, buf, w1, b1, w2, b2, n_blocks):
    d = buf.shape[1]
    ff = w2.shape[2]
    xmap = lambda i, be, fi, nv: (jnp.where(i < nv[0], i, 0), 0)
    wmap = lambda i, be, fi, nv: (layer, be[i], 0, 0)
    return pl.pallas_call(
        _expert_kernel,
        grid_spec=pltpu.PrefetchScalarGridSpec(
            num_scalar_prefetch=3, grid=(n_blocks,),
            in_specs=[pl.BlockSpec((EXPERT_BLOCK, d), xmap),
                      pl.BlockSpec((1, 1, d, 2 * ff), wmap),
                      pl.BlockSpec((1, 1, 1, 2 * ff), wmap),
                      pl.BlockSpec((1, 1, ff, d), wmap),
                      pl.BlockSpec((1, 1, 1, d), wmap)],
            out_specs=pl.BlockSpec((EXPERT_BLOCK, d), lambda i, be, fi, nv: (i, 0)),
            scratch_shapes=[pltpu.VMEM((d, 2 * ff), BF16), pltpu.VMEM((ff, d), BF16),
                            pltpu.VMEM((ff, 128), F32)]),
        out_shape=jax.ShapeDtypeStruct((n_blocks * EXPERT_BLOCK, d), F32),
        compiler_params=_ARB(1),
        name="expert_ffn",
    )(block_e, first, n_valid, buf, w1, b1, w2, b2)


def _combine_kernel(off_ref, start_ref, pieces_ref, slot_ref, gate_ref, h_ref, g2_ref, fin_ref,
                    obuf_ref, out_ref, rows_ref, sems, *, final_norm):
    step = pl.program_id(0)
    buf_slot = lax.rem(step, 2)
    rows_now = rows_ref.at[buf_slot]

    def copy_into(slot):
        def copy(local_row, global_row, rows):
            return _row_copy(obuf_ref.at[pl.ds(global_row, rows)],
                             rows_ref.at[slot, pl.ds(local_row, rows)], sems.at[slot])
        return copy

    @pl.when(step == 0)
    def _():
        rows_ref[...] = jnp.zeros_like(rows_ref)
        _start_runs(0, off_ref, start_ref, pieces_ref, copy_into(0))

    @pl.when(step + 1 < pl.num_programs(0))
    def _():
        _start_runs(step + 1, off_ref, start_ref, pieces_ref, copy_into(1 - buf_slot))

    total = lax.fori_loop(0, N_EXPERTS,
                          lambda e, t: t + pieces_ref[step * N_EXPERTS + e], 0)
    _wait_runs(total, copy_into(buf_slot))

    y = jnp.zeros(h_ref.shape, F32)
    for c in range(rows_now.shape[0] // SORT_CHUNK):
        hits = _slot_hits(slot_ref, c)
        weight = jnp.where(hits[0], gate_ref[0:1, :], 0.0)
        for k in range(1, TOP_K):
            weight = weight + jnp.where(hits[k], gate_ref[k:k + 1, :], 0.0)
        rows = rows_now[c * SORT_CHUNK:(c + 1) * SORT_CHUNK, :].astype(BF16)
        y = y + lax.dot_general(weight.astype(BF16), rows, (((0,), (0,)), ((), ())),
                                preferred_element_type=F32)
    hn = h_ref[...] + g2_ref[0] * y
    if final_norm:
        ms = jnp.mean(hn * hn, axis=-1, keepdims=True)
        hn = hn * lax.rsqrt(ms + RMS_EPS) * fin_ref[...]
    out_ref[...] = hn


def _combine(plan, slots, gate, h, g2, final_g, obuf, seq, final_norm):
    n, d = h.shape
    tm = TOKEN_TILE
    per_b = seq // tm
    return pl.pallas_call(
        functools.partial(_combine_kernel, final_norm=final_norm),
        grid_spec=pltpu.PrefetchScalarGridSpec(
            num_scalar_prefetch=3, grid=(n // tm,),
            in_specs=[pl.BlockSpec((TOP_K, tm), lambda i, *_: (0, i)),
                      pl.BlockSpec((TOP_K, tm), lambda i, *_: (0, i)),
                      pl.BlockSpec((tm, d), lambda i, *_: (i, 0)),
                      pl.BlockSpec((1, 1, d), lambda i, *_: (i // per_b, 0, 0)),
                      pl.BlockSpec((1, d), lambda i, *_: (0, 0)),
                      pl.BlockSpec(memory_space=pl.ANY)],
            out_specs=pl.BlockSpec((tm, d), lambda i, *_: (i, 0)),
            scratch_shapes=[pltpu.VMEM((2, _local_rows(tm), d), F32),
                            pltpu.SemaphoreType.DMA((2,))]),
        out_shape=jax.ShapeDtypeStruct((n, d), F32),
        compiler_params=_ARB(1),
        name="combine_rows",
    )(plan["run_off"], plan["run_start"], plan["run_pieces"], slots, gate, h, g2, final_g, obuf)


def _local_rows(tm):
    rows = tm * TOP_K + N_EXPERTS * (ROW_ALIGN - 1)
    return -(-rows // SORT_CHUNK) * SORT_CHUNK


def _num_blocks(n):
    rows = n * TOP_K + (n // TOKEN_TILE) * N_EXPERTS * (ROW_ALIGN - 1)
    return -(-rows // EXPERT_BLOCK) + N_EXPERTS


def _slot_plan(tile_counts, n):
    blk = EXPERT_BLOCK
    n_blocks = _num_blocks(n)
    cnt = (tile_counts + ROW_ALIGN - 1) // ROW_ALIGN * ROW_ALIGN
    total = jnp.sum(cnt, axis=0)
    padded = (total + blk - 1) // blk * blk
    pad_end = jnp.cumsum(padded)
    pad_start = pad_end - padded
    run_off = pad_start[None, :] + jnp.cumsum(cnt, axis=0) - cnt
    run_start = jnp.cumsum(cnt, axis=1) - cnt
    n_valid = pad_end[-1] // blk
    block_start = jnp.arange(n_blocks, dtype=I32) * blk
    block_e = jnp.sum(pad_end[None, :] <= block_start[:, None], axis=1)
    last_e = jnp.sum(pad_end <= (n_valid - 1) * blk)
    block_e = jnp.where(jnp.arange(n_blocks) < n_valid, block_e, last_e).astype(I32)
    first = jnp.concatenate([jnp.ones((1,), I32),
                             (block_e[1:] != block_e[:-1]).astype(I32)])
    tails = jnp.where(total > 0, pad_end // blk - 1, -1)
    spare = jnp.arange((n * TOP_K) // blk, n_blocks)
    cand = jnp.concatenate([tails, jnp.where(spare >= n_valid, spare, -1)]).astype(I32)
    zero_blocks = cand[jnp.argsort(cand < 0, stable=True)]
    flat = lambda a: a.reshape(-1).astype(I32)
    return dict(block_e=block_e, first=first, n_valid=n_valid.reshape(1).astype(I32),
                zero_blocks=jnp.maximum(zero_blocks, 0),
                n_zero=jnp.sum(cand >= 0).reshape(1).astype(I32),
                run_off=flat(run_off), run_start=flat(run_start),
                run_pieces=flat(cnt // ROW_ALIGN))


def kernel(x, c, positions, w_ada, b_ada, norm1_g, w_in, w_s, b_s, ln_g, ln_b, w_pa, w_pg,
           w_o, norm2_g, w_router, b_router, w1, b1, w2, b2, final_g):
    batch, seq, d = x.shape
    depth = w_ada.shape[0]
    n = batch * seq
    n_blocks = _num_blocks(n)

    mod = _modulation(c, w_ada, b_ada)
    cos, sin = _rope_tables(positions)

    n_mix = 3 * ATTN_W + 2 * GMLP_W
    b1r = b1.reshape(depth, N_EXPERTS, 1, -1)
    b2r = b2.reshape(depth, N_EXPERTS, 1, -1)

    h = x.reshape(n, d)
    for l in range(depth):
        ml = mod[l].reshape(batch, 6, 1, d)
        sh1, sc1, g1, sh2, sc2, g2 = [ml[:, i] for i in range(6)]
        w_mix = w_in[l, :, :n_mix].astype(BF16)
        w_gate = w_in[l, :, n_mix:].astype(BF16)

        *qkvs, gm = _inproj(h, sc1, sh1, norm1_g[l].reshape(1, d), w_mix, cos, sin,
                            w_s[l], b_s[l].T, ln_g[l].reshape(1, -1), ln_b[l].reshape(1, -1), seq)
        outs, lses = zip(*[_attention_group(qkvs[g], g) for g in range(N_GROUPS)])

        h, f, slots, gate, counts = _merge_route(
            h, (sc1, sh1, g1, sc2, sh2), norm1_g[l].reshape(1, d), norm2_g[l].reshape(1, d),
            w_gate, w_pa[l].astype(BF16), w_pg[l].astype(BF16), w_o[l].astype(BF16),
            outs, lses, gm, w_router[l].T, b_router[l].reshape(-1, 1), seq)

        plan = _slot_plan(counts[:, 0].reshape(-1, N_EXPERTS), n)
        buf = _dispatch(plan, slots, f, n_blocks)
        obuf = _experts(l, plan["block_e"], plan["first"], plan["n_valid"], buf,
                        w1, b1r, w2, b2r, n_blocks)

        h = _combine(plan, slots, gate, h, g2, final_g.reshape(1, d), obuf, seq,
                     final_norm=(l == depth - 1))
    return h.reshape(batch, seq, d)
```

```python
import functools

import jax
import jax.numpy as jnp
from jax import lax
from jax.experimental import pallas as pl
from jax.experimental.pallas import tpu as pltpu

F32 = jnp.float32
BF16 = jnp.bfloat16
I32 = jnp.int32

HEAD_DIM = 64
HEADS_PER_GROUP = 4
GROUP_W = HEADS_PER_GROUP * HEAD_DIM
ATTN_GROUPS = ((128, 1), (512, 4), (2048, 16))
N_GROUPS = len(ATTN_GROUPS)
ATTN_W = N_GROUPS * GROUP_W
WIN_BLOCK = 128
ROPE_THETA = 10000.0
GMLP_CHUNK = 128
GMLP_GROUPS = 4
GMLP_W = GMLP_GROUPS * 128
N_EXPERTS = 32
TOP_K = 4
SWIGLU_ALPHA = 1.702
SWIGLU_LIMIT = 7.0
RMS_EPS = 1e-5
LN_EPS = 1e-5
NEG_INF = -1e30

TOKEN_TILE = 512
MERGE_ROWS = 512
EXPERT_BLOCK = 512
ROW_ALIGN = 8
BIG_PIECE = 32
_BIG_SHIFT = (BIG_PIECE // ROW_ALIGN).bit_length() - 1
SORT_CHUNK = 256
VMEM_LIMIT = 56 * 1024 * 1024

_ARB = lambda n: pltpu.CompilerParams(dimension_semantics=("arbitrary",) * n,
                                      vmem_limit_bytes=VMEM_LIMIT)


def _bdot(a, b):
    return jnp.dot(a, b, preferred_element_type=F32)


def _sigmoid(x):
    return 0.5 * jnp.tanh(0.5 * x) + 0.5


def _n_strided_groups():
    return sum(1 for _, r in ATTN_GROUPS if r > 1)


def _split3(x):
    hi = x.astype(BF16)
    r1 = x - hi.astype(F32)
    mid = r1.astype(BF16)
    lo = (r1 - mid.astype(F32)).astype(BF16)
    return hi, mid, lo


def _dot_f32(a, b):
    a0, a1, a2 = _split3(a)
    b0, b1, b2 = _split3(b)
    small = _bdot(a0, b2) + _bdot(a1, b1) + _bdot(a2, b0)
    mid = _bdot(a0, b1) + _bdot(a1, b0)
    return (small + mid) + _bdot(a0, b0)


def _mod_kernel(c_ref, w_ref, b_ref, o_ref):
    c = c_ref[...]
    s = c * jax.nn.sigmoid(c)
    o_ref[0] = _dot_f32(s, w_ref[0]) + b_ref[0]


def _modulation(c, w_ada, b_ada):
    depth, d, six_d = w_ada.shape
    b = c.shape[0]
    n_col = six_d // d
    return pl.pallas_call(
        _mod_kernel,
        grid=(depth, n_col),
        in_specs=[pl.BlockSpec((b, d), lambda l, j: (0, 0)),
                  pl.BlockSpec((1, d, d), lambda l, j: (l, 0, j)),
                  pl.BlockSpec((1, 1, d), lambda l, j: (l, 0, j))],
        out_specs=pl.BlockSpec((1, b, d), lambda l, j: (l, 0, j)),
        out_shape=jax.ShapeDtypeStruct((depth, b, six_d), F32),
        compiler_params=_ARB(2),
        name="adaln_mod",
    )(c, w_ada, b_ada.reshape(depth, 1, six_d))


def _rope_kernel(pos_ref, freq_ref, sign_ref, cos_ref, sin_ref):
    ang = pos_ref[...].astype(F32) * freq_ref[...]
    cos_ref[...] = jnp.cos(ang)
    sin_ref[...] = jnp.sin(ang) * sign_ref[...]


def _rope_tables(positions):
    n = positions.size
    half = HEAD_DIM // 2
    inv_freq = ROPE_THETA ** (-jnp.arange(half, dtype=F32) / half)
    freq = jnp.tile(inv_freq, 4).reshape(1, 128)
    sign = jnp.tile(jnp.concatenate([-jnp.ones((half,), F32), jnp.ones((half,), F32)]), 2)
    sign = sign.reshape(1, 128)
    tile = 1024
    return pl.pallas_call(
        _rope_kernel,
        grid=(n // tile,),
        in_specs=[pl.BlockSpec((tile, 1), lambda i: (i, 0)),
                  pl.BlockSpec((1, 128), lambda i: (0, 0)),
                  pl.BlockSpec((1, 128), lambda i: (0, 0))],
        out_specs=[pl.BlockSpec((tile, 128), lambda i: (i, 0))] * 2,
        out_shape=[jax.ShapeDtypeStruct((n, 128), F32)] * 2,
        compiler_params=_ARB(1),
        name="rope_tables",
    )(positions.reshape(n, 1), freq, sign)


def _modulated_norm(x, g, sc, sh):
    ms = jnp.mean(x * x, axis=-1, keepdims=True)
    y = x * lax.rsqrt(ms + RMS_EPS) * g
    return y * (1.0 + sc) + sh


def _inproj_kernel(h_ref, sc_ref, sh_ref, g_ref, w_ref, cos_ref, sin_ref,
                   ws_ref, bs_ref, lng_ref, lnb_ref, qkv0_ref, qkv1_ref, qkv2_ref, gm_ref,
                   slab_ref):
    tm = h_ref.shape[0]
    a = _modulated_norm(h_ref[...], g_ref[...], sc_ref[0], sh_ref[0]).astype(BF16)
    cos = cos_ref[...]
    sin = sin_ref[...]
    lane = lax.broadcasted_iota(I32, (tm, 128), 1)
    first_half = (lane % HEAD_DIM) < (HEAD_DIM // 2)

    def rope(x):
        swapped = jnp.where(first_half, pltpu.roll(x, 96, 1), pltpu.roll(x, 32, 1))
        return x * cos + swapped * sin

    slab = 0
    for g, out_ref in enumerate((qkv0_ref, qkv1_ref, qkv2_ref)):
        r = ATTN_GROUPS[g][1]
        for part in range(3):
            c0 = part * ATTN_W + g * GROUP_W
            y = _bdot(a, w_ref[:, c0:c0 + GROUP_W])
            halves = [y[:, :128], y[:, 128:]]
            if part < 2:
                halves = [rope(x) for x in halves]
            if part == 0:
                halves = [x * (HEAD_DIM ** -0.5) for x in halves]
            for c, x in enumerate(halves):
                o0 = part * GROUP_W + c * 128
                if r == 1:
                    out_ref[0, 0, :, o0:o0 + 128] = x.astype(BF16)
                else:
                    slab_ref[slab] = x
                    for j in range(r):
                        rows = slab_ref[slab, pl.ds(j, tm // r, stride=r), :]
                        out_ref[0, j, :, o0:o0 + 128] = rows.astype(BF16)
                    slab += 1

    c0 = 3 * ATTN_W
    u = jax.nn.gelu(_bdot(a, w_ref[:, c0:c0 + GMLP_W]))
    v = jax.nn.gelu(_bdot(a, w_ref[:, c0 + GMLP_W:c0 + 2 * GMLP_W]))
    mu = jnp.mean(v, axis=-1, keepdims=True)
    var = jnp.mean(jnp.square(v - mu), axis=-1, keepdims=True)
    v = ((v - mu) * lax.rsqrt(var + LN_EPS) * lng_ref[...] + lnb_ref[...]).astype(BF16)
    row = lax.broadcasted_iota(I32, (GMLP_CHUNK, GMLP_CHUNK), 0)
    col = lax.broadcasted_iota(I32, (GMLP_CHUNK, GMLP_CHUNK), 1)
    tril = col <= row
    for g in range(GMLP_GROUPS):
        wsg = jnp.where(tril, ws_ref[g], 0.0).astype(BF16)
        bias = bs_ref[:, g:g + 1]
        cs = slice(g * 128, (g + 1) * 128)
        for c in range(tm // GMLP_CHUNK):
            rs = slice(c * GMLP_CHUNK, (c + 1) * GMLP_CHUNK)
            mixed = _bdot(wsg, v[rs, cs]) + bias
            gm_ref[rs, cs] = (u[rs, cs] * mixed).astype(BF16)


def _inproj(h, sc, sh, g, w, cos, sin, w_s, b_s_t, ln_g, ln_b, seq):
    n, d = h.shape
    tm = TOKEN_TILE
    per_b = seq // tm
    batch = n // seq
    wn = w.shape[1]
    row = lambda i: (i, 0)
    bat = lambda i: (i // per_b, 0, 0)
    full2 = lambda i: (0, 0)
    res = lambda i: (i // per_b, 0, i % per_b, 0)
    qkv_specs = [pl.BlockSpec((1, r, tm // r, 3 * GROUP_W), res) for _, r in ATTN_GROUPS]
    qkv_shapes = [jax.ShapeDtypeStruct((batch, r, seq // r, 3 * GROUP_W), BF16)
                  for _, r in ATTN_GROUPS]
    return pl.pallas_call(
        _inproj_kernel,
        grid=(n // tm,),
        in_specs=[pl.BlockSpec((tm, d), row),
                  pl.BlockSpec((1, 1, d), bat),
                  pl.BlockSpec((1, 1, d), bat),
                  pl.BlockSpec((1, d), full2),
                  pl.BlockSpec((d, wn), full2),
                  pl.BlockSpec((tm, 128), row),
                  pl.BlockSpec((tm, 128), row),
                  pl.BlockSpec((GMLP_GROUPS, GMLP_CHUNK, GMLP_CHUNK), lambda i: (0, 0, 0)),
                  pl.BlockSpec((GMLP_CHUNK, GMLP_GROUPS), full2),
                  pl.BlockSpec((1, GMLP_W), full2),
                  pl.BlockSpec((1, GMLP_W), full2)],
        out_specs=qkv_specs + [pl.BlockSpec((tm, GMLP_W), row)],
        out_shape=qkv_shapes + [jax.ShapeDtypeStruct((n, GMLP_W), BF16)],
        scratch_shapes=[pltpu.VMEM((_n_strided_groups() * 3 * (GROUP_W // 128), tm, 128), F32)],
        compiler_params=_ARB(1),
        name="inproj_rope_gmlp",
    )(h, sc, sh, g, w, cos, sin, w_s, b_s_t, ln_g, ln_b)


def _attn_kernel(q_ref, k_ref, v_ref, kp_ref, vp_ref, o_ref, lse_ref):
    q_ref, k_ref, v_ref, kp_ref, vp_ref, o_ref, lse_ref = (
        x.at[0] for x in (q_ref, k_ref, v_ref, kp_ref, vp_ref, o_ref, lse_ref))
    qb = q_ref.shape[1]
    first_key = jnp.where(pl.program_id(2) == 0, WIN_BLOCK, 0)
    qi = lax.broadcasted_iota(I32, (WIN_BLOCK, 2 * WIN_BLOCK), 0)
    kj = lax.broadcasted_iota(I32, (WIN_BLOCK, 2 * WIN_BLOCK), 1)
    band = (kj >= qi) & (kj <= qi + WIN_BLOCK)
    low_lanes = lax.broadcasted_iota(I32, (WIN_BLOCK, 128), 1) < HEAD_DIM
    ones = jnp.ones((2 * WIN_BLOCK, 128), BF16)
    for i in range(qb // WIN_BLOCK):
        rs = slice(i * WIN_BLOCK, (i + 1) * WIN_BLOCK)
        q = q_ref[0, rs, :]
        if i == 0:
            k_prev, v_prev = kp_ref[0], vp_ref[0]
            valid = band & (kj >= first_key)
        else:
            ps = slice((i - 1) * WIN_BLOCK, i * WIN_BLOCK)
            k_prev, v_prev = k_ref[0, ps, :], v_ref[0, ps, :]
            valid = band
        k_cat = jnp.concatenate([k_prev, k_ref[0, rs, :]], axis=0)
        v_cat = jnp.concatenate([v_prev, v_ref[0, rs, :]], axis=0)
        outs, lses = [], []
        for pair in range(GROUP_W // 128):
            cs = slice(pair * 128, (pair + 1) * 128)
            q_pair = q[:, cs].astype(F32)
            k_pair = k_cat[:, cs]
            v_ones = jnp.concatenate([v_cat[:, cs], ones], axis=1)
            o_half, l_half = [], []
            for head_lanes in (low_lanes, jnp.logical_not(low_lanes)):
                q_head = jnp.where(head_lanes, q_pair, 0.0).astype(BF16)
                s = lax.dot_general(q_head, k_pair, (((1,), (1,)), ((), ())),
                                    preferred_element_type=F32)
                s = jnp.where(valid, s, NEG_INF)
                m = jnp.max(s, axis=-1, keepdims=True)
                p = jnp.exp(s - m).astype(BF16)
                pv = _bdot(p, v_ones)
                den = pv[:, 128:129]
                o_half.append(pv[:, :128] / den)
                l_half.append(jnp.broadcast_to(m + jnp.log(den), (WIN_BLOCK, 128)))
            outs.append(jnp.where(low_lanes, o_half[0], o_half[1]))
            lses.append(jnp.where(low_lanes, l_half[0], l_half[1]))
        o_ref[0, rs, :] = jnp.concatenate(outs, axis=1).astype(o_ref.dtype)
        lse_ref[0, rs, :] = jnp.concatenate(lses, axis=1)


def _attention_group(qkv, g):
    batch, r, sub_len, _ = qkv.shape
    qb = min(512, sub_len)
    n_tiles = sub_len // qb
    per_tile = qb // WIN_BLOCK

    def cur(part):
        return pl.BlockSpec((1, 1, qb, GROUP_W), lambda b, j, n: (b, j, n, part))

    def prev(part):
        return pl.BlockSpec((1, 1, WIN_BLOCK, GROUP_W),
                            lambda b, j, n: (b, j, jnp.maximum(n * per_tile - 1, 0), part))

    out_spec = pl.BlockSpec((1, 1, qb, GROUP_W), lambda b, j, n: (b, j, n, 0))
    return pl.pallas_call(
        _attn_kernel,
        grid=(batch, r, n_tiles),
        in_specs=[cur(0), cur(1), cur(2), prev(1), prev(2)],
        out_specs=[out_spec, out_spec],
        out_shape=[jax.ShapeDtypeStruct((batch, r, sub_len, GROUP_W), BF16),
                   jax.ShapeDtypeStruct((batch, r, sub_len, GROUP_W), F32)],
        compiler_params=_ARB(3),
        name=f"dilated_attn_g{g}",
    )(qkv, qkv, qkv, qkv, qkv)


def _split_bf16(x):
    hi = x.astype(BF16)
    lo = (x - hi.astype(F32)).astype(BF16)
    return hi, lo


def _merge_kernel(h_ref, sc1_ref, sh1_ref, g1_ref, sc2_ref, sh2_ref, n1_ref, n2_ref,
                  wg_ref, wpa_ref, wpg_ref, wo_ref,
                  o0_ref, o1_ref, o2_ref, l0_ref, l1_ref, l2_ref, gm_ref,
                  wr_ref, br_ref,
                  hn_ref, f_ref, slot_ref, gate_ref, cnt_ref, slab_ref):
    tm, d = h_ref.shape
    slabs = iter(range(slab_ref.shape[0]))

    def token_major(ref):
        r = ref.shape[1]
        if r == 1:
            return ref[0, 0].astype(F32)
        halves = []
        for c in range(GROUP_W // 128):
            slab = next(slabs)
            for j in range(r):
                slab_ref[slab, pl.ds(j, tm // r, stride=r), :] = (
                    ref[0, j, :, c * 128:(c + 1) * 128].astype(F32))
            halves.append(slab_ref[slab])
        return jnp.concatenate(halves, axis=1)

    lses = [token_major(x) for x in (l0_ref, l1_ref, l2_ref)]
    outs = [token_major(x) for x in (o0_ref, o1_ref, o2_ref)]
    w_hi, w_lo = _split_bf16(wr_ref[...])
    nt = (((1,), (1,)), ((), ()))

    logit_chunks = []
    for c0 in range(0, tm, MERGE_ROWS):
        rs = slice(c0, c0 + MERGE_ROWS)
        h = h_ref[rs, :]
        a = _modulated_norm(h, n1_ref[...], sc1_ref[0], sh1_ref[0]).astype(BF16)
        gates = _sigmoid(_bdot(a, wg_ref[...]))

        ls = [x[rs, :] for x in lses]
        m = jnp.maximum(jnp.maximum(ls[0], ls[1]), ls[2])
        es = [jnp.exp(l - m) for l in ls]
        den = es[0] + es[1] + es[2]
        attn = (es[0] * outs[0][rs, :] + es[1] * outs[1][rs, :] + es[2] * outs[2][rs, :]) / den

        merged = (gates[:, :d] * _bdot(attn.astype(BF16), wpa_ref[...])
                  + gates[:, d:] * _bdot(gm_ref[rs, :], wpg_ref[...]))
        hn = h + g1_ref[0] * _bdot(merged.astype(BF16), wo_ref[...])
        hn_ref[rs, :] = hn

        f = _modulated_norm(hn, n2_ref[...], sc2_ref[0], sh2_ref[0])
        f_ref[rs, :] = f.astype(BF16)

        f_hi, f_lo = _split_bf16(f)
        logit_chunks.append(lax.dot_general(w_hi, f_hi, nt, preferred_element_type=F32)
                            + lax.dot_general(w_hi, f_lo, nt, preferred_element_type=F32)
                            + lax.dot_general(w_lo, f_hi, nt, preferred_element_type=F32))
    logits = jnp.concatenate(logit_chunks, axis=1) + br_ref[...]

    n_e = logits.shape[0]
    erow = lax.broadcasted_iota(I32, (n_e, tm), 0).astype(F32)
    vals, idxs, sels = [], [], []
    cur = logits
    for _ in range(TOP_K):
        mx = jnp.max(cur, axis=0, keepdims=True)
        ix = jnp.min(jnp.where(cur == mx, erow, float(n_e)), axis=0, keepdims=True)
        sel = erow == ix
        vals.append(mx)
        idxs.append(ix)
        sels.append(sel)
        cur = jnp.where(sel, -jnp.inf, cur)
    exps = [jnp.exp(v - vals[0]) for v in vals]
    tot = exps[0] + exps[1] + exps[2] + exps[3]
    gate_ref[...] = jnp.concatenate([e / tot for e in exps], axis=0)

    onehot = (sels[0] | sels[1] | sels[2] | sels[3])
    oh = jnp.where(onehot, 1.0, 0.0)
    t_r = lax.broadcasted_iota(I32, (tm, tm), 0)
    t_c = lax.broadcasted_iota(I32, (tm, tm), 1)
    upper = jnp.where(t_r < t_c, 1.0, 0.0).astype(BF16)
    prefix = _bdot(oh.astype(BF16), upper)
    cnt = jnp.sum(oh, axis=1, keepdims=True)
    cnt_al = jnp.floor((cnt + (ROW_ALIGN - 1)) * (1.0 / ROW_ALIGN)) * ROW_ALIGN
    e_r = lax.broadcasted_iota(I32, (n_e, n_e), 0)
    e_c = lax.broadcasted_iota(I32, (n_e, n_e), 1)
    lower = jnp.where(e_c < e_r, 1.0, 0.0).astype(BF16)
    start = _bdot(lower, jnp.broadcast_to(cnt_al, (n_e, 128)).astype(BF16))[:, 0:1]
    local = prefix + start
    slots = [jnp.sum(jnp.where(s, local, 0.0), axis=0, keepdims=True) for s in sels]
    slot_ref[...] = jnp.concatenate(slots, axis=0).astype(I32)
    cnt_ref[...] = jnp.broadcast_to(cnt, cnt_ref.shape).astype(I32)


def _merge_route(h, mods, n1, n2, wg, wpa, wpg, wo, outs, lses, gm, wr, br, seq):
    n, d = h.shape
    tm = TOKEN_TILE
    per_b = seq // tm
    row = lambda i: (i, 0)
    bat = lambda i: (i // per_b, 0, 0)
    full2 = lambda i: (0, 0)
    col = lambda i: (0, i)
    sc1, sh1, g1, sc2, sh2 = mods
    res = lambda i: (i // per_b, 0, i % per_b, 0)
    attn_specs = [pl.BlockSpec((1, r, tm // r, GROUP_W), res) for _, r in ATTN_GROUPS]
    in_specs = ([pl.BlockSpec((tm, d), row)]
                + [pl.BlockSpec((1, 1, d), bat)] * 5
                + [pl.BlockSpec((1, d), full2)] * 2
                + [pl.BlockSpec(wg.shape, full2), pl.BlockSpec(wpa.shape, full2),
                   pl.BlockSpec(wpg.shape, full2), pl.BlockSpec(wo.shape, full2)]
                + attn_specs * 2
                + [pl.BlockSpec((tm, GMLP_W), row)]
                + [pl.BlockSpec(wr.shape, full2), pl.BlockSpec(br.shape, full2)])
    out_specs = [pl.BlockSpec((tm, d), row), pl.BlockSpec((tm, d), row),
                 pl.BlockSpec((TOP_K, tm), col), pl.BlockSpec((TOP_K, tm), col),
                 pl.BlockSpec((N_EXPERTS, 128), row)]
    out_shape = [jax.ShapeDtypeStruct((n, d), F32), jax.ShapeDtypeStruct((n, d), BF16),
                 jax.ShapeDtypeStruct((TOP_K, n), I32), jax.ShapeDtypeStruct((TOP_K, n), F32),
                 jax.ShapeDtypeStruct((n // tm * N_EXPERTS, 128), I32)]
    return pl.pallas_call(
        _merge_kernel,
        grid=(n // tm,),
        in_specs=in_specs,
        out_specs=out_specs,
        out_shape=out_shape,
        scratch_shapes=[pltpu.VMEM((_n_strided_groups() * 2 * (GROUP_W // 128), tm, 128), F32)],
        compiler_params=_ARB(1),
        name="merge_proj_route",
    )(h, sc1, sh1, g1, sc2, sh2, n1, n2, wg, wpa, wpg, wo, *outs, *lses, gm, wr, br)


def _row_copy(src, dst, sem):
    return pltpu.make_async_copy(src, dst, sem)


def _start_runs(tile, off_ref, start_ref, pieces_ref, copy):
    def per_expert(e, total):
        base = tile * N_EXPERTS + e
        local0, global0, n = start_ref[base], off_ref[base], pieces_ref[base]
        n_big = lax.shift_right_logical(n, _BIG_SHIFT)

        def big(i, carry):
            row = pl.multiple_of(i * BIG_PIECE, ROW_ALIGN)
            copy(pl.multiple_of(local0 + row, ROW_ALIGN),
                 pl.multiple_of(global0 + row, ROW_ALIGN), BIG_PIECE).start()
            return carry

        def small(i, carry):
            row = pl.multiple_of(n_big * BIG_PIECE + i * ROW_ALIGN, ROW_ALIGN)
            copy(pl.multiple_of(local0 + row, ROW_ALIGN),
                 pl.multiple_of(global0 + row, ROW_ALIGN), ROW_ALIGN).start()
            return carry

        lax.fori_loop(0, n_big, big, 0)
        lax.fori_loop(0, n - lax.shift_left(n_big, _BIG_SHIFT), small, 0)
        return total + n

    return lax.fori_loop(0, N_EXPERTS, per_expert, 0)


def _wait_runs(total, copy):
    n_big = lax.shift_right_logical(total, _BIG_SHIFT)
    lax.fori_loop(0, n_big, lambda i, c: (copy(0, 0, BIG_PIECE).wait(), c)[1], 0)
    lax.fori_loop(0, total - lax.shift_left(n_big, _BIG_SHIFT),
                  lambda i, c: (copy(0, 0, ROW_ALIGN).wait(), c)[1], 0)


def _slot_hits(slot_ref, chunk):
    tm = slot_ref.shape[1]
    s = lax.broadcasted_iota(I32, (SORT_CHUNK, tm), 0) + chunk * SORT_CHUNK
    return [slot_ref[k:k + 1, :] == s for k in range(TOP_K)]


def _dispatch_kernel(zb_ref, nz_ref, off_ref, start_ref, pieces_ref, slot_ref, f_ref, buf_ref,
                     sorted_ref, zero_ref, pending_ref, sems, zsem):
    step = pl.program_id(0)
    buf_slot = lax.rem(step, 2)
    sorted_now = sorted_ref.at[buf_slot]

    @pl.when(step == 0)
    def _():
        zero_ref[...] = jnp.zeros_like(zero_ref)

        def zero_copy(j):
            start = pl.multiple_of(zb_ref[j] * EXPERT_BLOCK, EXPERT_BLOCK)
            return _row_copy(zero_ref, buf_ref.at[pl.ds(start, EXPERT_BLOCK)], zsem)

        lax.fori_loop(0, nz_ref[0], lambda j, c: (zero_copy(j).start(), c)[1], 0)
        lax.fori_loop(0, nz_ref[0], lambda j, c: (zero_copy(j).wait(), c)[1], 0)

    f = f_ref[...]
    for c in range(sorted_now.shape[0] // SORT_CHUNK):
        hits = _slot_hits(slot_ref, c)
        onehot = jnp.where(hits[0] | hits[1] | hits[2] | hits[3], 1.0, 0.0).astype(BF16)
        sorted_now[c * SORT_CHUNK:(c + 1) * SORT_CHUNK, :] = _bdot(onehot, f)

    def copy_from(slot):
        def copy(local_row, global_row, rows):
            return _row_copy(sorted_ref.at[slot, pl.ds(local_row, rows)],
                             buf_ref.at[pl.ds(global_row, rows)], sems.at[slot])
        return copy

    total = _start_runs(step, off_ref, start_ref, pieces_ref, copy_from(buf_slot))

    @pl.when(step > 0)
    def _():
        _wait_runs(pending_ref[0], copy_from(1 - buf_slot))

    pending_ref[0] = total

    @pl.when(step == pl.num_programs(0) - 1)
    def _():
        _wait_runs(total, copy_from(buf_slot))


def _dispatch(plan, slots, f, n_blocks):
    n, d = f.shape
    tm = TOKEN_TILE
    return pl.pallas_call(
        _dispatch_kernel,
        grid_spec=pltpu.PrefetchScalarGridSpec(
            num_scalar_prefetch=5, grid=(n // tm,),
            in_specs=[pl.BlockSpec((TOP_K, tm), lambda i, *_: (0, i)),
                      pl.BlockSpec((tm, d), lambda i, *_: (i, 0))],
            out_specs=pl.BlockSpec(memory_space=pl.ANY),
            scratch_shapes=[pltpu.VMEM((2, _local_rows(tm), d), F32),
                            pltpu.VMEM((EXPERT_BLOCK, d), F32),
                            pltpu.SMEM((1,), I32),
                            pltpu.SemaphoreType.DMA((2,)), pltpu.SemaphoreType.DMA(())]),
        out_shape=jax.ShapeDtypeStruct((n_blocks * EXPERT_BLOCK, d), F32),
        compiler_params=_ARB(1),
        name="dispatch_rows",
    )(plan["zero_blocks"], plan["n_zero"], plan["run_off"], plan["run_start"],
      plan["run_pieces"], slots, f)


def _expert_kernel(be_ref, first_ref, nv_ref, x_ref, w1_ref, b1_ref, w2_ref, b2_ref, o_ref,
                   w1b_ref, w2p_ref, slab_ref):
    del be_ref
    step = pl.program_id(0)
    live = step < nv_ref[0]
    ff, d = w2p_ref.shape
    half = 64

    @pl.when(live & (first_ref[step] == 1))
    def _():
        w1b_ref[...] = w1_ref[0, 0].astype(BF16)
        for c in range(d // 128):
            cs = slice(c * 128, (c + 1) * 128)
            for k in range(ff // 128):
                r0 = k * 128
                slab_ref[pl.ds(r0, half, stride=2), :] = w2_ref[0, 0, r0:r0 + half, cs]
                slab_ref[pl.ds(r0 + 1, half, stride=2), :] = w2_ref[0, 0, r0 + half:r0 + 128, cs]
            w2p_ref[:, cs] = slab_ref[...].astype(BF16)

    @pl.when(live)
    def _():
        x = x_ref[...].astype(BF16)
        hb = _bdot(x, w1b_ref[...]) + b1_ref[0, 0]
        lane = lax.broadcasted_iota(I32, (x.shape[0], 128), 1)
        even = (lane & 1) == 0
        glu, lin = [], []
        for k in range(ff // 128):
            a = hb[:, 256 * k:256 * k + 128]
            b = hb[:, 256 * k + 128:256 * k + 256]
            glu.append(jnp.where(even, a, pltpu.roll(b, 1, 1)))
            lin.append(jnp.where(even, pltpu.roll(a, 127, 1), b))
        x_glu = jnp.minimum(jnp.concatenate(glu, axis=1), SWIGLU_LIMIT)
        x_lin = jnp.clip(jnp.concatenate(lin, axis=1), -SWIGLU_LIMIT, SWIGLU_LIMIT)
        act = x_glu * _sigmoid(SWIGLU_ALPHA * x_glu) * (x_lin + 1.0)
        o_ref[...] = _bdot(act.astype(BF16), w2p_ref[...]) + b2_ref[0, 0]

    @pl.when(jnp.logical_not(live))
    def _():
        o_ref[...] = jnp.zeros_like(o_ref)


def _experts(layer, block_e, first, n_valid, buf, w1, b1, w2, b2, n_blocks):
    d = buf.shape[1]
    ff = w2.shape[2]
    xmap = lambda i, be, fi, nv: (jnp.where(i < nv[0], i, 0), 0)
    wmap = lambda i, be, fi, nv: (layer, be[i], 0, 0)
    return pl.pallas_call(
        _expert_kernel,
        grid_spec=pltpu.PrefetchScalarGridSpec(
            num_scalar_prefetch=3, grid=(n_blocks,),
            in_specs=[pl.BlockSpec((EXPERT_BLOCK, d), xmap),
                      pl.BlockSpec((1, 1, d, 2 * ff), wmap),
                      pl.BlockSpec((1, 1, 1, 2 * ff), wmap),
                      pl.BlockSpec((1, 1, ff, d), wmap),
                      pl.BlockSpec((1, 1, 1, d), wmap)],
            out_specs=pl.BlockSpec((EXPERT_BLOCK, d), lambda i, be, fi, nv: (i, 0)),
            scratch_shapes=[pltpu.VMEM((d, 2 * ff), BF16), pltpu.VMEM((ff, d), BF16),
                            pltpu.VMEM((ff, 128), F32)]),
        out_shape=jax.ShapeDtypeStruct((n_blocks * EXPERT_BLOCK, d), F32),
        compiler_params=_ARB(1),
        name="expert_ffn",
    )(block_e, first, n_valid, buf, w1, b1, w2, b2)


def _combine_kernel(off_ref, start_ref, pieces_ref, slot_ref, gate_ref, h_ref, g2_ref, fin_ref,
                    obuf_ref, out_ref, rows_ref, sems, *, final_norm):
    step = pl.program_id(0)
    buf_slot = lax.rem(step, 2)
    rows_now = rows_ref.at[buf_slot]

    def copy_into(slot):
        def copy(local_row, global_row, rows):
            return _row_copy(obuf_ref.at[pl.ds(global_row, rows)],
                             rows_ref.at[slot, pl.ds(local_row, rows)], sems.at[slot])
        return copy

    @pl.when(step == 0)
    def _():
        rows_ref[...] = jnp.zeros_like(rows_ref)
        _start_runs(0, off_ref, start_ref, pieces_ref, copy_into(0))

    @pl.when(step + 1 < pl.num_programs(0))
    def _():
        _start_runs(step + 1, off_ref, start_ref, pieces_ref, copy_into(1 - buf_slot))

    total = lax.fori_loop(0, N_EXPERTS,
                          lambda e, t: t + pieces_ref[step * N_EXPERTS + e], 0)
    _wait_runs(total, copy_into(buf_slot))

    y = jnp.zeros(h_ref.shape, F32)
    for c in range(rows_now.shape[0] // SORT_CHUNK):
        hits = _slot_hits(slot_ref, c)
        weight = jnp.where(hits[0], gate_ref[0:1, :], 0.0)
        for k in range(1, TOP_K):
            weight = weight + jnp.where(hits[k], gate_ref[k:k + 1, :], 0.0)
        rows = rows_now[c * SORT_CHUNK:(c + 1) * SORT_CHUNK, :].astype(BF16)
        y = y + lax.dot_general(weight.astype(BF16), rows, (((0,), (0,)), ((), ())),
                                preferred_element_type=F32)
    hn = h_ref[...] + g2_ref[0] * y
    if final_norm:
        ms = jnp.mean(hn * hn, axis=-1, keepdims=True)
        hn = hn * lax.rsqrt(ms + RMS_EPS) * fin_ref[...]
    out_ref[...] = hn


def _combine(plan, slots, gate, h, g2, final_g, obuf, seq, final_norm):
    n, d = h.shape
    tm = TOKEN_TILE
    per_b = seq // tm
    return pl.pallas_call(
        functools.partial(_combine_kernel, final_norm=final_norm),
        grid_spec=pltpu.PrefetchScalarGridSpec(
            num_scalar_prefetch=3, grid=(n // tm,),
            in_specs=[pl.BlockSpec((TOP_K, tm), lambda i, *_: (0, i)),
                      pl.BlockSpec((TOP_K, tm), lambda i, *_: (0, i)),
                      pl.BlockSpec((tm, d), lambda i, *_: (i, 0)),
                      pl.BlockSpec((1, 1, d), lambda i, *_: (i // per_b, 0, 0)),
                      pl.BlockSpec((1, d), lambda i, *_: (0, 0)),
                      pl.BlockSpec(memory_space=pl.ANY)],
            out_specs=pl.BlockSpec((tm, d), lambda i, *_: (i, 0)),
            scratch_shapes=[pltpu.VMEM((2, _local_rows(tm), d), F32),
                            pltpu.SemaphoreType.DMA((2,))]),
        out_shape=jax.ShapeDtypeStruct((n, d), F32),
        compiler_params=_ARB(1),
        name="combine_rows",
    )(plan["run_off"], plan["run_start"], plan["run_pieces"], slots, gate, h, g2, final_g, obuf)


def _local_rows(tm):
    rows = tm * TOP_K + N_EXPERTS * (ROW_ALIGN - 1)
    return -(-rows // SORT_CHUNK) * SORT_CHUNK


def _num_blocks(n):
    rows = n * TOP_K + (n // TOKEN_TILE) * N_EXPERTS * (ROW_ALIGN - 1)
    return -(-rows // EXPERT_BLOCK) + N_EXPERTS


def _slot_plan(tile_counts, n):
    blk = EXPERT_BLOCK
    n_blocks = _num_blocks(n)
    cnt = (tile_counts + ROW_ALIGN - 1) // ROW_ALIGN * ROW_ALIGN
    total = jnp.sum(cnt, axis=0)
    padded = (total + blk - 1) // blk * blk
    pad_end = jnp.cumsum(padded)
    pad_start = pad_end - padded
    run_off = pad_start[None, :] + jnp.cumsum(cnt, axis=0) - cnt
    run_start = jnp.cumsum(cnt, axis=1) - cnt
    n_valid = pad_end[-1] // blk
    block_start = jnp.arange(n_blocks, dtype=I32) * blk
    block_e = jnp.sum(pad_end[None, :] <= block_start[:, None], axis=1)
    last_e = jnp.sum(pad_end <= (n_valid - 1) * blk)
    block_e = jnp.where(jnp.arange(n_blocks) < n_valid, block_e, last_e).astype(I32)
    first = jnp.concatenate([jnp.ones((1,), I32),
                             (block_e[1:] != block_e[:-1]).astype(I32)])
    tails = jnp.where(total > 0, pad_end // blk - 1, -1)
    spare = jnp.arange((n * TOP_K) // blk, n_blocks)
    cand = jnp.concatenate([tails, jnp.where(spare >= n_valid, spare, -1)]).astype(I32)
    zero_blocks = cand[jnp.argsort(cand < 0, stable=True)]
    flat = lambda a: a.reshape(-1).astype(I32)
    return dict(block_e=block_e, first=first, n_valid=n_valid.reshape(1).astype(I32),
                zero_blocks=jnp.maximum(zero_blocks, 0),
                n_zero=jnp.sum(cand >= 0).reshape(1).astype(I32),
                run_off=flat(run_off), run_start=flat(run_start),
                run_pieces=flat(cnt // ROW_ALIGN))


def kernel(x, c, positions, w_ada, b_ada, norm1_g, w_in, w_s, b_s, ln_g, ln_b, w_pa, w_pg,
           w_o, norm2_g, w_router, b_router, w1, b1, w2, b2, final_g):
    batch, seq, d = x.shape
    depth = w_ada.shape[0]
    n = batch * seq
    n_blocks = _num_blocks(n)

    mod = _modulation(c, w_ada, b_ada)
    cos, sin = _rope_tables(positions)

    n_mix = 3 * ATTN_W + 2 * GMLP_W
    b1r = b1.reshape(depth, N_EXPERTS, 1, -1)
    b2r = b2.reshape(depth, N_EXPERTS, 1, -1)

    h = x.reshape(n, d)
    for l in range(depth):
        ml = mod[l].reshape(batch, 6, 1, d)
        sh1, sc1, g1, sh2, sc2, g2 = [ml[:, i] for i in range(6)]
        w_mix = w_in[l, :, :n_mix].astype(BF16)
        w_gate = w_in[l, :, n_mix:].astype(BF16)

        *qkvs, gm = _inproj(h, sc1, sh1, norm1_g[l].reshape(1, d), w_mix, cos, sin,
                            w_s[l], b_s[l].T, ln_g[l].reshape(1, -1), ln_b[l].reshape(1, -1), seq)
        outs, lses = zip(*[_attention_group(qkvs[g], g) for g in range(N_GROUPS)])

        h, f, slots, gate, counts = _merge_route(
            h, (sc1, sh1, g1, sc2, sh2), norm1_g[l].reshape(1, d), norm2_g[l].reshape(1, d),
            w_gate, w_pa[l].astype(BF16), w_pg[l].astype(BF16), w_o[l].astype(BF16),
            outs, lses, gm, w_router[l].T, b_router[l].reshape(-1, 1), seq)

        plan = _slot_plan(counts[:, 0].reshape(-1, N_EXPERTS), n)
        buf = _dispatch(plan, slots, f, n_blocks)
        obuf = _experts(l, plan["block_e"], plan["first"], plan["n_valid"], buf,
                        w1, b1r, w2, b2r, n_blocks)

        h = _combine(plan, slots, gate, h, g2, final_g.reshape(1, d), obuf, seq,
                     final_norm=(l == depth - 1))
    return h.reshape(batch, seq, d)
```

```python
import functools

import jax
import jax.numpy as jnp
from jax import lax
from jax.experimental import pallas as pl
from jax.experimental.pallas import tpu as pltpu

F32 = jnp.float32
BF16 = jnp.bfloat16
I32 = jnp.int32

HEAD_DIM = 64
HEADS_PER_GROUP = 4
GROUP_W = HEADS_PER_GROUP * HEAD_DIM
ATTN_GROUPS = ((128, 1), (512, 4), (2048, 16))
N_GROUPS = len(ATTN_GROUPS)
ATTN_W = N_GROUPS * GROUP_W
WIN_BLOCK = 128
ROPE_THETA = 10000.0
GMLP_CHUNK = 128
GMLP_GROUPS = 4
GMLP_W = GMLP_GROUPS * 128
N_EXPERTS = 32
TOP_K = 4
SWIGLU_ALPHA = 1.702
SWIGLU_LIMIT = 7.0
RMS_EPS = 1e-5
LN_EPS = 1e-5
NEG_INF = -1e30

TOKEN_TILE = 512
ATTN_STEP_ROWS = 2048
EXPERT_BLOCK = 512
EXPERT_SUB = 256
ROW_ALIGN = 8
BIG_PIECE = 32
_BIG_SHIFT = (BIG_PIECE // ROW_ALIGN).bit_length() - 1
SORT_CHUNK = 256
VMEM_LIMIT = 56 * 1024 * 1024

_ARB = lambda n: pltpu.CompilerParams(dimension_semantics=("arbitrary",) * n,
                                      vmem_limit_bytes=VMEM_LIMIT)


def _bdot(a, b):
    return jnp.dot(a, b, preferred_element_type=F32)


def _sigmoid(x):
    return 0.5 * jnp.tanh(0.5 * x) + 0.5


def _n_strided_groups():
    return sum(1 for _, r in ATTN_GROUPS if r > 1)


def _split3(x):
    hi = x.astype(BF16)
    r1 = x - hi.astype(F32)
    mid = r1.astype(BF16)
    lo = (r1 - mid.astype(F32)).astype(BF16)
    return hi, mid, lo


def _dot_f32(a, b):
    a0, a1, a2 = _split3(a)
    b0, b1, b2 = _split3(b)
    small = _bdot(a0, b2) + _bdot(a1, b1) + _bdot(a2, b0)
    mid = _bdot(a0, b1) + _bdot(a1, b0)
    return (small + mid) + _bdot(a0, b0)


def _mod_kernel(c_ref, w_ref, b_ref, o_ref):
    c = c_ref[...]
    s = c * jax.nn.sigmoid(c)
    o_ref[0] = _dot_f32(s, w_ref[0]) + b_ref[0]


def _modulation(c, w_ada, b_ada):
    depth, d, six_d = w_ada.shape
    b = c.shape[0]
    n_col = six_d // d
    return pl.pallas_call(
        _mod_kernel,
        grid=(depth, n_col),
        in_specs=[pl.BlockSpec((b, d), lambda l, j: (0, 0)),
                  pl.BlockSpec((1, d, d), lambda l, j: (l, 0, j)),
                  pl.BlockSpec((1, 1, d), lambda l, j: (l, 0, j))],
        out_specs=pl.BlockSpec((1, b, d), lambda l, j: (l, 0, j)),
        out_shape=jax.ShapeDtypeStruct((depth, b, six_d), F32),
        compiler_params=_ARB(2),
        name="adaln_mod",
    )(c, w_ada, b_ada.reshape(depth, 1, six_d))


def _rope_kernel(pos_ref, freq_ref, sign_ref, cos_ref, sin_ref):
    ang = pos_ref[...].astype(F32) * freq_ref[...]
    cos_ref[...] = jnp.cos(ang)
    sin_ref[...] = jnp.sin(ang) * sign_ref[...]


def _rope_tables(positions):
    n = positions.size
    half = HEAD_DIM // 2
    inv_freq = ROPE_THETA ** (-jnp.arange(half, dtype=F32) / half)
    freq = jnp.tile(inv_freq, 4).reshape(1, 128)
    sign = jnp.tile(jnp.concatenate([-jnp.ones((half,), F32), jnp.ones((half,), F32)]), 2)
    sign = sign.reshape(1, 128)
    tile = 1024
    return pl.pallas_call(
        _rope_kernel,
        grid=(n // tile,),
        in_specs=[pl.BlockSpec((tile, 1), lambda i: (i, 0)),
                  pl.BlockSpec((1, 128), lambda i: (0, 0)),
                  pl.BlockSpec((1, 128), lambda i: (0, 0))],
        out_specs=[pl.BlockSpec((tile, 128), lambda i: (i, 0))] * 2,
        out_shape=[jax.ShapeDtypeStruct((n, 128), F32)] * 2,
        compiler_params=_ARB(1),
        name="rope_tables",
    )(positions.reshape(n, 1), freq, sign)


def _modulated_norm(x, g, sc, sh):
    ms = jnp.mean(x * x, axis=-1, keepdims=True)
    y = x * lax.rsqrt(ms + RMS_EPS) * g
    return y * (1.0 + sc) + sh


def _inproj_kernel(h_ref, sc_ref, sh_ref, g_ref, w_ref, cos_ref, sin_ref,
                   ws_ref, bs_ref, lng_ref, lnb_ref, qkv0_ref, qkv1_ref, qkv2_ref, gm_ref,
                   slab_ref):
    tm = h_ref.shape[0]
    a = _modulated_norm(h_ref[...], g_ref[...], sc_ref[0], sh_ref[0]).astype(BF16)
    cos = cos_ref[...]
    sin = sin_ref[...]
    lane = lax.broadcasted_iota(I32, (tm, 128), 1)
    first_half = (lane % HEAD_DIM) < (HEAD_DIM // 2)

    def rope(x):
        swapped = jnp.where(first_half, pltpu.roll(x, 96, 1), pltpu.roll(x, 32, 1))
        return x * cos + swapped * sin

    slab = 0
    for g, out_ref in enumerate((qkv0_ref, qkv1_ref, qkv2_ref)):
        r = ATTN_GROUPS[g][1]
        for part in range(3):
            c0 = part * ATTN_W + g * GROUP_W
            y = _bdot(a, w_ref[:, c0:c0 + GROUP_W])
            halves = [y[:, :128], y[:, 128:]]
            if part < 2:
                halves = [rope(x) for x in halves]
            if part == 0:
                halves = [x * (HEAD_DIM ** -0.5) for x in halves]
            for c, x in enumerate(halves):
                o0 = part * GROUP_W + c * 128
                if r == 1:
                    out_ref[0, 0, :, o0:o0 + 128] = x.astype(BF16)
                else:
                    slab_ref[slab] = x
                    for j in range(r):
                        rows = slab_ref[slab, pl.ds(j, tm // r, stride=r), :]
                        out_ref[0, j, :, o0:o0 + 128] = rows.astype(BF16)
                    slab += 1

    c0 = 3 * ATTN_W
    u = jax.nn.gelu(_bdot(a, w_ref[:, c0:c0 + GMLP_W]))
    v = jax.nn.gelu(_bdot(a, w_ref[:, c0 + GMLP_W:c0 + 2 * GMLP_W]))
    mu = jnp.mean(v, axis=-1, keepdims=True)
    var = jnp.mean(jnp.square(v - mu), axis=-1, keepdims=True)
    v = ((v - mu) * lax.rsqrt(var + LN_EPS) * lng_ref[...] + lnb_ref[...]).astype(BF16)
    row = lax.broadcasted_iota(I32, (GMLP_CHUNK, GMLP_CHUNK), 0)
    col = lax.broadcasted_iota(I32, (GMLP_CHUNK, GMLP_CHUNK), 1)
    tril = col <= row
    chunks = [slice(c * GMLP_CHUNK, (c + 1) * GMLP_CHUNK) for c in range(tm // GMLP_CHUNK)]
    for g in range(GMLP_GROUPS):
        wsg = jnp.where(tril, ws_ref[g], 0.0).astype(BF16)
        cs = slice(g * 128, (g + 1) * 128)
        mixed = _bdot(wsg, jnp.concatenate([v[rs, cs] for rs in chunks], axis=1))
        mixed = mixed + bs_ref[:, g:g + 1]
        for c, rs in enumerate(chunks):
            gm_ref[rs, cs] = (u[rs, cs] * mixed[:, c * 128:(c + 1) * 128]).astype(BF16)


def _inproj(h, sc, sh, g, w, cos, sin, w_s, b_s_t, ln_g, ln_b, seq):
    n, d = h.shape
    tm = TOKEN_TILE
    per_b = seq // tm
    batch = n // seq
    wn = w.shape[1]
    row = lambda i: (i, 0)
    bat = lambda i: (i // per_b, 0, 0)
    full2 = lambda i: (0, 0)
    res = lambda i: (i // per_b, 0, i % per_b, 0)
    qkv_specs = [pl.BlockSpec((1, r, tm // r, 3 * GROUP_W), res) for _, r in ATTN_GROUPS]
    qkv_shapes = [jax.ShapeDtypeStruct((batch, r, seq // r, 3 * GROUP_W), BF16)
                  for _, r in ATTN_GROUPS]
    return pl.pallas_call(
        _inproj_kernel,
        grid=(n // tm,),
        in_specs=[pl.BlockSpec((tm, d), row),
                  pl.BlockSpec((1, 1, d), bat),
                  pl.BlockSpec((1, 1, d), bat),
                  pl.BlockSpec((1, d), full2),
                  pl.BlockSpec((d, wn), full2),
                  pl.BlockSpec((tm, 128), row),
                  pl.BlockSpec((tm, 128), row),
                  pl.BlockSpec((GMLP_GROUPS, GMLP_CHUNK, GMLP_CHUNK), lambda i: (0, 0, 0)),
                  pl.BlockSpec((GMLP_CHUNK, GMLP_GROUPS), full2),
                  pl.BlockSpec((1, GMLP_W), full2),
                  pl.BlockSpec((1, GMLP_W), full2)],
        out_specs=qkv_specs + [pl.BlockSpec((tm, GMLP_W), row)],
        out_shape=qkv_shapes + [jax.ShapeDtypeStruct((n, GMLP_W), BF16)],
        scratch_shapes=[pltpu.VMEM((_n_strided_groups() * 3 * (GROUP_W // 128), tm, 128), F32)],
        compiler_params=_ARB(1),
        name="inproj_rope_gmlp",
    )(h, sc, sh, g, w, cos, sin, w_s, b_s_t, ln_g, ln_b)


def _attn_kernel(q_ref, k_ref, v_ref, kp_ref, vp_ref, o_ref, lse_ref):
    q_ref, k_ref, v_ref, kp_ref, vp_ref, o_ref, lse_ref = (
        x.at[0] for x in (q_ref, k_ref, v_ref, kp_ref, vp_ref, o_ref, lse_ref))
    qb = q_ref.shape[1]
    first_key = jnp.where(pl.program_id(2) == 0, WIN_BLOCK, 0)
    qi = lax.broadcasted_iota(I32, (WIN_BLOCK, 2 * WIN_BLOCK), 0)
    kj = lax.broadcasted_iota(I32, (WIN_BLOCK, 2 * WIN_BLOCK), 1)
    band = (kj >= qi) & (kj <= qi + WIN_BLOCK)
    low_lanes = lax.broadcasted_iota(I32, (WIN_BLOCK, 128), 1) < HEAD_DIM
    ones = jnp.ones((2 * WIN_BLOCK, 128), BF16)
    for res, i in [(res, i) for res in range(q_ref.shape[0]) for i in range(qb // WIN_BLOCK)]:
        rs = slice(i * WIN_BLOCK, (i + 1) * WIN_BLOCK)
        q = q_ref[res, rs, :]
        if i == 0:
            k_prev, v_prev = kp_ref[res], vp_ref[res]
            valid = band & (kj >= first_key)
        else:
            ps = slice((i - 1) * WIN_BLOCK, i * WIN_BLOCK)
            k_prev, v_prev = k_ref[res, ps, :], v_ref[res, ps, :]
            valid = band
        k_cat = jnp.concatenate([k_prev, k_ref[res, rs, :]], axis=0)
        v_cat = jnp.concatenate([v_prev, v_ref[res, rs, :]], axis=0)
        outs, lses = [], []
        for pair in range(GROUP_W // 128):
            cs = slice(pair * 128, (pair + 1) * 128)
            q_pair = q[:, cs].astype(F32)
            k_pair = k_cat[:, cs]
            v_ones = jnp.concatenate([v_cat[:, cs], ones], axis=1)
            o_half, l_half = [], []
            for head_lanes in (low_lanes, jnp.logical_not(low_lanes)):
                q_head = jnp.where(head_lanes, q_pair, 0.0).astype(BF16)
                s = lax.dot_general(q_head, k_pair, (((1,), (1,)), ((), ())),
                                    preferred_element_type=F32)
                s = jnp.where(valid, s, NEG_INF)
                m = jnp.max(s, axis=-1, keepdims=True)
                p = jnp.exp(s - m).astype(BF16)
                pv = _bdot(p, v_ones)
                den = pv[:, 128:129]
                o_half.append(pv[:, :128] / den)
                l_half.append(jnp.broadcast_to(m + jnp.log(den), (WIN_BLOCK, 128)))
            outs.append(jnp.where(low_lanes, o_half[0], o_half[1]))
            lses.append(jnp.where(low_lanes, l_half[0], l_half[1]))
        o_ref[res, rs, :] = jnp.concatenate(outs, axis=1).astype(o_ref.dtype)
        lse_ref[res, rs, :] = jnp.concatenate(lses, axis=1)


def _attention_group(qkv, g):
    batch, r, sub_len, _ = qkv.shape
    qb = min(ATTN_STEP_ROWS, sub_len)
    n_res = min(ATTN_STEP_ROWS // qb, r)
    n_tiles = sub_len // qb
    per_tile = qb // WIN_BLOCK

    def cur(part):
        return pl.BlockSpec((1, n_res, qb, GROUP_W), lambda b, j, n: (b, j, n, part))

    def prev(part):
        return pl.BlockSpec((1, n_res, WIN_BLOCK, GROUP_W),
                            lambda b, j, n: (b, j, jnp.maximum(n * per_tile - 1, 0), part))

    out_spec = pl.BlockSpec((1, n_res, qb, GROUP_W), lambda b, j, n: (b, j, n, 0))
    return pl.pallas_call(
        _attn_kernel,
        grid=(batch, r // n_res, n_tiles),
        in_specs=[cur(0), cur(1), cur(2), prev(1), prev(2)],
        out_specs=[out_spec, out_spec],
        out_shape=[jax.ShapeDtypeStruct((batch, r, sub_len, GROUP_W), BF16),
                   jax.ShapeDtypeStruct((batch, r, sub_len, GROUP_W), F32)],
        compiler_params=_ARB(3),
        name=f"dilated_attn_g{g}",
    )(qkv, qkv, qkv, qkv, qkv)


def _split_bf16(x):
    hi = x.astype(BF16)
    lo = (x - hi.astype(F32)).astype(BF16)
    return hi, lo


def _merge_kernel(h_ref, sc1_ref, sh1_ref, g1_ref, sc2_ref, sh2_ref, n1_ref, n2_ref,
                  wg_ref, wpa_ref, wpg_ref, wo_ref,
                  o0_ref, o1_ref, o2_ref, l0_ref, l1_ref, l2_ref, gm_ref,
                  wr_ref, br_ref,
                  hn_ref, f_ref, slot_ref, gate_ref, cnt_ref, slab_ref):
    tm, d = h_ref.shape
    slabs = iter(range(slab_ref.shape[0]))

    def token_major(ref):
        r = ref.shape[1]
        if r == 1:
            return ref[0, 0].astype(F32)
        halves = []
        for c in range(GROUP_W // 128):
            slab = next(slabs)
            for j in range(r):
                slab_ref[slab, pl.ds(j, tm // r, stride=r), :] = (
                    ref[0, j, :, c * 128:(c + 1) * 128].astype(F32))
            halves.append(slab_ref[slab])
        return jnp.concatenate(halves, axis=1)

    lses = [token_major(x) for x in (l0_ref, l1_ref, l2_ref)]
    outs = [token_major(x) for x in (o0_ref, o1_ref, o2_ref)]

    h = h_ref[...]
    a = _modulated_norm(h, n1_ref[...], sc1_ref[0], sh1_ref[0]).astype(BF16)
    gates = _sigmoid(_bdot(a, wg_ref[...]))

    m = jnp.maximum(jnp.maximum(lses[0], lses[1]), lses[2])
    es = [jnp.exp(l - m) for l in lses]
    den = es[0] + es[1] + es[2]
    attn = (es[0] * outs[0] + es[1] * outs[1] + es[2] * outs[2]) / den

    merged = (gates[:, :d] * _bdot(attn.astype(BF16), wpa_ref[...])
              + gates[:, d:] * _bdot(gm_ref[...], wpg_ref[...]))
    hn = h + g1_ref[0] * _bdot(merged.astype(BF16), wo_ref[...])
    hn_ref[...] = hn

    f = _modulated_norm(hn, n2_ref[...], sc2_ref[0], sh2_ref[0])
    f_ref[...] = f.astype(BF16)
    _route_tile(f, wr_ref, br_ref, slot_ref, gate_ref, cnt_ref)


def _route_tile(f, wr_ref, br_ref, slot_ref, gate_ref, cnt_ref):
    tm = f.shape[0]
    f_hi, f_lo = _split_bf16(f)
    w_hi, w_lo = _split_bf16(wr_ref[...])
    nt = (((1,), (1,)), ((), ()))
    logits = (lax.dot_general(w_hi, f_hi, nt, preferred_element_type=F32)
              + lax.dot_general(w_hi, f_lo, nt, preferred_element_type=F32)
              + lax.dot_general(w_lo, f_hi, nt, preferred_element_type=F32)) + br_ref[...]

    n_e = logits.shape[0]
    erow = lax.broadcasted_iota(I32, (n_e, tm), 0).astype(F32)
    vals, idxs, sels = [], [], []
    cur = logits
    for _ in range(TOP_K):
        mx = jnp.max(cur, axis=0, keepdims=True)
        ix = jnp.min(jnp.where(cur == mx, erow, float(n_e)), axis=0, keepdims=True)
        sel = erow == ix
        vals.append(mx)
        idxs.append(ix)
        sels.append(sel)
        cur = jnp.where(sel, -jnp.inf, cur)
    exps = [jnp.exp(v - vals[0]) for v in vals]
    tot = exps[0] + exps[1] + exps[2] + exps[3]
    gate_ref[...] = jnp.concatenate([e / tot for e in exps], axis=0)

    onehot = (sels[0] | sels[1] | sels[2] | sels[3])
    oh = jnp.where(onehot, 1.0, 0.0)
    t_r = lax.broadcasted_iota(I32, (tm, tm), 0)
    t_c = lax.broadcasted_iota(I32, (tm, tm), 1)
    upper = jnp.where(t_r < t_c, 1.0, 0.0).astype(BF16)
    prefix = _bdot(oh.astype(BF16), upper)
    cnt = jnp.sum(oh, axis=1, keepdims=True)
    cnt_al = jnp.floor((cnt + (ROW_ALIGN - 1)) * (1.0 / ROW_ALIGN)) * ROW_ALIGN
    e_r = lax.broadcasted_iota(I32, (n_e, n_e), 0)
    e_c = lax.broadcasted_iota(I32, (n_e, n_e), 1)
    lower = jnp.where(e_c < e_r, 1.0, 0.0).astype(BF16)
    start = _bdot(lower, jnp.broadcast_to(cnt_al, (n_e, 128)).astype(BF16))[:, 0:1]
    local = prefix + start
    slots = [jnp.sum(jnp.where(s, local, 0.0), axis=0, keepdims=True) for s in sels]
    slot_ref[...] = jnp.concatenate(slots, axis=0).astype(I32)
    cnt_ref[...] = jnp.broadcast_to(cnt, cnt_ref.shape).astype(I32)


def _merge_route(h, mods, n1, n2, wg, wpa, wpg, wo, outs, lses, gm, wr, br, seq):
    n, d = h.shape
    tm = TOKEN_TILE
    per_b = seq // tm
    n_tiles = n // tm
    row = lambda i: (i, 0)
    bat = lambda i: (i // per_b, 0, 0)
    full2 = lambda i: (0, 0)
    col = lambda i: (0, i)
    sc1, sh1, g1, sc2, sh2 = mods
    res = lambda i: (i // per_b, 0, i % per_b, 0)
    attn_specs = [pl.BlockSpec((1, r, tm // r, GROUP_W), res) for _, r in ATTN_GROUPS]
    in_specs = ([pl.BlockSpec((tm, d), row)]
                + [pl.BlockSpec((1, 1, d), bat)] * 5
                + [pl.BlockSpec((1, d), full2)] * 2
                + [pl.BlockSpec(wg.shape, full2), pl.BlockSpec(wpa.shape, full2),
                   pl.BlockSpec(wpg.shape, full2), pl.BlockSpec(wo.shape, full2)]
                + attn_specs * 2
                + [pl.BlockSpec((tm, GMLP_W), row)]
                + [pl.BlockSpec(wr.shape, full2), pl.BlockSpec(br.shape, full2)])
    out_specs = [pl.BlockSpec((tm, d), row), pl.BlockSpec((tm, d), row),
                 pl.BlockSpec((TOP_K, tm), col), pl.BlockSpec((TOP_K, tm), col),
                 pl.BlockSpec((N_EXPERTS, 128), row)]
    out_shape = [jax.ShapeDtypeStruct((n, d), F32), jax.ShapeDtypeStruct((n, d), BF16),
                 jax.ShapeDtypeStruct((TOP_K, n), I32), jax.ShapeDtypeStruct((TOP_K, n), F32),
                 jax.ShapeDtypeStruct((n_tiles * N_EXPERTS, 128), I32)]
    return pl.pallas_call(
        _merge_kernel,
        grid=(n_tiles,),
        in_specs=in_specs,
        out_specs=out_specs,
        out_shape=out_shape,
        scratch_shapes=[pltpu.VMEM((_n_strided_groups() * 2 * (GROUP_W // 128), tm, 128), F32)],
        compiler_params=_ARB(1),
        name="merge_proj_route",
    )(h, sc1, sh1, g1, sc2, sh2, n1, n2, wg, wpa, wpg, wo, *outs, *lses, gm, wr, br)


def _row_copy(src, dst, sem):
    return pltpu.make_async_copy(src, dst, sem)


def _start_runs(tile, off_ref, start_ref, pieces_ref, copy):
    def per_expert(e, total):
        base = tile * N_EXPERTS + e
        local0, global0, n = start_ref[base], off_ref[base], pieces_ref[base]
        n_big = lax.shift_right_logical(n, _BIG_SHIFT)

        def big(i, carry):
            row = pl.multiple_of(i * BIG_PIECE, ROW_ALIGN)
            copy(pl.multiple_of(local0 + row, ROW_ALIGN),
                 pl.multiple_of(global0 + row, ROW_ALIGN), BIG_PIECE).start()
            return carry

        def small(i, carry):
            row = pl.multiple_of(n_big * BIG_PIECE + i * ROW_ALIGN, ROW_ALIGN)
            copy(pl.multiple_of(local0 + row, ROW_ALIGN),
                 pl.multiple_of(global0 + row, ROW_ALIGN), ROW_ALIGN).start()
            return carry

        lax.fori_loop(0, n_big, big, 0)
        lax.fori_loop(0, n - lax.shift_left(n_big, _BIG_SHIFT), small, 0)
        return total + n

    return lax.fori_loop(0, N_EXPERTS, per_expert, 0)


def _wait_runs(total, copy):
    n_big = lax.shift_right_logical(total, _BIG_SHIFT)
    lax.fori_loop(0, n_big, lambda i, c: (copy(0, 0, BIG_PIECE).wait(), c)[1], 0)
    lax.fori_loop(0, total - lax.shift_left(n_big, _BIG_SHIFT),
                  lambda i, c: (copy(0, 0, ROW_ALIGN).wait(), c)[1], 0)


def _slot_hits(slot_ref, chunk):
    tm = slot_ref.shape[1]
    s = lax.broadcasted_iota(I32, (SORT_CHUNK, tm), 0) + chunk * SORT_CHUNK
    return [slot_ref[k:k + 1, :] == s for k in range(TOP_K)]


def _dispatch_kernel(zb_ref, nz_ref, tail_off_ref, tail_pieces_ref, off_ref, start_ref, pieces_ref,
                     slot_ref, f_ref, buf_ref, sorted_ref, zero_ref, pending_ref, sems, zsem):
    step = pl.program_id(0)
    last_step = pl.num_programs(0) - 1
    buf_slot = lax.rem(step, 2)
    sorted_now = sorted_ref.at[buf_slot]

    def zero_block(j):
        start = pl.multiple_of(zb_ref[j] * EXPERT_BLOCK, EXPERT_BLOCK)
        return _row_copy(zero_ref, buf_ref.at[pl.ds(start, EXPERT_BLOCK)], zsem)

    def zero_piece(local_row, global_row, rows):
        del local_row
        return _row_copy(zero_ref.at[pl.ds(0, rows)], buf_ref.at[pl.ds(global_row, rows)], zsem)

    @pl.when(step == 0)
    def _():
        zero_ref[...] = jnp.zeros_like(zero_ref)
        lax.fori_loop(0, nz_ref[0], lambda j, c: (zero_block(j).start(), c)[1], 0)
        _start_runs(0, tail_off_ref, tail_off_ref, tail_pieces_ref, zero_piece)

    f = f_ref[...]
    for c in range(sorted_now.shape[0] // SORT_CHUNK):
        hits = _slot_hits(slot_ref, c)
        onehot = jnp.where(hits[0] | hits[1] | hits[2] | hits[3], 1.0, 0.0).astype(BF16)
        sorted_now[c * SORT_CHUNK:(c + 1) * SORT_CHUNK, :] = _bdot(onehot, f)

    def copy_from(slot):
        def copy(local_row, global_row, rows):
            return _row_copy(sorted_ref.at[slot, pl.ds(local_row, rows)],
                             buf_ref.at[pl.ds(global_row, rows)], sems.at[slot])
        return copy

    total = _start_runs(step, off_ref, start_ref, pieces_ref, copy_from(buf_slot))

    @pl.when(step > 0)
    def _():
        _wait_runs(pending_ref[0], copy_from(1 - buf_slot))

    pending_ref[0] = total

    @pl.when(step == last_step)
    def _():
        _wait_runs(total, copy_from(buf_slot))
        lax.fori_loop(0, nz_ref[0], lambda j, c: (zero_block(j).wait(), c)[1], 0)
        tail_total = lax.fori_loop(0, N_EXPERTS, lambda e, t: t + tail_pieces_ref[e], 0)
        _wait_runs(tail_total, zero_piece)


def _dispatch(plan, slots, f, n_blocks):
    n, d = f.shape
    tm = TOKEN_TILE
    return pl.pallas_call(
        _dispatch_kernel,
        grid_spec=pltpu.PrefetchScalarGridSpec(
            num_scalar_prefetch=7, grid=(n // tm,),
            in_specs=[pl.BlockSpec((TOP_K, tm), lambda i, *_: (0, i)),
                      pl.BlockSpec((tm, d), lambda i, *_: (i, 0))],
            out_specs=pl.BlockSpec(memory_space=pl.ANY),
            scratch_shapes=[pltpu.VMEM((2, _local_rows(tm), d), F32),
                            pltpu.VMEM((EXPERT_BLOCK, d), F32),
                            pltpu.SMEM((1,), I32),
                            pltpu.SemaphoreType.DMA((2,)), pltpu.SemaphoreType.DMA(())]),
        out_shape=jax.ShapeDtypeStruct((n_blocks * EXPERT_BLOCK, d), F32),
        compiler_params=_ARB(1),
        name="dispatch_rows",
    )(plan["zero_blocks"], plan["n_zero"], plan["tail_off"], plan["tail_pieces"],
      plan["run_off"], plan["run_start"], plan["run_pieces"], slots, f)


def _expert_kernel(be_ref, first_ref, rows_ref, x_ref, w1_ref, b1_ref, w2_ref, b2_ref, o_ref,
                   w1b_ref, w2p_ref, slab_ref):
    del be_ref
    step = pl.program_id(0)
    n_rows = rows_ref[step]
    live = n_rows > 0
    ff, d = w2p_ref.shape
    half = 64

    @pl.when(live & (first_ref[step] == 1))
    def _():
        w1b_ref[...] = w1_ref[0, 0].astype(BF16)
        for c in range(d // 128):
            cs = slice(c * 128, (c + 1) * 128)
            for k in range(ff // 128):
                r0 = k * 128
                slab_ref[pl.ds(r0, half, stride=2), :] = w2_ref[0, 0, r0:r0 + half, cs]
                slab_ref[pl.ds(r0 + 1, half, stride=2), :] = w2_ref[0, 0, r0 + half:r0 + 128, cs]
            w2p_ref[:, cs] = slab_ref[...].astype(BF16)

    def ffn(rs):
        x = x_ref[rs, :].astype(BF16)
        hb = _bdot(x, w1b_ref[...]) + b1_ref[0, 0]
        lane = lax.broadcasted_iota(I32, (x.shape[0], 128), 1)
        even = (lane & 1) == 0
        glu, lin = [], []
        for k in range(ff // 128):
            a = hb[:, 256 * k:256 * k + 128]
            b = hb[:, 256 * k + 128:256 * k + 256]
            glu.append(jnp.where(even, a, pltpu.roll(b, 1, 1)))
            lin.append(jnp.where(even, pltpu.roll(a, 127, 1), b))
        x_glu = jnp.minimum(jnp.concatenate(glu, axis=1), SWIGLU_LIMIT)
        x_lin = jnp.clip(jnp.concatenate(lin, axis=1), -SWIGLU_LIMIT, SWIGLU_LIMIT)
        act = x_glu * _sigmoid(SWIGLU_ALPHA * x_glu) * (x_lin + 1.0)
        o_ref[rs, :] = _bdot(act.astype(BF16), w2p_ref[...]) + b2_ref[0, 0]

    for r0 in range(0, EXPERT_BLOCK, EXPERT_SUB):
        rs = slice(r0, r0 + EXPERT_SUB)
        pl.when(n_rows > r0)(functools.partial(ffn, rs))

        @pl.when(n_rows <= r0)
        def _():
            o_ref[rs, :] = jnp.zeros((EXPERT_SUB, d), F32)


def _experts(layer, block_e, first, block_rows, buf, w1, b1, w2, b2, n_blocks):
    d = buf.shape[1]
    ff = w2.shape[2]
    xmap = lambda i, be, fi, rw: (jnp.where(rw[i] > 0, i, 0), 0)
    wmap = lambda i, be, fi, rw: (layer, be[i], 0, 0)
    return pl.pallas_call(
        _expert_kernel,
        grid_spec=pltpu.PrefetchScalarGridSpec(
            num_scalar_prefetch=3, grid=(n_blocks,),
            in_specs=[pl.BlockSpec((EXPERT_BLOCK, d), xmap),
                      pl.BlockSpec((1, 1, d, 2 * ff), wmap),
                      pl.BlockSpec((1, 1, 1, 2 * ff), wmap),
                      pl.BlockSpec((1, 1, ff, d), wmap),
                      pl.BlockSpec((1, 1, 1, d), wmap)],
            out_specs=pl.BlockSpec((EXPERT_BLOCK, d), lambda i, be, fi, rw: (i, 0)),
            scratch_shapes=[pltpu.VMEM((d, 2 * ff), BF16), pltpu.VMEM((ff, d), BF16),
                            pltpu.VMEM((ff, 128), F32)]),
        out_shape=jax.ShapeDtypeStruct((n_blocks * EXPERT_BLOCK, d), F32),
        compiler_params=_ARB(1),
        name="expert_ffn",
    )(block_e, first, block_rows, buf, w1, b1, w2, b2)


def _combine_kernel(off_ref, start_ref, pieces_ref, slot_ref, gate_ref, h_ref, g2_ref, fin_ref,
                    obuf_ref, out_ref, rows_ref, sems, *, final_norm):
    step = pl.program_id(0)
    buf_slot = lax.rem(step, 2)
    rows_now = rows_ref.at[buf_slot]

    def copy_into(slot):
        def copy(local_row, global_row, rows):
            return _row_copy(obuf_ref.at[pl.ds(global_row, rows)],
                             rows_ref.at[slot, pl.ds(local_row, rows)], sems.at[slot])
        return copy

    @pl.when(step == 0)
    def _():
        rows_ref[...] = jnp.zeros_like(rows_ref)
        _start_runs(0, off_ref, start_ref, pieces_ref, copy_into(0))

    @pl.when(step + 1 < pl.num_programs(0))
    def _():
        _start_runs(step + 1, off_ref, start_ref, pieces_ref, copy_into(1 - buf_slot))

    total = lax.fori_loop(0, N_EXPERTS,
                          lambda e, t: t + pieces_ref[step * N_EXPERTS + e], 0)
    _wait_runs(total, copy_into(buf_slot))

    y = jnp.zeros(h_ref.shape, F32)
    for c in range(rows_now.shape[0] // SORT_CHUNK):
        hits = _slot_hits(slot_ref, c)
        weight = jnp.where(hits[0], gate_ref[0:1, :], 0.0)
        for k in range(1, TOP_K):
            weight = weight + jnp.where(hits[k], gate_ref[k:k + 1, :], 0.0)
        rows = rows_now[c * SORT_CHUNK:(c + 1) * SORT_CHUNK, :].astype(BF16)
        y = y + lax.dot_general(weight.astype(BF16), rows, (((0,), (0,)), ((), ())),
                                preferred_element_type=F32)
    hn = h_ref[...] + g2_ref[0] * y
    if final_norm:
        ms = jnp.mean(hn * hn, axis=-1, keepdims=True)
        hn = hn * lax.rsqrt(ms + RMS_EPS) * fin_ref[...]
    out_ref[...] = hn


def _combine(plan, slots, gate, h, g2, final_g, obuf, seq, final_norm):
    n, d = h.shape
    tm = TOKEN_TILE
    per_b = seq // tm
    return pl.pallas_call(
        functools.partial(_combine_kernel, final_norm=final_norm),
        grid_spec=pltpu.PrefetchScalarGridSpec(
            num_scalar_prefetch=3, grid=(n // tm,),
            in_specs=[pl.BlockSpec((TOP_K, tm), lambda i, *_: (0, i)),
                      pl.BlockSpec((TOP_K, tm), lambda i, *_: (0, i)),
                      pl.BlockSpec((tm, d), lambda i, *_: (i, 0)),
                      pl.BlockSpec((1, 1, d), lambda i, *_: (i // per_b, 0, 0)),
                      pl.BlockSpec((1, d), lambda i, *_: (0, 0)),
                      pl.BlockSpec(memory_space=pl.ANY)],
            out_specs=pl.BlockSpec((tm, d), lambda i, *_: (i, 0)),
            scratch_shapes=[pltpu.VMEM((2, _local_rows(tm), d), F32),
                            pltpu.SemaphoreType.DMA((2,))]),
        out_shape=jax.ShapeDtypeStruct((n, d), F32),
        compiler_params=_ARB(1),
        name="combine_rows",
    )(plan["run_off"], plan["run_start"], plan["run_pieces"], slots, gate, h, g2, final_g, obuf)


def _local_rows(tm):
    rows = tm * TOP_K + N_EXPERTS * (ROW_ALIGN - 1)
    return -(-rows // SORT_CHUNK) * SORT_CHUNK


def _num_blocks(n):
    rows = n * TOP_K + (n // TOKEN_TILE) * N_EXPERTS * (ROW_ALIGN - 1)
    return -(-rows // EXPERT_BLOCK) + N_EXPERTS


def _slot_plan(tile_counts, n):
    blk = EXPERT_BLOCK
    n_blocks = _num_blocks(n)
    cnt = (tile_counts + ROW_ALIGN - 1) // ROW_ALIGN * ROW_ALIGN
    total = jnp.sum(cnt, axis=0)
    padded = (total + blk - 1) // blk * blk
    pad_end = jnp.cumsum(padded)
    pad_start = pad_end - padded
    run_off = pad_start[None, :] + jnp.cumsum(cnt, axis=0) - cnt
    run_start = jnp.cumsum(cnt, axis=1) - cnt
    n_valid = pad_end[-1] // blk
    block_start = jnp.arange(n_blocks, dtype=I32) * blk
    block_e = jnp.sum(pad_end[None, :] <= block_start[:, None], axis=1)
    last_e = jnp.sum(pad_end <= (n_valid - 1) * blk)
    block_e = jnp.where(jnp.arange(n_blocks) < n_valid, block_e, last_e).astype(I32)
    first = jnp.concatenate([jnp.ones((1,), I32),
                             (block_e[1:] != block_e[:-1]).astype(I32)])
    hit = jnp.arange(N_EXPERTS)[None, :] == block_e[:, None]
    data_end = jnp.sum(jnp.where(hit, pad_start + total, 0), axis=1)
    block_rows = jnp.where(jnp.arange(n_blocks) < n_valid,
                           jnp.clip(data_end - block_start, 0, blk), 0).astype(I32)
    flat = lambda a: a.reshape(-1).astype(I32)
    zero_blocks = n_valid + jnp.arange(n_blocks - (n * TOP_K) // blk)
    return dict(block_e=block_e, first=first, block_rows=block_rows,
                zero_blocks=flat(jnp.minimum(zero_blocks, n_blocks - 1)),
                n_zero=(n_blocks - n_valid).reshape(1).astype(I32),
                tail_off=flat(pad_start + total), tail_pieces=flat((padded - total) // ROW_ALIGN),
                run_off=flat(run_off), run_start=flat(run_start),
                run_pieces=flat(cnt // ROW_ALIGN))


def kernel(x, c, positions, w_ada, b_ada, norm1_g, w_in, w_s, b_s, ln_g, ln_b, w_pa, w_pg,
           w_o, norm2_g, w_router, b_router, w1, b1, w2, b2, final_g):
    batch, seq, d = x.shape
    depth = w_ada.shape[0]
    n = batch * seq
    n_blocks = _num_blocks(n)

    mod = _modulation(c, w_ada, b_ada)
    cos, sin = _rope_tables(positions)

    n_mix = 3 * ATTN_W + 2 * GMLP_W
    b1r = b1.reshape(depth, N_EXPERTS, 1, -1)
    b2r = b2.reshape(depth, N_EXPERTS, 1, -1)

    h = x.reshape(n, d)
    for l in range(depth):
        ml = mod[l].reshape(batch, 6, 1, d)
        sh1, sc1, g1, sh2, sc2, g2 = [ml[:, i] for i in range(6)]
        w_mix = w_in[l, :, :n_mix].astype(BF16)
        w_gate = w_in[l, :, n_mix:].astype(BF16)

        *qkvs, gm = _inproj(h, sc1, sh1, norm1_g[l].reshape(1, d), w_mix, cos, sin,
                            w_s[l], b_s[l].T, ln_g[l].reshape(1, -1), ln_b[l].reshape(1, -1), seq)
        outs, lses = zip(*[_attention_group(qkvs[g], g) for g in range(N_GROUPS)])

        h, f, slots, gate, counts = _merge_route(
            h, (sc1, sh1, g1, sc2, sh2), norm1_g[l].reshape(1, d), norm2_g[l].reshape(1, d),
            w_gate, w_pa[l].astype(BF16), w_pg[l].astype(BF16), w_o[l].astype(BF16),
            outs, lses, gm, w_router[l].T, b_router[l].reshape(-1, 1), seq)

        plan = _slot_plan(counts[:, 0].reshape(-1, N_EXPERTS), n)
        buf = _dispatch(plan, slots, f, n_blocks)
        obuf = _experts(l, plan["block_e"], plan["first"], plan["block_rows"], buf,
                        w1, b1r, w2, b2r, n_blocks)

        h = _combine(plan, slots, gate, h, g2, final_g.reshape(1, d), obuf, seq,
                     final_norm=(l == depth - 1))
    return h.reshape(batch, seq, d)
```

```python
import functools

import jax
import jax.numpy as jnp
from jax import lax
from jax.experimental import pallas as pl
from jax.experimental.pallas import tpu as pltpu

F32 = jnp.float32
BF16 = jnp.bfloat16
I32 = jnp.int32

HEAD_DIM = 64
HEADS_PER_GROUP = 4
GROUP_W = HEADS_PER_GROUP * HEAD_DIM
ATTN_GROUPS = ((128, 1), (512, 4), (2048, 16))
N_GROUPS = len(ATTN_GROUPS)
ATTN_W = N_GROUPS * GROUP_W
WIN_BLOCK = 128
ROPE_THETA = 10000.0
GMLP_CHUNK = 128
GMLP_GROUPS = 4
GMLP_W = GMLP_GROUPS * 128
N_EXPERTS = 32
TOP_K = 4
SWIGLU_ALPHA = 1.702
SWIGLU_LIMIT = 7.0
RMS_EPS = 1e-5
LN_EPS = 1e-5
NEG_INF = -1e30

TOKEN_TILE = 512
PROJ_TILE = 1024
ATTN_STEP_ROWS = 2048
EXPERT_BLOCK = 512
EXPERT_SUB = 512
ROW_ALIGN = 8
BIG_PIECE = 32
_BIG_SHIFT = (BIG_PIECE // ROW_ALIGN).bit_length() - 1
SORT_CHUNK = 256
VMEM_LIMIT = 56 * 1024 * 1024

_ARB = lambda n: pltpu.CompilerParams(dimension_semantics=("arbitrary",) * n,
                                      vmem_limit_bytes=VMEM_LIMIT)


def _bdot(a, b):
    return jnp.dot(a, b, preferred_element_type=F32)


def _sigmoid(x):
    return 0.5 * jnp.tanh(0.5 * x) + 0.5


def _n_strided_groups():
    return sum(1 for _, r in ATTN_GROUPS if r > 1)


def _split3(x):
    hi = x.astype(BF16)
    r1 = x - hi.astype(F32)
    mid = r1.astype(BF16)
    lo = (r1 - mid.astype(F32)).astype(BF16)
    return hi, mid, lo


def _dot_f32(a, b):
    a0, a1, a2 = _split3(a)
    b0, b1, b2 = _split3(b)
    small = _bdot(a0, b2) + _bdot(a1, b1) + _bdot(a2, b0)
    mid = _bdot(a0, b1) + _bdot(a1, b0)
    return (small + mid) + _bdot(a0, b0)


def _mod_kernel(c_ref, w_ref, b_ref, o_ref):
    c = c_ref[...]
    s = c * jax.nn.sigmoid(c)
    o_ref[0] = _dot_f32(s, w_ref[0]) + b_ref[0]


def _modulation(c, w_ada, b_ada):
    depth, d, six_d = w_ada.shape
    b = c.shape[0]
    n_col = six_d // d
    return pl.pallas_call(
        _mod_kernel,
        grid=(depth, n_col),
        in_specs=[pl.BlockSpec((b, d), lambda l, j: (0, 0)),
                  pl.BlockSpec((1, d, d), lambda l, j: (l, 0, j)),
                  pl.BlockSpec((1, 1, d), lambda l, j: (l, 0, j))],
        out_specs=pl.BlockSpec((1, b, d), lambda l, j: (l, 0, j)),
        out_shape=jax.ShapeDtypeStruct((depth, b, six_d), F32),
        compiler_params=_ARB(2),
        name="adaln_mod",
    )(c, w_ada, b_ada.reshape(depth, 1, six_d))


def _rope_kernel(pos_ref, freq_ref, sign_ref, cos_ref, sin_ref):
    ang = pos_ref[...].astype(F32) * freq_ref[...]
    cos_ref[...] = jnp.cos(ang)
    sin_ref[...] = jnp.sin(ang) * sign_ref[...]


def _rope_tables(positions):
    n = positions.size
    half = HEAD_DIM // 2
    inv_freq = ROPE_THETA ** (-jnp.arange(half, dtype=F32) / half)
    freq = jnp.tile(inv_freq, 4).reshape(1, 128)
    sign = jnp.tile(jnp.concatenate([-jnp.ones((half,), F32), jnp.ones((half,), F32)]), 2)
    sign = sign.reshape(1, 128)
    tile = 1024
    return pl.pallas_call(
        _rope_kernel,
        grid=(n // tile,),
        in_specs=[pl.BlockSpec((tile, 1), lambda i: (i, 0)),
                  pl.BlockSpec((1, 128), lambda i: (0, 0)),
                  pl.BlockSpec((1, 128), lambda i: (0, 0))],
        out_specs=[pl.BlockSpec((tile, 128), lambda i: (i, 0))] * 2,
        out_shape=[jax.ShapeDtypeStruct((n, 128), F32)] * 2,
        compiler_params=_ARB(1),
        name="rope_tables",
    )(positions.reshape(n, 1), freq, sign)


def _modulated_norm(x, g, sc, sh):
    ms = jnp.mean(x * x, axis=-1, keepdims=True)
    y = x * lax.rsqrt(ms + RMS_EPS) * g
    return y * (1.0 + sc) + sh


def _inproj_kernel(h_ref, sc_ref, sh_ref, g_ref, w_ref, cos_ref, sin_ref,
                   ws_ref, bs_ref, lng_ref, lnb_ref, qkv0_ref, qkv1_ref, qkv2_ref, gm_ref,
                   slab_ref):
    tm = h_ref.shape[0]
    a = _modulated_norm(h_ref[...], g_ref[...], sc_ref[0], sh_ref[0]).astype(BF16)
    cos = cos_ref[...]
    sin = sin_ref[...]
    lane = lax.broadcasted_iota(I32, (tm, 128), 1)
    first_half = (lane % HEAD_DIM) < (HEAD_DIM // 2)

    def rope(x):
        swapped = jnp.where(first_half, pltpu.roll(x, 96, 1), pltpu.roll(x, 32, 1))
        return x * cos + swapped * sin

    slab = 0
    for g, out_ref in enumerate((qkv0_ref, qkv1_ref, qkv2_ref)):
        r = ATTN_GROUPS[g][1]
        for part in range(3):
            c0 = part * ATTN_W + g * GROUP_W
            y = _bdot(a, w_ref[:, c0:c0 + GROUP_W])
            halves = [y[:, :128], y[:, 128:]]
            if part < 2:
                halves = [rope(x) for x in halves]
            if part == 0:
                halves = [x * (HEAD_DIM ** -0.5) for x in halves]
            for c, x in enumerate(halves):
                o0 = part * GROUP_W + c * 128
                if r == 1:
                    out_ref[0, 0, :, o0:o0 + 128] = x.astype(BF16)
                else:
                    slab_ref[slab] = x
                    for j in range(r):
                        rows = slab_ref[slab, pl.ds(j, tm // r, stride=r), :]
                        out_ref[0, j, :, o0:o0 + 128] = rows.astype(BF16)
                    slab += 1

    c0 = 3 * ATTN_W
    u = jax.nn.gelu(_bdot(a, w_ref[:, c0:c0 + GMLP_W]))
    v = jax.nn.gelu(_bdot(a, w_ref[:, c0 + GMLP_W:c0 + 2 * GMLP_W]))
    mu = jnp.mean(v, axis=-1, keepdims=True)
    var = jnp.mean(jnp.square(v - mu), axis=-1, keepdims=True)
    v = ((v - mu) * lax.rsqrt(var + LN_EPS) * lng_ref[...] + lnb_ref[...]).astype(BF16)
    row = lax.broadcasted_iota(I32, (GMLP_CHUNK, GMLP_CHUNK), 0)
    col = lax.broadcasted_iota(I32, (GMLP_CHUNK, GMLP_CHUNK), 1)
    tril = col <= row
    chunks = [slice(c * GMLP_CHUNK, (c + 1) * GMLP_CHUNK) for c in range(tm // GMLP_CHUNK)]
    for g in range(GMLP_GROUPS):
        wsg = jnp.where(tril, ws_ref[g], 0.0).astype(BF16)
        cs = slice(g * 128, (g + 1) * 128)
        mixed = _bdot(wsg, jnp.concatenate([v[rs, cs] for rs in chunks], axis=1))
        mixed = mixed + bs_ref[:, g:g + 1]
        for c, rs in enumerate(chunks):
            gm_ref[rs, cs] = (u[rs, cs] * mixed[:, c * 128:(c + 1) * 128]).astype(BF16)


def _inproj(h, sc, sh, g, w, cos, sin, w_s, b_s_t, ln_g, ln_b, seq):
    n, d = h.shape
    tm = min(PROJ_TILE, seq)
    per_b = seq // tm
    batch = n // seq
    wn = w.shape[1]
    row = lambda i: (i, 0)
    bat = lambda i: (i // per_b, 0, 0)
    full2 = lambda i: (0, 0)
    res = lambda i: (i // per_b, 0, i % per_b, 0)
    qkv_specs = [pl.BlockSpec((1, r, tm // r, 3 * GROUP_W), res) for _, r in ATTN_GROUPS]
    qkv_shapes = [jax.ShapeDtypeStruct((batch, r, seq // r, 3 * GROUP_W), BF16)
                  for _, r in ATTN_GROUPS]
    return pl.pallas_call(
        _inproj_kernel,
        grid=(n // tm,),
        in_specs=[pl.BlockSpec((tm, d), row),
                  pl.BlockSpec((1, 1, d), bat),
                  pl.BlockSpec((1, 1, d), bat),
                  pl.BlockSpec((1, d), full2),
                  pl.BlockSpec((d, wn), full2),
                  pl.BlockSpec((tm, 128), row),
                  pl.BlockSpec((tm, 128), row),
                  pl.BlockSpec((GMLP_GROUPS, GMLP_CHUNK, GMLP_CHUNK), lambda i: (0, 0, 0)),
                  pl.BlockSpec((GMLP_CHUNK, GMLP_GROUPS), full2),
                  pl.BlockSpec((1, GMLP_W), full2),
                  pl.BlockSpec((1, GMLP_W), full2)],
        out_specs=qkv_specs + [pl.BlockSpec((tm, GMLP_W), row)],
        out_shape=qkv_shapes + [jax.ShapeDtypeStruct((n, GMLP_W), BF16)],
        scratch_shapes=[pltpu.VMEM((_n_strided_groups() * 3 * (GROUP_W // 128), tm, 128), F32)],
        compiler_params=_ARB(1),
        name="inproj_rope_gmlp",
    )(h, sc, sh, g, w, cos, sin, w_s, b_s_t, ln_g, ln_b)


def _attn_kernel(q_ref, k_ref, v_ref, kp_ref, vp_ref, o_ref, lse_ref):
    q_ref, k_ref, v_ref, kp_ref, vp_ref, o_ref, lse_ref = (
        x.at[0] for x in (q_ref, k_ref, v_ref, kp_ref, vp_ref, o_ref, lse_ref))
    qb = q_ref.shape[1]
    first_key = jnp.where(pl.program_id(2) == 0, WIN_BLOCK, 0)
    qi = lax.broadcasted_iota(I32, (WIN_BLOCK, 2 * WIN_BLOCK), 0)
    kj = lax.broadcasted_iota(I32, (WIN_BLOCK, 2 * WIN_BLOCK), 1)
    band = (kj >= qi) & (kj <= qi + WIN_BLOCK)
    low_lanes = lax.broadcasted_iota(I32, (WIN_BLOCK, 128), 1) < HEAD_DIM
    ones = jnp.ones((2 * WIN_BLOCK, 128), BF16)
    for res, i in [(res, i) for res in range(q_ref.shape[0]) for i in range(qb // WIN_BLOCK)]:
        rs = slice(i * WIN_BLOCK, (i + 1) * WIN_BLOCK)
        q = q_ref[res, rs, :]
        if i == 0:
            k_prev, v_prev = kp_ref[res], vp_ref[res]
            valid = band & (kj >= first_key)
        else:
            ps = slice((i - 1) * WIN_BLOCK, i * WIN_BLOCK)
            k_prev, v_prev = k_ref[res, ps, :], v_ref[res, ps, :]
            valid = band
        k_cat = jnp.concatenate([k_prev, k_ref[res, rs, :]], axis=0)
        v_cat = jnp.concatenate([v_prev, v_ref[res, rs, :]], axis=0)
        outs, lses = [], []
        for pair in range(GROUP_W // 128):
            cs = slice(pair * 128, (pair + 1) * 128)
            q_pair = q[:, cs].astype(F32)
            k_pair = k_cat[:, cs]
            v_ones = jnp.concatenate([v_cat[:, cs], ones], axis=1)
            o_half, l_half = [], []
            for head_lanes in (low_lanes, jnp.logical_not(low_lanes)):
                q_head = jnp.where(head_lanes, q_pair, 0.0).astype(BF16)
                s = lax.dot_general(q_head, k_pair, (((1,), (1,)), ((), ())),
                                    preferred_element_type=F32)
                s = jnp.where(valid, s, NEG_INF)
                m = jnp.max(s, axis=-1, keepdims=True)
                p = jnp.exp(s - m).astype(BF16)
                pv = _bdot(p, v_ones)
                den = pv[:, 128:129]
                o_half.append(pv[:, :128] / den)
                l_half.append(jnp.broadcast_to(m + jnp.log(den), (WIN_BLOCK, 128)))
            outs.append(jnp.where(low_lanes, o_half[0], o_half[1]))
            lses.append(jnp.where(low_lanes, l_half[0], l_half[1]))
        o_ref[res, rs, :] = jnp.concatenate(outs, axis=1).astype(o_ref.dtype)
        lse_ref[res, rs, :] = jnp.concatenate(lses, axis=1)


def _attention_group(qkv, g):
    batch, r, sub_len, _ = qkv.shape
    qb = min(ATTN_STEP_ROWS, sub_len)
    n_res = min(ATTN_STEP_ROWS // qb, r)
    n_tiles = sub_len // qb
    per_tile = qb // WIN_BLOCK

    def cur(part):
        return pl.BlockSpec((1, n_res, qb, GROUP_W), lambda b, j, n: (b, j, n, part))

    def prev(part):
        return pl.BlockSpec((1, n_res, WIN_BLOCK, GROUP_W),
                            lambda b, j, n: (b, j, jnp.maximum(n * per_tile - 1, 0), part))

    out_spec = pl.BlockSpec((1, n_res, qb, GROUP_W), lambda b, j, n: (b, j, n, 0))
    return pl.pallas_call(
        _attn_kernel,
        grid=(batch, r // n_res, n_tiles),
        in_specs=[cur(0), cur(1), cur(2), prev(1), prev(2)],
        out_specs=[out_spec, out_spec],
        out_shape=[jax.ShapeDtypeStruct((batch, r, sub_len, GROUP_W), BF16),
                   jax.ShapeDtypeStruct((batch, r, sub_len, GROUP_W), F32)],
        compiler_params=_ARB(3),
        name=f"dilated_attn_g{g}",
    )(qkv, qkv, qkv, qkv, qkv)


def _split_bf16(x):
    hi = x.astype(BF16)
    lo = (x - hi.astype(F32)).astype(BF16)
    return hi, lo


def _merge_kernel(h_ref, sc1_ref, sh1_ref, g1_ref, sc2_ref, sh2_ref, n1_ref, n2_ref,
                  wg_ref, wpa_ref, wpg_ref, wo_ref,
                  o0_ref, o1_ref, o2_ref, l0_ref, l1_ref, l2_ref, gm_ref,
                  wr_ref, br_ref,
                  hn_ref, f_ref, slot_ref, gate_ref, cnt_ref, slab_ref):
    tm, d = h_ref.shape
    slabs = iter(range(slab_ref.shape[0]))

    def token_major(ref):
        r = ref.shape[1]
        if r == 1:
            return ref[0, 0].astype(F32)
        halves = []
        for c in range(GROUP_W // 128):
            slab = next(slabs)
            for j in range(r):
                slab_ref[slab, pl.ds(j, tm // r, stride=r), :] = (
                    ref[0, j, :, c * 128:(c + 1) * 128].astype(F32))
            halves.append(slab_ref[slab])
        return jnp.concatenate(halves, axis=1)

    lses = [token_major(x) for x in (l0_ref, l1_ref, l2_ref)]
    outs = [token_major(x) for x in (o0_ref, o1_ref, o2_ref)]

    h = h_ref[...]
    a = _modulated_norm(h, n1_ref[...], sc1_ref[0], sh1_ref[0]).astype(BF16)
    gates = _sigmoid(_bdot(a, wg_ref[...]))

    m = jnp.maximum(jnp.maximum(lses[0], lses[1]), lses[2])
    es = [jnp.exp(l - m) for l in lses]
    den = es[0] + es[1] + es[2]
    attn = (es[0] * outs[0] + es[1] * outs[1] + es[2] * outs[2]) / den

    merged = (gates[:, :d] * _bdot(attn.astype(BF16), wpa_ref[...])
              + gates[:, d:] * _bdot(gm_ref[...], wpg_ref[...]))
    hn = h + g1_ref[0] * _bdot(merged.astype(BF16), wo_ref[...])
    hn_ref[...] = hn

    f = _modulated_norm(hn, n2_ref[...], sc2_ref[0], sh2_ref[0])
    f_ref[...] = f.astype(BF16)
    _route_tile(f, wr_ref, br_ref, slot_ref, gate_ref, cnt_ref)


def _route_tile(f, wr_ref, br_ref, slot_ref, gate_ref, cnt_ref):
    tm = f.shape[0]
    f_hi, f_lo = _split_bf16(f)
    w_hi, w_lo = _split_bf16(wr_ref[...])
    nt = (((1,), (1,)), ((), ()))
    logits = (lax.dot_general(w_hi, f_hi, nt, preferred_element_type=F32)
              + lax.dot_general(w_hi, f_lo, nt, preferred_element_type=F32)
              + lax.dot_general(w_lo, f_hi, nt, preferred_element_type=F32)) + br_ref[...]

    n_e = logits.shape[0]
    erow = lax.broadcasted_iota(I32, (n_e, tm), 0).astype(F32)
    vals, idxs, sels = [], [], []
    cur = logits
    for _ in range(TOP_K):
        mx = jnp.max(cur, axis=0, keepdims=True)
        ix = jnp.min(jnp.where(cur == mx, erow, float(n_e)), axis=0, keepdims=True)
        sel = erow == ix
        vals.append(mx)
        idxs.append(ix)
        sels.append(sel)
        cur = jnp.where(sel, -jnp.inf, cur)
    exps = [jnp.exp(v - vals[0]) for v in vals]
    tot = exps[0] + exps[1] + exps[2] + exps[3]
    gate_ref[...] = jnp.concatenate([e / tot for e in exps], axis=0)

    onehot = (sels[0] | sels[1] | sels[2] | sels[3])
    oh = jnp.where(onehot, 1.0, 0.0)
    t_r = lax.broadcasted_iota(I32, (tm, tm), 0)
    t_c = lax.broadcasted_iota(I32, (tm, tm), 1)
    upper = jnp.where(t_r < t_c, 1.0, 0.0).astype(BF16)
    prefix = _bdot(oh.astype(BF16), upper)
    cnt = jnp.sum(oh, axis=1, keepdims=True)
    cnt_al = jnp.floor((cnt + (ROW_ALIGN - 1)) * (1.0 / ROW_ALIGN)) * ROW_ALIGN
    e_r = lax.broadcasted_iota(I32, (n_e, n_e), 0)
    e_c = lax.broadcasted_iota(I32, (n_e, n_e), 1)
    lower = jnp.where(e_c < e_r, 1.0, 0.0).astype(BF16)
    start = _bdot(lower, jnp.broadcast_to(cnt_al, (n_e, 128)).astype(BF16))[:, 0:1]
    local = prefix + start
    slots = [jnp.sum(jnp.where(s, local, 0.0), axis=0, keepdims=True) for s in sels]
    slot_ref[...] = jnp.concatenate(slots, axis=0).astype(I32)
    cnt_ref[...] = jnp.broadcast_to(cnt, cnt_ref.shape).astype(I32)


def _merge_route(h, mods, n1, n2, wg, wpa, wpg, wo, outs, lses, gm, wr, br, seq):
    n, d = h.shape
    tm = TOKEN_TILE
    per_b = seq // tm
    n_tiles = n // tm
    row = lambda i: (i, 0)
    bat = lambda i: (i // per_b, 0, 0)
    full2 = lambda i: (0, 0)
    col = lambda i: (0, i)
    sc1, sh1, g1, sc2, sh2 = mods
    res = lambda i: (i // per_b, 0, i % per_b, 0)
    attn_specs = [pl.BlockSpec((1, r, tm // r, GROUP_W), res) for _, r in ATTN_GROUPS]
    in_specs = ([pl.BlockSpec((tm, d), row)]
                + [pl.BlockSpec((1, 1, d), bat)] * 5
                + [pl.BlockSpec((1, d), full2)] * 2
                + [pl.BlockSpec(wg.shape, full2), pl.BlockSpec(wpa.shape, full2),
                   pl.BlockSpec(wpg.shape, full2), pl.BlockSpec(wo.shape, full2)]
                + attn_specs * 2
                + [pl.BlockSpec((tm, GMLP_W), row)]
                + [pl.BlockSpec(wr.shape, full2), pl.BlockSpec(br.shape, full2)])
    out_specs = [pl.BlockSpec((tm, d), row), pl.BlockSpec((tm, d), row),
                 pl.BlockSpec((TOP_K, tm), col), pl.BlockSpec((TOP_K, tm), col),
                 pl.BlockSpec((N_EXPERTS, 128), row)]
    out_shape = [jax.ShapeDtypeStruct((n, d), F32), jax.ShapeDtypeStruct((n, d), BF16),
                 jax.ShapeDtypeStruct((TOP_K, n), I32), jax.ShapeDtypeStruct((TOP_K, n), F32),
                 jax.ShapeDtypeStruct((n_tiles * N_EXPERTS, 128), I32)]
    return pl.pallas_call(
        _merge_kernel,
        grid=(n_tiles,),
        in_specs=in_specs,
        out_specs=out_specs,
        out_shape=out_shape,
        scratch_shapes=[pltpu.VMEM((_n_strided_groups() * 2 * (GROUP_W // 128), tm, 128), F32)],
        compiler_params=_ARB(1),
        name="merge_proj_route",
    )(h, sc1, sh1, g1, sc2, sh2, n1, n2, wg, wpa, wpg, wo, *outs, *lses, gm, wr, br)


def _row_copy(src, dst, sem):
    return pltpu.make_async_copy(src, dst, sem)


def _start_runs(tile, off_ref, start_ref, pieces_ref, copy):
    def per_expert(e, total):
        base = tile * N_EXPERTS + e
        local0, global0, n = start_ref[base], off_ref[base], pieces_ref[base]
        n_big = lax.shift_right_logical(n, _BIG_SHIFT)

        def big(i, carry):
            row = pl.multiple_of(i * BIG_PIECE, ROW_ALIGN)
            copy(pl.multiple_of(local0 + row, ROW_ALIGN),
                 pl.multiple_of(global0 + row, ROW_ALIGN), BIG_PIECE).start()
            return carry

        def small(i, carry):
            row = pl.multiple_of(n_big * BIG_PIECE + i * ROW_ALIGN, ROW_ALIGN)
            copy(pl.multiple_of(local0 + row, ROW_ALIGN),
                 pl.multiple_of(global0 + row, ROW_ALIGN), ROW_ALIGN).start()
            return carry

        lax.fori_loop(0, n_big, big, 0)
        lax.fori_loop(0, n - lax.shift_left(n_big, _BIG_SHIFT), small, 0)
        return total + n

    return lax.fori_loop(0, N_EXPERTS, per_expert, 0)


def _wait_runs(total, copy):
    n_big = lax.shift_right_logical(total, _BIG_SHIFT)
    lax.fori_loop(0, n_big, lambda i, c: (copy(0, 0, BIG_PIECE).wait(), c)[1], 0)
    lax.fori_loop(0, total - lax.shift_left(n_big, _BIG_SHIFT),
                  lambda i, c: (copy(0, 0, ROW_ALIGN).wait(), c)[1], 0)


def _slot_hits(slot_ref, chunk):
    tm = slot_ref.shape[1]
    s = lax.broadcasted_iota(I32, (SORT_CHUNK, tm), 0) + chunk * SORT_CHUNK
    return [slot_ref[k:k + 1, :] == s for k in range(TOP_K)]


def _dispatch_kernel(zb_ref, nz_ref, tail_off_ref, tail_pieces_ref, off_ref, start_ref, pieces_ref,
                     slot_ref, f_ref, buf_ref, sorted_ref, zero_ref, pending_ref, sems, zsem):
    step = pl.program_id(0)
    last_step = pl.num_programs(0) - 1
    buf_slot = lax.rem(step, 2)
    sorted_now = sorted_ref.at[buf_slot]

    def zero_block(j):
        start = pl.multiple_of(zb_ref[j] * EXPERT_BLOCK, EXPERT_BLOCK)
        return _row_copy(zero_ref, buf_ref.at[pl.ds(start, EXPERT_BLOCK)], zsem)

    def zero_piece(local_row, global_row, rows):
        del local_row
        return _row_copy(zero_ref.at[pl.ds(0, rows)], buf_ref.at[pl.ds(global_row, rows)], zsem)

    @pl.when(step == 0)
    def _():
        zero_ref[...] = jnp.zeros_like(zero_ref)
        lax.fori_loop(0, nz_ref[0], lambda j, c: (zero_block(j).start(), c)[1], 0)
        _start_runs(0, tail_off_ref, tail_off_ref, tail_pieces_ref, zero_piece)

    f = f_ref[...]
    for c in range(sorted_now.shape[0] // SORT_CHUNK):
        hits = _slot_hits(slot_ref, c)
        onehot = jnp.where(hits[0] | hits[1] | hits[2] | hits[3], 1.0, 0.0).astype(BF16)
        sorted_now[c * SORT_CHUNK:(c + 1) * SORT_CHUNK, :] = _bdot(onehot, f)

    def copy_from(slot):
        def copy(local_row, global_row, rows):
            return _row_copy(sorted_ref.at[slot, pl.ds(local_row, rows)],
                             buf_ref.at[pl.ds(global_row, rows)], sems.at[slot])
        return copy

    total = _start_runs(step, off_ref, start_ref, pieces_ref, copy_from(buf_slot))

    @pl.when(step > 0)
    def _():
        _wait_runs(pending_ref[0], copy_from(1 - buf_slot))

    pending_ref[0] = total

    @pl.when(step == last_step)
    def _():
        _wait_runs(total, copy_from(buf_slot))
        lax.fori_loop(0, nz_ref[0], lambda j, c: (zero_block(j).wait(), c)[1], 0)
        tail_total = lax.fori_loop(0, N_EXPERTS, lambda e, t: t + tail_pieces_ref[e], 0)
        _wait_runs(tail_total, zero_piece)


def _dispatch(plan, slots, f, n_blocks):
    n, d = f.shape
    tm = TOKEN_TILE
    return pl.pallas_call(
        _dispatch_kernel,
        grid_spec=pltpu.PrefetchScalarGridSpec(
            num_scalar_prefetch=7, grid=(n // tm,),
            in_specs=[pl.BlockSpec((TOP_K, tm), lambda i, *_: (0, i)),
                      pl.BlockSpec((tm, d), lambda i, *_: (i, 0))],
            out_specs=pl.BlockSpec(memory_space=pl.ANY),
            scratch_shapes=[pltpu.VMEM((2, _local_rows(tm), d), F32),
                            pltpu.VMEM((EXPERT_BLOCK, d), F32),
                            pltpu.SMEM((1,), I32),
                            pltpu.SemaphoreType.DMA((2,)), pltpu.SemaphoreType.DMA(())]),
        out_shape=jax.ShapeDtypeStruct((n_blocks * EXPERT_BLOCK, d), F32),
        compiler_params=_ARB(1),
        name="dispatch_rows",
    )(plan["zero_blocks"], plan["n_zero"], plan["tail_off"], plan["tail_pieces"],
      plan["run_off"], plan["run_start"], plan["run_pieces"], slots, f)


def _expert_kernel(be_ref, first_ref, rows_ref, x_ref, w1_ref, b1_ref, w2_ref, b2_ref, o_ref,
                   w1b_ref, w2p_ref, slab_ref):
    del be_ref
    step = pl.program_id(0)
    n_rows = rows_ref[step]
    live = n_rows > 0
    ff, d = w2p_ref.shape
    half = 64

    @pl.when(live & (first_ref[step] == 1))
    def _():
        w1b_ref[...] = w1_ref[0, 0].astype(BF16)
        for c in range(d // 128):
            cs = slice(c * 128, (c + 1) * 128)
            for k in range(ff // 128):
                r0 = k * 128
                slab_ref[pl.ds(r0, half, stride=2), :] = w2_ref[0, 0, r0:r0 + half, cs]
                slab_ref[pl.ds(r0 + 1, half, stride=2), :] = w2_ref[0, 0, r0 + half:r0 + 128, cs]
            w2p_ref[:, cs] = slab_ref[...].astype(BF16)

    def ffn(rs):
        x = x_ref[rs, :].astype(BF16)
        hb = _bdot(x, w1b_ref[...]) + b1_ref[0, 0]
        lane = lax.broadcasted_iota(I32, (x.shape[0], 128), 1)
        even = (lane & 1) == 0
        glu, lin = [], []
        for k in range(ff // 128):
            a = hb[:, 256 * k:256 * k + 128]
            b = hb[:, 256 * k + 128:256 * k + 256]
            glu.append(jnp.where(even, a, pltpu.roll(b, 1, 1)))
            lin.append(jnp.where(even, pltpu.roll(a, 127, 1), b))
        x_glu = jnp.minimum(jnp.concatenate(glu, axis=1), SWIGLU_LIMIT)
        x_lin = jnp.clip(jnp.concatenate(lin, axis=1), -SWIGLU_LIMIT, SWIGLU_LIMIT)
        act = x_glu * _sigmoid(SWIGLU_ALPHA * x_glu) * (x_lin + 1.0)
        o_ref[rs, :] = _bdot(act.astype(BF16), w2p_ref[...]) + b2_ref[0, 0]

    for r0 in range(0, EXPERT_BLOCK, EXPERT_SUB):
        rs = slice(r0, r0 + EXPERT_SUB)
        pl.when(n_rows > r0)(functools.partial(ffn, rs))

        @pl.when(n_rows <= r0)
        def _():
            o_ref[rs, :] = jnp.zeros((EXPERT_SUB, d), F32)


def _experts(layer, block_e, first, block_rows, buf, w1, b1, w2, b2, n_blocks):
    d = buf.shape[1]
    ff = w2.shape[2]
    xmap = lambda i, be, fi, rw: (jnp.where(rw[i] > 0, i, 0), 0)
    wmap = lambda i, be, fi, rw: (layer, be[i], 0, 0)
    return pl.pallas_call(
        _expert_kernel,
        grid_spec=pltpu.PrefetchScalarGridSpec(
            num_scalar_prefetch=3, grid=(n_blocks,),
            in_specs=[pl.BlockSpec((EXPERT_BLOCK, d), xmap),
                      pl.BlockSpec((1, 1, d, 2 * ff), wmap),
                      pl.BlockSpec((1, 1, 1, 2 * ff), wmap),
                      pl.BlockSpec((1, 1, ff, d), wmap),
                      pl.BlockSpec((1, 1, 1, d), wmap)],
            out_specs=pl.BlockSpec((EXPERT_BLOCK, d), lambda i, be, fi, rw: (i, 0)),
            scratch_shapes=[pltpu.VMEM((d, 2 * ff), BF16), pltpu.VMEM((ff, d), BF16),
                            pltpu.VMEM((ff, 128), F32)]),
        out_shape=jax.ShapeDtypeStruct((n_blocks * EXPERT_BLOCK, d), F32),
        compiler_params=_ARB(1),
        name="expert_ffn",
    )(block_e, first, block_rows, buf, w1, b1, w2, b2)


def _combine_kernel(off_ref, start_ref, pieces_ref, slot_ref, gate_ref, h_ref, g2_ref, fin_ref,
                    obuf_ref, out_ref, rows_ref, sems, *, final_norm):
    step = pl.program_id(0)
    buf_slot = lax.rem(step, 2)
    rows_now = rows_ref.at[buf_slot]

    def copy_into(slot):
        def copy(local_row, global_row, rows):
            return _row_copy(obuf_ref.at[pl.ds(global_row, rows)],
                             rows_ref.at[slot, pl.ds(local_row, rows)], sems.at[slot])
        return copy

    @pl.when(step == 0)
    def _():
        rows_ref[...] = jnp.zeros_like(rows_ref)
        _start_runs(0, off_ref, start_ref, pieces_ref, copy_into(0))

    @pl.when(step + 1 < pl.num_programs(0))
    def _():
        _start_runs(step + 1, off_ref, start_ref, pieces_ref, copy_into(1 - buf_slot))

    total = lax.fori_loop(0, N_EXPERTS,
                          lambda e, t: t + pieces_ref[step * N_EXPERTS + e], 0)
    _wait_runs(total, copy_into(buf_slot))

    y = jnp.zeros(h_ref.shape, F32)
    for c in range(rows_now.shape[0] // SORT_CHUNK):
        hits = _slot_hits(slot_ref, c)
        weight = jnp.where(hits[0], gate_ref[0:1, :], 0.0)
        for k in range(1, TOP_K):
            weight = weight + jnp.where(hits[k], gate_ref[k:k + 1, :], 0.0)
        rows = rows_now[c * SORT_CHUNK:(c + 1) * SORT_CHUNK, :].astype(BF16)
        y = y + lax.dot_general(weight.astype(BF16), rows, (((0,), (0,)), ((), ())),
                                preferred_element_type=F32)
    hn = h_ref[...] + g2_ref[0] * y
    if final_norm:
        ms = jnp.mean(hn * hn, axis=-1, keepdims=True)
        hn = hn * lax.rsqrt(ms + RMS_EPS) * fin_ref[...]
    out_ref[...] = hn


def _combine(plan, slots, gate, h, g2, final_g, obuf, seq, final_norm):
    n, d = h.shape
    tm = TOKEN_TILE
    per_b = seq // tm
    return pl.pallas_call(
        functools.partial(_combine_kernel, final_norm=final_norm),
        grid_spec=pltpu.PrefetchScalarGridSpec(
            num_scalar_prefetch=3, grid=(n // tm,),
            in_specs=[pl.BlockSpec((TOP_K, tm), lambda i, *_: (0, i)),
                      pl.BlockSpec((TOP_K, tm), lambda i, *_: (0, i)),
                      pl.BlockSpec((tm, d), lambda i, *_: (i, 0)),
                      pl.BlockSpec((1, 1, d), lambda i, *_: (i // per_b, 0, 0)),
                      pl.BlockSpec((1, d), lambda i, *_: (0, 0)),
                      pl.BlockSpec(memory_space=pl.ANY)],
            out_specs=pl.BlockSpec((tm, d), lambda i, *_: (i, 0)),
            scratch_shapes=[pltpu.VMEM((2, _local_rows(tm), d), F32),
                            pltpu.SemaphoreType.DMA((2,))]),
        out_shape=jax.ShapeDtypeStruct((n, d), F32),
        compiler_params=_ARB(1),
        name="combine_rows",
    )(plan["run_off"], plan["run_start"], plan["run_pieces"], slots, gate, h, g2, final_g, obuf)


def _local_rows(tm):
    rows = tm * TOP_K + N_EXPERTS * (ROW_ALIGN - 1)
    return -(-rows // SORT_CHUNK) * SORT_CHUNK


def _num_blocks(n):
    rows = n * TOP_K + (n // TOKEN_TILE) * N_EXPERTS * (ROW_ALIGN - 1)
    return -(-rows // EXPERT_BLOCK) + N_EXPERTS


def _slot_plan(tile_counts, n):
    blk = EXPERT_BLOCK
    n_blocks = _num_blocks(n)
    cnt = (tile_counts + ROW_ALIGN - 1) // ROW_ALIGN * ROW_ALIGN
    total = jnp.sum(cnt, axis=0)
    padded = (total + blk - 1) // blk * blk
    pad_end = jnp.cumsum(padded)
    pad_start = pad_end - padded
    run_off = pad_start[None, :] + jnp.cumsum(cnt, axis=0) - cnt
    run_start = jnp.cumsum(cnt, axis=1) - cnt
    n_valid = pad_end[-1] // blk
    block_start = jnp.arange(n_blocks, dtype=I32) * blk
    block_e = jnp.sum(pad_end[None, :] <= block_start[:, None], axis=1)
    last_e = jnp.sum(pad_end <= (n_valid - 1) * blk)
    block_e = jnp.where(jnp.arange(n_blocks) < n_valid, block_e, last_e).astype(I32)
    first = jnp.concatenate([jnp.ones((1,), I32),
                             (block_e[1:] != block_e[:-1]).astype(I32)])
    hit = jnp.arange(N_EXPERTS)[None, :] == block_e[:, None]
    data_end = jnp.sum(jnp.where(hit, pad_start + total, 0), axis=1)
    block_rows = jnp.where(jnp.arange(n_blocks) < n_valid,
                           jnp.clip(data_end - block_start, 0, blk), 0).astype(I32)
    flat = lambda a: a.reshape(-1).astype(I32)
    zero_blocks = n_valid + jnp.arange(n_blocks - (n * TOP_K) // blk)
    return dict(block_e=block_e, first=first, block_rows=block_rows,
                zero_blocks=flat(jnp.minimum(zero_blocks, n_blocks - 1)),
                n_zero=(n_blocks - n_valid).reshape(1).astype(I32),
                tail_off=flat(pad_start + total), tail_pieces=flat((padded - total) // ROW_ALIGN),
                run_off=flat(run_off), run_start=flat(run_start),
                run_pieces=flat(cnt // ROW_ALIGN))


def kernel(x, c, positions, w_ada, b_ada, norm1_g, w_in, w_s, b_s, ln_g, ln_b, w_pa, w_pg,
           w_o, norm2_g, w_router, b_router, w1, b1, w2, b2, final_g):
    batch, seq, d = x.shape
    depth = w_ada.shape[0]
    n = batch * seq
    n_blocks = _num_blocks(n)

    mod = _modulation(c, w_ada, b_ada)
    cos, sin = _rope_tables(positions)

    n_mix = 3 * ATTN_W + 2 * GMLP_W
    b1r = b1.reshape(depth, N_EXPERTS, 1, -1)
    b2r = b2.reshape(depth, N_EXPERTS, 1, -1)

    h = x.reshape(n, d)
    for l in range(depth):
        ml = mod[l].reshape(batch, 6, 1, d)
        sh1, sc1, g1, sh2, sc2, g2 = [ml[:, i] for i in range(6)]
        w_mix = w_in[l, :, :n_mix].astype(BF16)
        w_gate = w_in[l, :, n_mix:].astype(BF16)

        *qkvs, gm = _inproj(h, sc1, sh1, norm1_g[l].reshape(1, d), w_mix, cos, sin,
                            w_s[l], b_s[l].T, ln_g[l].reshape(1, -1), ln_b[l].reshape(1, -1), seq)
        outs, lses = zip(*[_attention_group(qkvs[g], g) for g in range(N_GROUPS)])

        h, f, slots, gate, counts = _merge_route(
            h, (sc1, sh1, g1, sc2, sh2), norm1_g[l].reshape(1, d), norm2_g[l].reshape(1, d),
            w_gate, w_pa[l].astype(BF16), w_pg[l].astype(BF16), w_o[l].astype(BF16),
            outs, lses, gm, w_router[l].T, b_router[l].reshape(-1, 1), seq)

        plan = _slot_plan(counts[:, 0].reshape(-1, N_EXPERTS), n)
        buf = _dispatch(plan, slots, f, n_blocks)
        obuf = _experts(l, plan["block_e"], plan["first"], plan["block_rows"], buf,
                        w1, b1r, w2, b2r, n_blocks)

        h = _combine(plan, slots, gate, h, g2, final_g.reshape(1, d), obuf, seq,
                     final_norm=(l == depth - 1))
    return h.reshape(batch, seq, d)
```

```python
import functools

import jax
import jax.numpy as jnp
from jax import lax
from jax.experimental import pallas as pl
from jax.experimental.pallas import tpu as pltpu

F32 = jnp.float32
BF16 = jnp.bfloat16
I32 = jnp.int32

HEAD_DIM = 64
HEADS_PER_GROUP = 4
GROUP_W = HEADS_PER_GROUP * HEAD_DIM
ATTN_GROUPS = ((128, 1), (512, 4), (2048, 16))
N_GROUPS = len(ATTN_GROUPS)
ATTN_W = N_GROUPS * GROUP_W
WIN_BLOCK = 128
ROPE_THETA = 10000.0
GMLP_CHUNK = 128
GMLP_GROUPS = 4
GMLP_W = GMLP_GROUPS * 128
N_EXPERTS = 32
TOP_K = 4
SWIGLU_ALPHA = 1.702
SWIGLU_LIMIT = 7.0
RMS_EPS = 1e-5
LN_EPS = 1e-5
NEG_INF = -1e30

TOKEN_TILE = 512
PROJ_TILE = 1024
ATTN_STEP_ROWS = 2048
EXPERT_BLOCK = 512
EXPERT_SUB = 512
ROW_ALIGN = 8
_MAX_UNITS = max(TOKEN_TILE, EXPERT_BLOCK) // ROW_ALIGN
_UNIT_BITS = _MAX_UNITS.bit_length()
SORT_CHUNK = 256
UNSORT_CHUNK = 256
VMEM_LIMIT = 56 * 1024 * 1024

_ARB = lambda n: pltpu.CompilerParams(dimension_semantics=("arbitrary",) * n,
                                      vmem_limit_bytes=VMEM_LIMIT)


def _bdot(a, b):
    return jnp.dot(a, b, preferred_element_type=F32)


def _sigmoid(x):
    return 0.5 * jnp.tanh(0.5 * x) + 0.5


def _n_strided_groups():
    return sum(1 for _, r in ATTN_GROUPS if r > 1)


def _split3(x):
    hi = x.astype(BF16)
    r1 = x - hi.astype(F32)
    mid = r1.astype(BF16)
    lo = (r1 - mid.astype(F32)).astype(BF16)
    return hi, mid, lo


def _dot_f32(a, b):
    a0, a1, a2 = _split3(a)
    b0, b1, b2 = _split3(b)
    small = _bdot(a0, b2) + _bdot(a1, b1) + _bdot(a2, b0)
    mid = _bdot(a0, b1) + _bdot(a1, b0)
    return (small + mid) + _bdot(a0, b0)


def _mod_kernel(c_ref, w_ref, b_ref, o_ref):
    c = c_ref[...]
    s = c * jax.nn.sigmoid(c)
    o_ref[0] = _dot_f32(s, w_ref[0]) + b_ref[0]


def _modulation(c, w_ada, b_ada):
    depth, d, six_d = w_ada.shape
    b = c.shape[0]
    n_col = six_d // d
    return pl.pallas_call(
        _mod_kernel,
        grid=(depth, n_col),
        in_specs=[pl.BlockSpec((b, d), lambda l, j: (0, 0)),
                  pl.BlockSpec((1, d, d), lambda l, j: (l, 0, j)),
                  pl.BlockSpec((1, 1, d), lambda l, j: (l, 0, j))],
        out_specs=pl.BlockSpec((1, b, d), lambda l, j: (l, 0, j)),
        out_shape=jax.ShapeDtypeStruct((depth, b, six_d), F32),
        compiler_params=_ARB(2),
        name="adaln_mod",
    )(c, w_ada, b_ada.reshape(depth, 1, six_d))


def _rope_kernel(pos_ref, freq_ref, sign_ref, cos_ref, sin_ref):
    ang = pos_ref[...].astype(F32) * freq_ref[...]
    cos_ref[...] = jnp.cos(ang)
    sin_ref[...] = jnp.sin(ang) * sign_ref[...]


def _rope_tables(positions):
    n = positions.size
    half = HEAD_DIM // 2
    inv_freq = ROPE_THETA ** (-jnp.arange(half, dtype=F32) / half)
    freq = jnp.tile(inv_freq, 4).reshape(1, 128)
    sign = jnp.tile(jnp.concatenate([-jnp.ones((half,), F32), jnp.ones((half,), F32)]), 2)
    sign = sign.reshape(1, 128)
    tile = 1024
    return pl.pallas_call(
        _rope_kernel,
        grid=(n // tile,),
        in_specs=[pl.BlockSpec((tile, 1), lambda i: (i, 0)),
                  pl.BlockSpec((1, 128), lambda i: (0, 0)),
                  pl.BlockSpec((1, 128), lambda i: (0, 0))],
        out_specs=[pl.BlockSpec((tile, 128), lambda i: (i, 0))] * 2,
        out_shape=[jax.ShapeDtypeStruct((n, 128), F32)] * 2,
        compiler_params=_ARB(1),
        name="rope_tables",
    )(positions.reshape(n, 1), freq, sign)


def _modulated_norm(x, g, sc, sh):
    ms = jnp.mean(x * x, axis=-1, keepdims=True)
    y = x * lax.rsqrt(ms + RMS_EPS) * g
    return y * (1.0 + sc) + sh


def _inproj_kernel(h_ref, sc_ref, sh_ref, g_ref, w_ref, cos_ref, sin_ref,
                   ws_ref, bs_ref, lng_ref, lnb_ref, qkv0_ref, qkv1_ref, qkv2_ref, gm_ref,
                   slab_ref):
    tm = h_ref.shape[0]
    a = _modulated_norm(h_ref[...], g_ref[...], sc_ref[0], sh_ref[0]).astype(BF16)
    cos = cos_ref[...]
    sin = sin_ref[...]
    lane = lax.broadcasted_iota(I32, (tm, 128), 1)
    first_half = (lane % HEAD_DIM) < (HEAD_DIM // 2)

    def rope(x):
        swapped = jnp.where(first_half, pltpu.roll(x, 96, 1), pltpu.roll(x, 32, 1))
        return x * cos + swapped * sin

    slab = 0
    for g, out_ref in enumerate((qkv0_ref, qkv1_ref, qkv2_ref)):
        r = ATTN_GROUPS[g][1]
        for part in range(3):
            c0 = part * ATTN_W + g * GROUP_W
            y = _bdot(a, w_ref[:, c0:c0 + GROUP_W])
            halves = [y[:, :128], y[:, 128:]]
            if part < 2:
                halves = [rope(x) for x in halves]
            if part == 0:
                halves = [x * (HEAD_DIM ** -0.5) for x in halves]
            for c, x in enumerate(halves):
                o0 = part * GROUP_W + c * 128
                if r == 1:
                    out_ref[0, 0, :, o0:o0 + 128] = x.astype(BF16)
                else:
                    slab_ref[slab] = x
                    for j in range(r):
                        rows = slab_ref[slab, pl.ds(j, tm // r, stride=r), :]
                        out_ref[0, j, :, o0:o0 + 128] = rows.astype(BF16)
                    slab += 1

    c0 = 3 * ATTN_W
    u = jax.nn.gelu(_bdot(a, w_ref[:, c0:c0 + GMLP_W]))
    v = jax.nn.gelu(_bdot(a, w_ref[:, c0 + GMLP_W:c0 + 2 * GMLP_W]))
    mu = jnp.mean(v, axis=-1, keepdims=True)
    var = jnp.mean(jnp.square(v - mu), axis=-1, keepdims=True)
    v = ((v - mu) * lax.rsqrt(var + LN_EPS) * lng_ref[...] + lnb_ref[...]).astype(BF16)
    row = lax.broadcasted_iota(I32, (GMLP_CHUNK, GMLP_CHUNK), 0)
    col = lax.broadcasted_iota(I32, (GMLP_CHUNK, GMLP_CHUNK), 1)
    tril = col <= row
    chunks = [slice(c * GMLP_CHUNK, (c + 1) * GMLP_CHUNK) for c in range(tm // GMLP_CHUNK)]
    for g in range(GMLP_GROUPS):
        wsg = jnp.where(tril, ws_ref[g], 0.0).astype(BF16)
        cs = slice(g * 128, (g + 1) * 128)
        mixed = _bdot(wsg, jnp.concatenate([v[rs, cs] for rs in chunks], axis=1))
        mixed = mixed + bs_ref[:, g:g + 1]
        for c, rs in enumerate(chunks):
            gm_ref[rs, cs] = (u[rs, cs] * mixed[:, c * 128:(c + 1) * 128]).astype(BF16)


def _inproj(h, sc, sh, g, w, cos, sin, w_s, b_s_t, ln_g, ln_b, seq):
    n, d = h.shape
    tm = min(PROJ_TILE, seq)
    per_b = seq // tm
    batch = n // seq
    wn = w.shape[1]
    row = lambda i: (i, 0)
    bat = lambda i: (i // per_b, 0, 0)
    full2 = lambda i: (0, 0)
    res = lambda i: (i // per_b, 0, i % per_b, 0)
    qkv_specs = [pl.BlockSpec((1, r, tm // r, 3 * GROUP_W), res) for _, r in ATTN_GROUPS]
    qkv_shapes = [jax.ShapeDtypeStruct((batch, r, seq // r, 3 * GROUP_W), BF16)
                  for _, r in ATTN_GROUPS]
    return pl.pallas_call(
        _inproj_kernel,
        grid=(n // tm,),
        in_specs=[pl.BlockSpec((tm, d), row),
                  pl.BlockSpec((1, 1, d), bat),
                  pl.BlockSpec((1, 1, d), bat),
                  pl.BlockSpec((1, d), full2),
                  pl.BlockSpec((d, wn), full2),
                  pl.BlockSpec((tm, 128), row),
                  pl.BlockSpec((tm, 128), row),
                  pl.BlockSpec((GMLP_GROUPS, GMLP_CHUNK, GMLP_CHUNK), lambda i: (0, 0, 0)),
                  pl.BlockSpec((GMLP_CHUNK, GMLP_GROUPS), full2),
                  pl.BlockSpec((1, GMLP_W), full2),
                  pl.BlockSpec((1, GMLP_W), full2)],
        out_specs=qkv_specs + [pl.BlockSpec((tm, GMLP_W), row)],
        out_shape=qkv_shapes + [jax.ShapeDtypeStruct((n, GMLP_W), BF16)],
        scratch_shapes=[pltpu.VMEM((_n_strided_groups() * 3 * (GROUP_W // 128), tm, 128), F32)],
        compiler_params=_ARB(1),
        name="inproj_rope_gmlp",
    )(h, sc, sh, g, w, cos, sin, w_s, b_s_t, ln_g, ln_b)


def _attn_kernel(q_ref, k_ref, v_ref, kp_ref, vp_ref, o_ref, lse_ref):
    q_ref, k_ref, v_ref, kp_ref, vp_ref, o_ref, lse_ref = (
        x.at[0] for x in (q_ref, k_ref, v_ref, kp_ref, vp_ref, o_ref, lse_ref))
    qb = q_ref.shape[1]
    first_key = jnp.where(pl.program_id(2) == 0, WIN_BLOCK, 0)
    qi = lax.broadcasted_iota(I32, (WIN_BLOCK, 2 * WIN_BLOCK), 0)
    kj = lax.broadcasted_iota(I32, (WIN_BLOCK, 2 * WIN_BLOCK), 1)
    band = (kj >= qi) & (kj <= qi + WIN_BLOCK)
    low_lanes = lax.broadcasted_iota(I32, (WIN_BLOCK, 128), 1) < HEAD_DIM
    ones = jnp.ones((2 * WIN_BLOCK, 128), BF16)
    for res, i in [(res, i) for res in range(q_ref.shape[0]) for i in range(qb // WIN_BLOCK)]:
        rs = slice(i * WIN_BLOCK, (i + 1) * WIN_BLOCK)
        q = q_ref[res, rs, :]
        if i == 0:
            k_prev, v_prev = kp_ref[res], vp_ref[res]
            valid = band & (kj >= first_key)
        else:
            ps = slice((i - 1) * WIN_BLOCK, i * WIN_BLOCK)
            k_prev, v_prev = k_ref[res, ps, :], v_ref[res, ps, :]
            valid = band
        k_cat = jnp.concatenate([k_prev, k_ref[res, rs, :]], axis=0)
        v_cat = jnp.concatenate([v_prev, v_ref[res, rs, :]], axis=0)
        outs, lses = [], []
        for pair in range(GROUP_W // 128):
            cs = slice(pair * 128, (pair + 1) * 128)
            q_pair = q[:, cs].astype(F32)
            k_pair = k_cat[:, cs]
            v_ones = jnp.concatenate([v_cat[:, cs], ones], axis=1)
            o_half, l_half = [], []
            for head_lanes in (low_lanes, jnp.logical_not(low_lanes)):
                q_head = jnp.where(head_lanes, q_pair, 0.0).astype(BF16)
                s = lax.dot_general(q_head, k_pair, (((1,), (1,)), ((), ())),
                                    preferred_element_type=F32)
                s = jnp.where(valid, s, NEG_INF)
                m = jnp.max(s, axis=-1, keepdims=True)
                p = jnp.exp(s - m).astype(BF16)
                pv = _bdot(p, v_ones)
                den = pv[:, 128:129]
                o_half.append(pv[:, :128] / den)
                l_half.append(jnp.broadcast_to(m + jnp.log(den), (WIN_BLOCK, 128)))
            outs.append(jnp.where(low_lanes, o_half[0], o_half[1]))
            lses.append(jnp.where(low_lanes, l_half[0], l_half[1]))
        o_ref[res, rs, :] = jnp.concatenate(outs, axis=1).astype(o_ref.dtype)
        lse_ref[res, rs, :] = jnp.concatenate(lses, axis=1)


def _attention_group(qkv, g):
    batch, r, sub_len, _ = qkv.shape
    qb = min(ATTN_STEP_ROWS, sub_len)
    n_res = min(ATTN_STEP_ROWS // qb, r)
    n_tiles = sub_len // qb
    per_tile = qb // WIN_BLOCK

    def cur(part):
        return pl.BlockSpec((1, n_res, qb, GROUP_W), lambda b, j, n: (b, j, n, part))

    def prev(part):
        return pl.BlockSpec((1, n_res, WIN_BLOCK, GROUP_W),
                            lambda b, j, n: (b, j, jnp.maximum(n * per_tile - 1, 0), part))

    out_spec = pl.BlockSpec((1, n_res, qb, GROUP_W), lambda b, j, n: (b, j, n, 0))
    return pl.pallas_call(
        _attn_kernel,
        grid=(batch, r // n_res, n_tiles),
        in_specs=[cur(0), cur(1), cur(2), prev(1), prev(2)],
        out_specs=[out_spec, out_spec],
        out_shape=[jax.ShapeDtypeStruct((batch, r, sub_len, GROUP_W), BF16),
                   jax.ShapeDtypeStruct((batch, r, sub_len, GROUP_W), F32)],
        compiler_params=_ARB(3),
        name=f"dilated_attn_g{g}",
    )(qkv, qkv, qkv, qkv, qkv)


def _split_bf16(x):
    hi = x.astype(BF16)
    lo = (x - hi.astype(F32)).astype(BF16)
    return hi, lo


def _merge_kernel(h_ref, sc1_ref, sh1_ref, g1_ref, sc2_ref, sh2_ref, n1_ref, n2_ref,
                  wg_ref, wpa_ref, wpg_ref, wo_ref,
                  o0_ref, o1_ref, o2_ref, l0_ref, l1_ref, l2_ref, gm_ref,
                  wr_ref, br_ref,
                  hn_ref, f_ref, slot_ref, gate_ref, cnt_ref, slab_ref):
    tm, d = h_ref.shape
    slabs = iter(range(slab_ref.shape[0]))

    def token_major(ref):
        r = ref.shape[1]
        if r == 1:
            return ref[0, 0].astype(F32)
        halves = []
        for c in range(GROUP_W // 128):
            slab = next(slabs)
            for j in range(r):
                slab_ref[slab, pl.ds(j, tm // r, stride=r), :] = (
                    ref[0, j, :, c * 128:(c + 1) * 128].astype(F32))
            halves.append(slab_ref[slab])
        return jnp.concatenate(halves, axis=1)

    lses = [token_major(x) for x in (l0_ref, l1_ref, l2_ref)]
    outs = [token_major(x) for x in (o0_ref, o1_ref, o2_ref)]

    h = h_ref[...]
    a = _modulated_norm(h, n1_ref[...], sc1_ref[0], sh1_ref[0]).astype(BF16)
    gates = _sigmoid(_bdot(a, wg_ref[...]))

    m = jnp.maximum(jnp.maximum(lses[0], lses[1]), lses[2])
    es = [jnp.exp(l - m) for l in lses]
    den = es[0] + es[1] + es[2]
    attn = (es[0] * outs[0] + es[1] * outs[1] + es[2] * outs[2]) / den

    merged = (gates[:, :d] * _bdot(attn.astype(BF16), wpa_ref[...])
              + gates[:, d:] * _bdot(gm_ref[...], wpg_ref[...]))
    hn = h + g1_ref[0] * _bdot(merged.astype(BF16), wo_ref[...])
    hn_ref[...] = hn

    f = _modulated_norm(hn, n2_ref[...], sc2_ref[0], sh2_ref[0])
    f_ref[...] = f.astype(BF16)
    _route_tile(f, wr_ref, br_ref, slot_ref, gate_ref, cnt_ref)


def _route_tile(f, wr_ref, br_ref, slot_ref, gate_ref, cnt_ref):
    tm = f.shape[0]
    f_hi, f_lo = _split_bf16(f)
    w_hi, w_lo = _split_bf16(wr_ref[...])
    nt = (((1,), (1,)), ((), ()))
    logits = (lax.dot_general(w_hi, f_hi, nt, preferred_element_type=F32)
              + lax.dot_general(w_hi, f_lo, nt, preferred_element_type=F32)
              + lax.dot_general(w_lo, f_hi, nt, preferred_element_type=F32)) + br_ref[...]

    n_e = logits.shape[0]
    erow = lax.broadcasted_iota(I32, (n_e, tm), 0).astype(F32)
    vals, idxs, sels = [], [], []
    cur = logits
    for _ in range(TOP_K):
        mx = jnp.max(cur, axis=0, keepdims=True)
        ix = jnp.min(jnp.where(cur == mx, erow, float(n_e)), axis=0, keepdims=True)
        sel = erow == ix
        vals.append(mx)
        idxs.append(ix)
        sels.append(sel)
        cur = jnp.where(sel, -jnp.inf, cur)
    exps = [jnp.exp(v - vals[0]) for v in vals]
    tot = exps[0] + exps[1] + exps[2] + exps[3]
    gate_ref[...] = jnp.concatenate([e / tot for e in exps], axis=0)

    onehot = (sels[0] | sels[1] | sels[2] | sels[3])
    oh = jnp.where(onehot, 1.0, 0.0)
    t_r = lax.broadcasted_iota(I32, (tm, tm), 0)
    t_c = lax.broadcasted_iota(I32, (tm, tm), 1)
    upper = jnp.where(t_r < t_c, 1.0, 0.0).astype(BF16)
    prefix = _bdot(oh.astype(BF16), upper)
    cnt = jnp.sum(oh, axis=1, keepdims=True)
    cnt_al = jnp.floor((cnt + (ROW_ALIGN - 1)) * (1.0 / ROW_ALIGN)) * ROW_ALIGN
    e_r = lax.broadcasted_iota(I32, (n_e, n_e), 0)
    e_c = lax.broadcasted_iota(I32, (n_e, n_e), 1)
    lower = jnp.where(e_c < e_r, 1.0, 0.0).astype(BF16)
    start = _bdot(lower, jnp.broadcast_to(cnt_al, (n_e, 128)).astype(BF16))[:, 0:1]
    local = prefix + start
    slots = [jnp.sum(jnp.where(s, local, 0.0), axis=0, keepdims=True) for s in sels]
    slot_ref[...] = jnp.concatenate(slots, axis=0).astype(I32)
    cnt_ref[...] = jnp.broadcast_to(cnt, cnt_ref.shape).astype(I32)


def _merge_route(h, mods, n1, n2, wg, wpa, wpg, wo, outs, lses, gm, wr, br, seq):
    n, d = h.shape
    tm = TOKEN_TILE
    per_b = seq // tm
    n_tiles = n // tm
    row = lambda i: (i, 0)
    bat = lambda i: (i // per_b, 0, 0)
    full2 = lambda i: (0, 0)
    col = lambda i: (0, i)
    sc1, sh1, g1, sc2, sh2 = mods
    res = lambda i: (i // per_b, 0, i % per_b, 0)
    attn_specs = [pl.BlockSpec((1, r, tm // r, GROUP_W), res) for _, r in ATTN_GROUPS]
    in_specs = ([pl.BlockSpec((tm, d), row)]
                + [pl.BlockSpec((1, 1, d), bat)] * 5
                + [pl.BlockSpec((1, d), full2)] * 2
                + [pl.BlockSpec(wg.shape, full2), pl.BlockSpec(wpa.shape, full2),
                   pl.BlockSpec(wpg.shape, full2), pl.BlockSpec(wo.shape, full2)]
                + attn_specs * 2
                + [pl.BlockSpec((tm, GMLP_W), row)]
                + [pl.BlockSpec(wr.shape, full2), pl.BlockSpec(br.shape, full2)])
    out_specs = [pl.BlockSpec((tm, d), row), pl.BlockSpec((tm, d), row),
                 pl.BlockSpec((TOP_K, tm), col), pl.BlockSpec((TOP_K, tm), col),
                 pl.BlockSpec((N_EXPERTS, 128), row)]
    out_shape = [jax.ShapeDtypeStruct((n, d), F32), jax.ShapeDtypeStruct((n, d), BF16),
                 jax.ShapeDtypeStruct((TOP_K, n), I32), jax.ShapeDtypeStruct((TOP_K, n), F32),
                 jax.ShapeDtypeStruct((n_tiles * N_EXPERTS, 128), I32)]
    return pl.pallas_call(
        _merge_kernel,
        grid=(n_tiles,),
        in_specs=in_specs,
        out_specs=out_specs,
        out_shape=out_shape,
        scratch_shapes=[pltpu.VMEM((_n_strided_groups() * 2 * (GROUP_W // 128), tm, 128), F32)],
        compiler_params=_ARB(1),
        name="merge_proj_route",
    )(h, sc1, sh1, g1, sc2, sh2, n1, n2, wg, wpa, wpg, wo, *outs, *lses, gm, wr, br)


def _row_copy(src, dst, sem):
    return pltpu.make_async_copy(src, dst, sem)


def _start_runs(tile, off_ref, start_ref, pieces_ref, copy):
    def per_expert(e, total):
        base = tile * N_EXPERTS + e
        local0, global0, n = start_ref[base], off_ref[base], pieces_ref[base]
        for bit in reversed(range(_UNIT_BITS)):
            rows = ROW_ALIGN << bit
            done = lax.shift_left(lax.shift_right_logical(n, bit + 1), bit + 1) * ROW_ALIGN

            @pl.when((lax.shift_right_logical(n, bit) & 1) == 1)
            def _():
                copy(pl.multiple_of(local0 + done, ROW_ALIGN),
                     pl.multiple_of(global0 + done, ROW_ALIGN), rows).start()

        return total + n

    return lax.fori_loop(0, N_EXPERTS, per_expert, 0)


def _wait_runs(total, copy):
    top = _UNIT_BITS - 1
    lax.fori_loop(0, lax.shift_right_logical(total, top),
                  lambda i, c: (copy(0, 0, ROW_ALIGN << top).wait(), c)[1], 0)
    for bit in reversed(range(top)):
        @pl.when((lax.shift_right_logical(total, bit) & 1) == 1)
        def _():
            copy(0, 0, ROW_ALIGN << bit).wait()


def _slot_hits(slot_ref, chunk, rows=SORT_CHUNK):
    tm = slot_ref.shape[1]
    s = lax.broadcasted_iota(I32, (rows, tm), 0) + chunk * rows
    return [slot_ref[k:k + 1, :] == s for k in range(TOP_K)]


def _dispatch_kernel(zb_ref, nz_ref, tail_off_ref, tail_pieces_ref, off_ref, start_ref, pieces_ref,
                     slot_ref, f_ref, buf_ref, sorted_ref, zero_ref, pending_ref, sems, zsem):
    step = pl.program_id(0)
    last_step = pl.num_programs(0) - 1
    buf_slot = lax.rem(step, 2)
    sorted_now = sorted_ref.at[buf_slot]

    def zero_block(j):
        start = pl.multiple_of(zb_ref[j] * EXPERT_BLOCK, EXPERT_BLOCK)
        return _row_copy(zero_ref, buf_ref.at[pl.ds(start, EXPERT_BLOCK)], zsem)

    def zero_piece(local_row, global_row, rows):
        del local_row
        return _row_copy(zero_ref.at[pl.ds(0, rows)], buf_ref.at[pl.ds(global_row, rows)], zsem)

    @pl.when(step == 0)
    def _():
        zero_ref[...] = jnp.zeros_like(zero_ref)
        lax.fori_loop(0, nz_ref[0], lambda j, c: (zero_block(j).start(), c)[1], 0)
        _start_runs(0, tail_off_ref, tail_off_ref, tail_pieces_ref, zero_piece)

    f = f_ref[...]
    for c in range(sorted_now.shape[0] // SORT_CHUNK):
        hits = _slot_hits(slot_ref, c)
        onehot = jnp.where(hits[0] | hits[1] | hits[2] | hits[3], 1.0, 0.0).astype(BF16)
        sorted_now[c * SORT_CHUNK:(c + 1) * SORT_CHUNK, :] = _bdot(onehot, f)

    def copy_from(slot):
        def copy(local_row, global_row, rows):
            return _row_copy(sorted_ref.at[slot, pl.ds(local_row, rows)],
                             buf_ref.at[pl.ds(global_row, rows)], sems.at[slot])
        return copy

    total = _start_runs(step, off_ref, start_ref, pieces_ref, copy_from(buf_slot))

    @pl.when(step > 0)
    def _():
        _wait_runs(pending_ref[0], copy_from(1 - buf_slot))

    pending_ref[0] = total

    @pl.when(step == last_step)
    def _():
        _wait_runs(total, copy_from(buf_slot))
        lax.fori_loop(0, nz_ref[0], lambda j, c: (zero_block(j).wait(), c)[1], 0)
        tail_total = lax.fori_loop(0, N_EXPERTS, lambda e, t: t + tail_pieces_ref[e], 0)
        _wait_runs(tail_total, zero_piece)


def _dispatch(plan, slots, f, n_blocks):
    n, d = f.shape
    tm = TOKEN_TILE
    return pl.pallas_call(
        _dispatch_kernel,
        grid_spec=pltpu.PrefetchScalarGridSpec(
            num_scalar_prefetch=7, grid=(n // tm,),
            in_specs=[pl.BlockSpec((TOP_K, tm), lambda i, *_: (0, i)),
                      pl.BlockSpec((tm, d), lambda i, *_: (i, 0))],
            out_specs=pl.BlockSpec(memory_space=pl.ANY),
            scratch_shapes=[pltpu.VMEM((2, _local_rows(tm), d), F32),
                            pltpu.VMEM((EXPERT_BLOCK, d), F32),
                            pltpu.SMEM((1,), I32),
                            pltpu.SemaphoreType.DMA((2,)), pltpu.SemaphoreType.DMA(())]),
        out_shape=jax.ShapeDtypeStruct((n_blocks * EXPERT_BLOCK, d), F32),
        compiler_params=_ARB(1),
        name="dispatch_rows",
    )(plan["zero_blocks"], plan["n_zero"], plan["tail_off"], plan["tail_pieces"],
      plan["run_off"], plan["run_start"], plan["run_pieces"], slots, f)


def _expert_kernel(be_ref, first_ref, rows_ref, x_ref, w1_ref, b1_ref, w2_ref, b2_ref, o_ref,
                   w1b_ref, w2p_ref, slab_ref):
    del be_ref
    step = pl.program_id(0)
    n_rows = rows_ref[step]
    live = n_rows > 0
    ff, d = w2p_ref.shape
    half = 64

    @pl.when(live & (first_ref[step] == 1))
    def _():
        w1b_ref[...] = w1_ref[0, 0].astype(BF16)
        for c in range(d // 128):
            cs = slice(c * 128, (c + 1) * 128)
            for k in range(ff // 128):
                r0 = k * 128
                slab_ref[pl.ds(r0, half, stride=2), :] = w2_ref[0, 0, r0:r0 + half, cs]
                slab_ref[pl.ds(r0 + 1, half, stride=2), :] = w2_ref[0, 0, r0 + half:r0 + 128, cs]
            w2p_ref[:, cs] = slab_ref[...].astype(BF16)

    def ffn(rs):
        x = x_ref[rs, :].astype(BF16)
        hb = _bdot(x, w1b_ref[...]) + b1_ref[0, 0]
        lane = lax.broadcasted_iota(I32, (x.shape[0], 128), 1)
        even = (lane & 1) == 0
        glu, lin = [], []
        for k in range(ff // 128):
            a = hb[:, 256 * k:256 * k + 128]
            b = hb[:, 256 * k + 128:256 * k + 256]
            glu.append(jnp.where(even, a, pltpu.roll(b, 1, 1)))
            lin.append(jnp.where(even, pltpu.roll(a, 127, 1), b))
        x_glu = jnp.minimum(jnp.concatenate(glu, axis=1), SWIGLU_LIMIT)
        x_lin = jnp.clip(jnp.concatenate(lin, axis=1), -SWIGLU_LIMIT, SWIGLU_LIMIT)
        act = x_glu * _sigmoid(SWIGLU_ALPHA * x_glu) * (x_lin + 1.0)
        o_ref[rs, :] = _bdot(act.astype(BF16), w2p_ref[...]) + b2_ref[0, 0]

    for r0 in range(0, EXPERT_BLOCK, EXPERT_SUB):
        rs = slice(r0, r0 + EXPERT_SUB)
        pl.when(n_rows > r0)(functools.partial(ffn, rs))

        @pl.when(n_rows <= r0)
        def _():
            o_ref[rs, :] = jnp.zeros((EXPERT_SUB, d), F32)


def _experts(layer, block_e, first, block_rows, buf, w1, b1, w2, b2, n_blocks):
    d = buf.shape[1]
    ff = w2.shape[2]
    xmap = lambda i, be, fi, rw: (jnp.where(rw[i] > 0, i, 0), 0)
    wmap = lambda i, be, fi, rw: (layer, be[i], 0, 0)
    return pl.pallas_call(
        _expert_kernel,
        grid_spec=pltpu.PrefetchScalarGridSpec(
            num_scalar_prefetch=3, grid=(n_blocks,),
            in_specs=[pl.BlockSpec((EXPERT_BLOCK, d), xmap),
                      pl.BlockSpec((1, 1, d, 2 * ff), wmap),
                      pl.BlockSpec((1, 1, 1, 2 * ff), wmap),
                      pl.BlockSpec((1, 1, ff, d), wmap),
                      pl.BlockSpec((1, 1, 1, d), wmap)],
            out_specs=pl.BlockSpec((EXPERT_BLOCK, d), lambda i, be, fi, rw: (i, 0)),
            scratch_shapes=[pltpu.VMEM((d, 2 * ff), BF16), pltpu.VMEM((ff, d), BF16),
                            pltpu.VMEM((ff, 128), F32)]),
        out_shape=jax.ShapeDtypeStruct((n_blocks * EXPERT_BLOCK, d), F32),
        compiler_params=_ARB(1),
        name="expert_ffn",
    )(block_e, first, block_rows, buf, w1, b1, w2, b2)


def _combine_kernel(off_ref, start_ref, pieces_ref, slot_ref, gate_ref, h_ref, g2_ref, fin_ref,
                    obuf_ref, out_ref, rows_ref, sems, *, final_norm):
    step = pl.program_id(0)
    buf_slot = lax.rem(step, 2)
    rows_now = rows_ref.at[buf_slot]

    def copy_into(slot):
        def copy(local_row, global_row, rows):
            return _row_copy(obuf_ref.at[pl.ds(global_row, rows)],
                             rows_ref.at[slot, pl.ds(local_row, rows)], sems.at[slot])
        return copy

    @pl.when(step == 0)
    def _():
        rows_ref[...] = jnp.zeros_like(rows_ref)
        _start_runs(0, off_ref, start_ref, pieces_ref, copy_into(0))

    @pl.when(step + 1 < pl.num_programs(0))
    def _():
        _start_runs(step + 1, off_ref, start_ref, pieces_ref, copy_into(1 - buf_slot))

    total = lax.fori_loop(0, N_EXPERTS,
                          lambda e, t: t + pieces_ref[step * N_EXPERTS + e], 0)
    _wait_runs(total, copy_into(buf_slot))

    y = jnp.zeros(h_ref.shape, F32)
    for c in range(rows_now.shape[0] // UNSORT_CHUNK):
        hits = _slot_hits(slot_ref, c, UNSORT_CHUNK)
        weight = jnp.where(hits[0], gate_ref[0:1, :], 0.0)
        for k in range(1, TOP_K):
            weight = weight + jnp.where(hits[k], gate_ref[k:k + 1, :], 0.0)
        rows = rows_now[c * UNSORT_CHUNK:(c + 1) * UNSORT_CHUNK, :].astype(BF16)
        y = y + lax.dot_general(weight.astype(BF16), rows, (((0,), (0,)), ((), ())),
                                preferred_element_type=F32)
    hn = h_ref[...] + g2_ref[0] * y
    if final_norm:
        ms = jnp.mean(hn * hn, axis=-1, keepdims=True)
        hn = hn * lax.rsqrt(ms + RMS_EPS) * fin_ref[...]
    out_ref[...] = hn


def _combine(plan, slots, gate, h, g2, final_g, obuf, seq, final_norm):
    n, d = h.shape
    tm = TOKEN_TILE
    per_b = seq // tm
    return pl.pallas_call(
        functools.partial(_combine_kernel, final_norm=final_norm),
        grid_spec=pltpu.PrefetchScalarGridSpec(
            num_scalar_prefetch=3, grid=(n // tm,),
            in_specs=[pl.BlockSpec((TOP_K, tm), lambda i, *_: (0, i)),
                      pl.BlockSpec((TOP_K, tm), lambda i, *_: (0, i)),
                      pl.BlockSpec((tm, d), lambda i, *_: (i, 0)),
                      pl.BlockSpec((1, 1, d), lambda i, *_: (i // per_b, 0, 0)),
                      pl.BlockSpec((1, d), lambda i, *_: (0, 0)),
                      pl.BlockSpec(memory_space=pl.ANY)],
            out_specs=pl.BlockSpec((tm, d), lambda i, *_: (i, 0)),
            scratch_shapes=[pltpu.VMEM((2, _local_rows(tm), d), F32),
                            pltpu.SemaphoreType.DMA((2,))]),
        out_shape=jax.ShapeDtypeStruct((n, d), F32),
        compiler_params=_ARB(1),
        name="combine_rows",
    )(plan["run_off"], plan["run_start"], plan["run_pieces"], slots, gate, h, g2, final_g, obuf)


def _local_rows(tm):
    rows = tm * TOP_K + N_EXPERTS * (ROW_ALIGN - 1)
    return -(-rows // SORT_CHUNK) * SORT_CHUNK


def _num_blocks(n):
    rows = n * TOP_K + (n // TOKEN_TILE) * N_EXPERTS * (ROW_ALIGN - 1)
    return -(-rows // EXPERT_BLOCK) + N_EXPERTS


def _slot_plan(tile_counts, n):
    blk = EXPERT_BLOCK
    n_blocks = _num_blocks(n)
    cnt = (tile_counts + ROW_ALIGN - 1) // ROW_ALIGN * ROW_ALIGN
    total = jnp.sum(cnt, axis=0)
    padded = (total + blk - 1) // blk * blk
    pad_end = jnp.cumsum(padded)
    pad_start = pad_end - padded
    run_off = pad_start[None, :] + jnp.cumsum(cnt, axis=0) - cnt
    run_start = jnp.cumsum(cnt, axis=1) - cnt
    n_valid = pad_end[-1] // blk
    block_start = jnp.arange(n_blocks, dtype=I32) * blk
    block_e = jnp.sum(pad_end[None, :] <= block_start[:, None], axis=1)
    last_e = jnp.sum(pad_end <= (n_valid - 1) * blk)
    block_e = jnp.where(jnp.arange(n_blocks) < n_valid, block_e, last_e).astype(I32)
    first = jnp.concatenate([jnp.ones((1,), I32),
                             (block_e[1:] != block_e[:-1]).astype(I32)])
    hit = jnp.arange(N_EXPERTS)[None, :] == block_e[:, None]
    data_end = jnp.sum(jnp.where(hit, pad_start + total, 0), axis=1)
    block_rows = jnp.where(jnp.arange(n_blocks) < n_valid,
                           jnp.clip(data_end - block_start, 0, blk), 0).astype(I32)
    flat = lambda a: a.reshape(-1).astype(I32)
    zero_blocks = n_valid + jnp.arange(n_blocks - (n * TOP_K) // blk)
    return dict(block_e=block_e, first=first, block_rows=block_rows,
                zero_blocks=flat(jnp.minimum(zero_blocks, n_blocks - 1)),
                n_zero=(n_blocks - n_valid).reshape(1).astype(I32),
                tail_off=flat(pad_start + total), tail_pieces=flat((padded - total) // ROW_ALIGN),
                run_off=flat(run_off), run_start=flat(run_start),
                run_pieces=flat(cnt // ROW_ALIGN))


def kernel(x, c, positions, w_ada, b_ada, norm1_g, w_in, w_s, b_s, ln_g, ln_b, w_pa, w_pg,
           w_o, norm2_g, w_router, b_router, w1, b1, w2, b2, final_g):
    batch, seq, d = x.shape
    depth = w_ada.shape[0]
    n = batch * seq
    n_blocks = _num_blocks(n)

    mod = _modulation(c, w_ada, b_ada)
    cos, sin = _rope_tables(positions)

    n_mix = 3 * ATTN_W + 2 * GMLP_W
    b1r = b1.reshape(depth, N_EXPERTS, 1, -1)
    b2r = b2.reshape(depth, N_EXPERTS, 1, -1)

    h = x.reshape(n, d)
    for l in range(depth):
        ml = mod[l].reshape(batch, 6, 1, d)
        sh1, sc1, g1, sh2, sc2, g2 = [ml[:, i] for i in range(6)]
        w_mix = w_in[l, :, :n_mix].astype(BF16)
        w_gate = w_in[l, :, n_mix:].astype(BF16)

        *qkvs, gm = _inproj(h, sc1, sh1, norm1_g[l].reshape(1, d), w_mix, cos, sin,
                            w_s[l], b_s[l].T, ln_g[l].reshape(1, -1), ln_b[l].reshape(1, -1), seq)
        outs, lses = zip(*[_attention_group(qkvs[g], g) for g in range(N_GROUPS)])

        h, f, slots, gate, counts = _merge_route(
            h, (sc1, sh1, g1, sc2, sh2), norm1_g[l].reshape(1, d), norm2_g[l].reshape(1, d),
            w_gate, w_pa[l].astype(BF16), w_pg[l].astype(BF16), w_o[l].astype(BF16),
            outs, lses, gm, w_router[l].T, b_router[l].reshape(-1, 1), seq)

        plan = _slot_plan(counts[:, 0].reshape(-1, N_EXPERTS), n)
        buf = _dispatch(plan, slots, f, n_blocks)
        obuf = _experts(l, plan["block_e"], plan["first"], plan["block_rows"], buf,
                        w1, b1r, w2, b2r, n_blocks)

        h = _combine(plan, slots, gate, h, g2, final_g.reshape(1, d), obuf, seq,
                     final_norm=(l == depth - 1))
    return h.reshape(batch, seq, d)
```

```python
import functools

import jax
import jax.numpy as jnp
from jax import lax
from jax.experimental import pallas as pl
from jax.experimental.pallas import tpu as pltpu

F32 = jnp.float32
BF16 = jnp.bfloat16
I32 = jnp.int32

HEAD_DIM = 64
HEADS_PER_GROUP = 4
GROUP_W = HEADS_PER_GROUP * HEAD_DIM
ATTN_GROUPS = ((128, 1), (512, 4), (2048, 16))
N_GROUPS = len(ATTN_GROUPS)
ATTN_W = N_GROUPS * GROUP_W
WIN_BLOCK = 128
ROPE_THETA = 10000.0
GMLP_CHUNK = 128
GMLP_GROUPS = 4
GMLP_W = GMLP_GROUPS * 128
N_EXPERTS = 32
TOP_K = 4
SWIGLU_ALPHA = 1.702
SWIGLU_LIMIT = 7.0
RMS_EPS = 1e-5
LN_EPS = 1e-5
NEG_INF = -1e30

TOKEN_TILE = 512
PROJ_TILE = 1024
ATTN_STEP_ROWS = 4096
EXPERT_BLOCK = 512
EXPERT_SUB = 512
ROW_ALIGN = 8
_MAX_UNITS = max(TOKEN_TILE, EXPERT_BLOCK) // ROW_ALIGN
_UNIT_BITS = _MAX_UNITS.bit_length()
SORT_CHUNK = 256
UNSORT_CHUNK = 256
VMEM_LIMIT = 56 * 1024 * 1024

def _ARB(n, **kw):
    return pltpu.CompilerParams(dimension_semantics=("arbitrary",) * n,
                                vmem_limit_bytes=VMEM_LIMIT, **kw)


def _bdot(a, b):
    return jnp.dot(a, b, preferred_element_type=F32)


def _sigmoid(x):
    return 0.5 * jnp.tanh(0.5 * x) + 0.5


def _n_strided_groups():
    return sum(1 for _, r in ATTN_GROUPS if r > 1)


def _split3(x):
    hi = x.astype(BF16)
    r1 = x - hi.astype(F32)
    mid = r1.astype(BF16)
    lo = (r1 - mid.astype(F32)).astype(BF16)
    return hi, mid, lo


def _dot_f32(a, b):
    a0, a1, a2 = _split3(a)
    b0, b1, b2 = _split3(b)
    small = _bdot(a0, b2) + _bdot(a1, b1) + _bdot(a2, b0)
    mid = _bdot(a0, b1) + _bdot(a1, b0)
    return (small + mid) + _bdot(a0, b0)


def _mod_kernel(c_ref, w_ref, b_ref, o_ref):
    c = c_ref[...]
    s = c * jax.nn.sigmoid(c)
    o_ref[0] = _dot_f32(s, w_ref[0]) + b_ref[0]


def _modulation(c, w_ada, b_ada):
    depth, d, six_d = w_ada.shape
    b = c.shape[0]
    n_col = six_d // d
    return pl.pallas_call(
        _mod_kernel,
        grid=(depth, n_col),
        in_specs=[pl.BlockSpec((b, d), lambda l, j: (0, 0)),
                  pl.BlockSpec((1, d, d), lambda l, j: (l, 0, j)),
                  pl.BlockSpec((1, 1, d), lambda l, j: (l, 0, j))],
        out_specs=pl.BlockSpec((1, b, d), lambda l, j: (l, 0, j)),
        out_shape=jax.ShapeDtypeStruct((depth, b, six_d), F32),
        compiler_params=_ARB(2),
        name="adaln_mod",
    )(c, w_ada, b_ada.reshape(depth, 1, six_d))


def _rope_kernel(pos_ref, freq_ref, sign_ref, cos_ref, sin_ref):
    ang = pos_ref[...].astype(F32) * freq_ref[...]
    cos_ref[...] = jnp.cos(ang)
    sin_ref[...] = jnp.sin(ang) * sign_ref[...]


def _rope_tables(positions):
    n = positions.size
    half = HEAD_DIM // 2
    inv_freq = ROPE_THETA ** (-jnp.arange(half, dtype=F32) / half)
    freq = jnp.tile(inv_freq, 4).reshape(1, 128)
    sign = jnp.tile(jnp.concatenate([-jnp.ones((half,), F32), jnp.ones((half,), F32)]), 2)
    sign = sign.reshape(1, 128)
    tile = 1024
    return pl.pallas_call(
        _rope_kernel,
        grid=(n // tile,),
        in_specs=[pl.BlockSpec((tile, 1), lambda i: (i, 0)),
                  pl.BlockSpec((1, 128), lambda i: (0, 0)),
                  pl.BlockSpec((1, 128), lambda i: (0, 0))],
        out_specs=[pl.BlockSpec((tile, 128), lambda i: (i, 0))] * 2,
        out_shape=[jax.ShapeDtypeStruct((n, 128), F32)] * 2,
        compiler_params=_ARB(1),
        name="rope_tables",
    )(positions.reshape(n, 1), freq, sign)


def _modulated_norm(x, g, sc, sh):
    ms = jnp.mean(x * x, axis=-1, keepdims=True)
    y = x * lax.rsqrt(ms + RMS_EPS) * g
    return y * (1.0 + sc) + sh


def _inproj_kernel(h_ref, sc_ref, sh_ref, g_ref, w_ref, cos_ref, sin_ref,
                   ws_ref, bs_ref, lng_ref, lnb_ref, qkv0_ref, qkv1_ref, qkv2_ref, gm_ref,
                   slab_ref):
    tm = h_ref.shape[0]
    a = _modulated_norm(h_ref[...], g_ref[...], sc_ref[0], sh_ref[0]).astype(BF16)
    cos = cos_ref[...]
    sin = sin_ref[...]
    lane = lax.broadcasted_iota(I32, (tm, 128), 1)
    first_half = (lane % HEAD_DIM) < (HEAD_DIM // 2)

    def rope(x):
        swapped = jnp.where(first_half, pltpu.roll(x, 96, 1), pltpu.roll(x, 32, 1))
        return x * cos + swapped * sin

    slab = 0
    for g, out_ref in enumerate((qkv0_ref, qkv1_ref, qkv2_ref)):
        r = ATTN_GROUPS[g][1]
        for part in range(3):
            c0 = part * ATTN_W + g * GROUP_W
            y = _bdot(a, w_ref[:, c0:c0 + GROUP_W])
            halves = [y[:, :128], y[:, 128:]]
            if part < 2:
                halves = [rope(x) for x in halves]
            if part == 0:
                halves = [x * (HEAD_DIM ** -0.5) for x in halves]
            for c, x in enumerate(halves):
                o0 = part * GROUP_W + c * 128
                if r == 1:
                    out_ref[0, 0, :, o0:o0 + 128] = x.astype(BF16)
                else:
                    slab_ref[slab] = x
                    for j in range(r):
                        rows = slab_ref[slab, pl.ds(j, tm // r, stride=r), :]
                        out_ref[0, j, :, o0:o0 + 128] = rows.astype(BF16)
                    slab += 1

    c0 = 3 * ATTN_W
    u = jax.nn.gelu(_bdot(a, w_ref[:, c0:c0 + GMLP_W]))
    v = jax.nn.gelu(_bdot(a, w_ref[:, c0 + GMLP_W:c0 + 2 * GMLP_W]))
    mu = jnp.mean(v, axis=-1, keepdims=True)
    var = jnp.mean(jnp.square(v - mu), axis=-1, keepdims=True)
    v = ((v - mu) * lax.rsqrt(var + LN_EPS) * lng_ref[...] + lnb_ref[...]).astype(BF16)
    row = lax.broadcasted_iota(I32, (GMLP_CHUNK, GMLP_CHUNK), 0)
    col = lax.broadcasted_iota(I32, (GMLP_CHUNK, GMLP_CHUNK), 1)
    tril = col <= row
    chunks = [slice(c * GMLP_CHUNK, (c + 1) * GMLP_CHUNK) for c in range(tm // GMLP_CHUNK)]
    for g in range(GMLP_GROUPS):
        wsg = jnp.where(tril, ws_ref[g], 0.0).astype(BF16)
        cs = slice(g * 128, (g + 1) * 128)
        mixed = _bdot(wsg, jnp.concatenate([v[rs, cs] for rs in chunks], axis=1))
        mixed = mixed + bs_ref[:, g:g + 1]
        for c, rs in enumerate(chunks):
            gm_ref[rs, cs] = (u[rs, cs] * mixed[:, c * 128:(c + 1) * 128]).astype(BF16)


def _inproj(h, sc, sh, g, w, cos, sin, w_s, b_s_t, ln_g, ln_b, seq):
    n, d = h.shape
    tm = min(PROJ_TILE, seq)
    per_b = seq // tm
    batch = n // seq
    wn = w.shape[1]
    row = lambda i: (i, 0)
    bat = lambda i: (i // per_b, 0, 0)
    full2 = lambda i: (0, 0)
    res = lambda i: (i // per_b, 0, i % per_b, 0)
    qkv_specs = [pl.BlockSpec((1, r, tm // r, 3 * GROUP_W), res) for _, r in ATTN_GROUPS]
    qkv_shapes = [jax.ShapeDtypeStruct((batch, r, seq // r, 3 * GROUP_W), BF16)
                  for _, r in ATTN_GROUPS]
    return pl.pallas_call(
        _inproj_kernel,
        grid=(n // tm,),
        in_specs=[pl.BlockSpec((tm, d), row),
                  pl.BlockSpec((1, 1, d), bat),
                  pl.BlockSpec((1, 1, d), bat),
                  pl.BlockSpec((1, d), full2),
                  pl.BlockSpec((d, wn), full2),
                  pl.BlockSpec((tm, 128), row),
                  pl.BlockSpec((tm, 128), row),
                  pl.BlockSpec((GMLP_GROUPS, GMLP_CHUNK, GMLP_CHUNK), lambda i: (0, 0, 0)),
                  pl.BlockSpec((GMLP_CHUNK, GMLP_GROUPS), full2),
                  pl.BlockSpec((1, GMLP_W), full2),
                  pl.BlockSpec((1, GMLP_W), full2)],
        out_specs=qkv_specs + [pl.BlockSpec((tm, GMLP_W), row)],
        out_shape=qkv_shapes + [jax.ShapeDtypeStruct((n, GMLP_W), BF16)],
        scratch_shapes=[pltpu.VMEM((_n_strided_groups() * 3 * (GROUP_W // 128), tm, 128), F32)],
        compiler_params=_ARB(1),
        name="inproj_rope_gmlp",
    )(h, sc, sh, g, w, cos, sin, w_s, b_s_t, ln_g, ln_b)


def _attn_kernel(q_ref, k_ref, v_ref, kp_ref, vp_ref, o_ref, lse_ref):
    q_ref, k_ref, v_ref, kp_ref, vp_ref, o_ref, lse_ref = (
        x.at[0] for x in (q_ref, k_ref, v_ref, kp_ref, vp_ref, o_ref, lse_ref))
    qb = q_ref.shape[1]
    first_key = jnp.where(pl.program_id(2) == 0, WIN_BLOCK, 0)
    qi = lax.broadcasted_iota(I32, (WIN_BLOCK, 2 * WIN_BLOCK), 0)
    kj = lax.broadcasted_iota(I32, (WIN_BLOCK, 2 * WIN_BLOCK), 1)
    band = (kj >= qi) & (kj <= qi + WIN_BLOCK)
    low_lanes = lax.broadcasted_iota(I32, (WIN_BLOCK, 128), 1) < HEAD_DIM
    ones = jnp.ones((2 * WIN_BLOCK, 128), BF16)
    for res, i in [(res, i) for res in range(q_ref.shape[0]) for i in range(qb // WIN_BLOCK)]:
        rs = slice(i * WIN_BLOCK, (i + 1) * WIN_BLOCK)
        q = q_ref[res, rs, :]
        if i == 0:
            k_prev, v_prev = kp_ref[res], vp_ref[res]
            valid = band & (kj >= first_key)
        else:
            ps = slice((i - 1) * WIN_BLOCK, i * WIN_BLOCK)
            k_prev, v_prev = k_ref[res, ps, :], v_ref[res, ps, :]
            valid = band
        k_cat = jnp.concatenate([k_prev, k_ref[res, rs, :]], axis=0)
        v_cat = jnp.concatenate([v_prev, v_ref[res, rs, :]], axis=0)
        outs, lses = [], []
        for pair in range(GROUP_W // 128):
            cs = slice(pair * 128, (pair + 1) * 128)
            q_pair = q[:, cs].astype(F32)
            k_pair = k_cat[:, cs]
            v_ones = jnp.concatenate([v_cat[:, cs], ones], axis=1)
            o_half, l_half = [], []
            for head_lanes in (low_lanes, jnp.logical_not(low_lanes)):
                q_head = jnp.where(head_lanes, q_pair, 0.0).astype(BF16)
                s = lax.dot_general(q_head, k_pair, (((1,), (1,)), ((), ())),
                                    preferred_element_type=F32)
                s = jnp.where(valid, s, NEG_INF)
                m = jnp.max(s, axis=-1, keepdims=True)
                p = jnp.exp(s - m).astype(BF16)
                pv = _bdot(p, v_ones)
                den = pv[:, 128:129]
                o_half.append(pv[:, :128] / den)
                l_half.append(jnp.broadcast_to(m + jnp.log(den), (WIN_BLOCK, 128)))
            outs.append(jnp.where(low_lanes, o_half[0], o_half[1]))
            lses.append(jnp.where(low_lanes, l_half[0], l_half[1]))
        o_ref[res, rs, :] = jnp.concatenate(outs, axis=1).astype(o_ref.dtype)
        lse_ref[res, rs, :] = jnp.concatenate(lses, axis=1)


def _attention_group(qkv, g):
    batch, r, sub_len, _ = qkv.shape
    qb = min(ATTN_STEP_ROWS, sub_len)
    n_res = min(ATTN_STEP_ROWS // qb, r)
    n_tiles = sub_len // qb
    per_tile = qb // WIN_BLOCK

    def cur(part):
        return pl.BlockSpec((1, n_res, qb, GROUP_W), lambda b, j, n: (b, j, n, part))

    def prev(part):
        return pl.BlockSpec((1, n_res, WIN_BLOCK, GROUP_W),
                            lambda b, j, n: (b, j, jnp.maximum(n * per_tile - 1, 0), part))

    out_spec = pl.BlockSpec((1, n_res, qb, GROUP_W), lambda b, j, n: (b, j, n, 0))
    return pl.pallas_call(
        _attn_kernel,
        grid=(batch, r // n_res, n_tiles),
        in_specs=[cur(0), cur(1), cur(2), prev(1), prev(2)],
        out_specs=[out_spec, out_spec],
        out_shape=[jax.ShapeDtypeStruct((batch, r, sub_len, GROUP_W), BF16),
                   jax.ShapeDtypeStruct((batch, r, sub_len, GROUP_W), F32)],
        compiler_params=_ARB(3),
        name=f"dilated_attn_g{g}",
    )(qkv, qkv, qkv, qkv, qkv)


def _split_bf16(x):
    hi = x.astype(BF16)
    lo = (x - hi.astype(F32)).astype(BF16)
    return hi, lo


def _merge_kernel(h_ref, sc1_ref, sh1_ref, g1_ref, sc2_ref, sh2_ref, n1_ref, n2_ref,
                  wg_ref, wpa_ref, wpg_ref, wo_ref,
                  o0_ref, o1_ref, o2_ref, l0_ref, l1_ref, l2_ref, gm_ref,
                  wr_ref, br_ref,
                  hn_ref, f_ref, slot_ref, gate_ref, cnt_ref, slab_ref):
    tm, d = h_ref.shape
    slabs = iter(range(slab_ref.shape[0]))

    def token_major(ref):
        r = ref.shape[1]
        if r == 1:
            return ref[0, 0].astype(F32)
        halves = []
        for c in range(GROUP_W // 128):
            slab = next(slabs)
            for j in range(r):
                slab_ref[slab, pl.ds(j, tm // r, stride=r), :] = (
                    ref[0, j, :, c * 128:(c + 1) * 128].astype(F32))
            halves.append(slab_ref[slab])
        return jnp.concatenate(halves, axis=1)

    lses = [token_major(x) for x in (l0_ref, l1_ref, l2_ref)]
    outs = [token_major(x) for x in (o0_ref, o1_ref, o2_ref)]

    h = h_ref[...]
    a = _modulated_norm(h, n1_ref[...], sc1_ref[0], sh1_ref[0]).astype(BF16)
    gates = _sigmoid(_bdot(a, wg_ref[...]))

    m = jnp.maximum(jnp.maximum(lses[0], lses[1]), lses[2])
    es = [jnp.exp(l - m) for l in lses]
    den = es[0] + es[1] + es[2]
    attn = (es[0] * outs[0] + es[1] * outs[1] + es[2] * outs[2]) / den

    merged = (gates[:, :d] * _bdot(attn.astype(BF16), wpa_ref[...])
              + gates[:, d:] * _bdot(gm_ref[...], wpg_ref[...]))
    hn = h + g1_ref[0] * _bdot(merged.astype(BF16), wo_ref[...])
    hn_ref[...] = hn

    f = _modulated_norm(hn, n2_ref[...], sc2_ref[0], sh2_ref[0])
    f_ref[...] = f.astype(BF16)
    _route_tile(f, wr_ref, br_ref, slot_ref, gate_ref, cnt_ref)


def _route_tile(f, wr_ref, br_ref, slot_ref, gate_ref, cnt_ref):
    tm = f.shape[0]
    f_hi, f_lo = _split_bf16(f)
    w_hi, w_lo = _split_bf16(wr_ref[...])
    nt = (((1,), (1,)), ((), ()))
    logits = (lax.dot_general(w_hi, f_hi, nt, preferred_element_type=F32)
              + lax.dot_general(w_hi, f_lo, nt, preferred_element_type=F32)
              + lax.dot_general(w_lo, f_hi, nt, preferred_element_type=F32)) + br_ref[...]

    n_e = logits.shape[0]
    erow = lax.broadcasted_iota(I32, (n_e, tm), 0).astype(F32)
    vals, idxs, sels = [], [], []
    cur = logits
    for _ in range(TOP_K):
        mx = jnp.max(cur, axis=0, keepdims=True)
        ix = jnp.min(jnp.where(cur == mx, erow, float(n_e)), axis=0, keepdims=True)
        sel = erow == ix
        vals.append(mx)
        idxs.append(ix)
        sels.append(sel)
        cur = jnp.where(sel, -jnp.inf, cur)
    exps = [jnp.exp(v - vals[0]) for v in vals]
    tot = exps[0] + exps[1] + exps[2] + exps[3]
    gate_ref[...] = jnp.concatenate([e / tot for e in exps], axis=0)

    onehot = (sels[0] | sels[1] | sels[2] | sels[3])
    oh = jnp.where(onehot, 1.0, 0.0)
    t_r = lax.broadcasted_iota(I32, (tm, tm), 0)
    t_c = lax.broadcasted_iota(I32, (tm, tm), 1)
    upper = jnp.where(t_r < t_c, 1.0, 0.0).astype(BF16)
    prefix = _bdot(oh.astype(BF16), upper)
    cnt = jnp.sum(oh, axis=1, keepdims=True)
    cnt_al = jnp.floor((cnt + (ROW_ALIGN - 1)) * (1.0 / ROW_ALIGN)) * ROW_ALIGN
    e_r = lax.broadcasted_iota(I32, (n_e, n_e), 0)
    e_c = lax.broadcasted_iota(I32, (n_e, n_e), 1)
    lower = jnp.where(e_c < e_r, 1.0, 0.0).astype(BF16)
    start = _bdot(lower, jnp.broadcast_to(cnt_al, (n_e, 128)).astype(BF16))[:, 0:1]
    local = prefix + start
    slots = [jnp.sum(jnp.where(s, local, 0.0), axis=0, keepdims=True) for s in sels]
    slot_ref[...] = jnp.concatenate(slots, axis=0).astype(I32)
    cnt_ref[...] = jnp.broadcast_to(cnt, cnt_ref.shape).astype(I32)


def _merge_route(h, mods, n1, n2, wg, wpa, wpg, wo, outs, lses, gm, wr, br, seq):
    n, d = h.shape
    tm = TOKEN_TILE
    per_b = seq // tm
    n_tiles = n // tm
    row = lambda i: (i, 0)
    bat = lambda i: (i // per_b, 0, 0)
    full2 = lambda i: (0, 0)
    col = lambda i: (0, i)
    sc1, sh1, g1, sc2, sh2 = mods
    res = lambda i: (i // per_b, 0, i % per_b, 0)
    attn_specs = [pl.BlockSpec((1, r, tm // r, GROUP_W), res) for _, r in ATTN_GROUPS]
    in_specs = ([pl.BlockSpec((tm, d), row)]
                + [pl.BlockSpec((1, 1, d), bat)] * 5
                + [pl.BlockSpec((1, d), full2)] * 2
                + [pl.BlockSpec(wg.shape, full2), pl.BlockSpec(wpa.shape, full2),
                   pl.BlockSpec(wpg.shape, full2), pl.BlockSpec(wo.shape, full2)]
                + attn_specs * 2
                + [pl.BlockSpec((tm, GMLP_W), row)]
                + [pl.BlockSpec(wr.shape, full2), pl.BlockSpec(br.shape, full2)])
    out_specs = [pl.BlockSpec((tm, d), row), pl.BlockSpec((tm, d), row),
                 pl.BlockSpec((TOP_K, tm), col), pl.BlockSpec((TOP_K, tm), col),
                 pl.BlockSpec((N_EXPERTS, 128), row)]
    out_shape = [jax.ShapeDtypeStruct((n, d), F32), jax.ShapeDtypeStruct((n, d), BF16),
                 jax.ShapeDtypeStruct((TOP_K, n), I32), jax.ShapeDtypeStruct((TOP_K, n), F32),
                 jax.ShapeDtypeStruct((n_tiles * N_EXPERTS, 128), I32)]
    return pl.pallas_call(
        _merge_kernel,
        grid=(n_tiles,),
        in_specs=in_specs,
        out_specs=out_specs,
        out_shape=out_shape,
        scratch_shapes=[pltpu.VMEM((_n_strided_groups() * 2 * (GROUP_W // 128), tm, 128), F32)],
        compiler_params=_ARB(1),
        name="merge_proj_route",
    )(h, sc1, sh1, g1, sc2, sh2, n1, n2, wg, wpa, wpg, wo, *outs, *lses, gm, wr, br)


def _row_copy(src, dst, sem):
    return pltpu.make_async_copy(src, dst, sem)


def _start_runs(tile, off_ref, start_ref, pieces_ref, copy):
    def per_expert(e, total):
        base = tile * N_EXPERTS + e
        local0, global0, n = start_ref[base], off_ref[base], pieces_ref[base]
        for bit in reversed(range(_UNIT_BITS)):
            rows = ROW_ALIGN << bit
            done = lax.shift_left(lax.shift_right_logical(n, bit + 1), bit + 1) * ROW_ALIGN

            @pl.when((lax.shift_right_logical(n, bit) & 1) == 1)
            def _():
                copy(pl.multiple_of(local0 + done, ROW_ALIGN),
                     pl.multiple_of(global0 + done, ROW_ALIGN), rows).start()

        return total + n

    return lax.fori_loop(0, N_EXPERTS, per_expert, 0, unroll=4)


def _wait_runs(total, copy):
    top = _UNIT_BITS - 1
    lax.fori_loop(0, lax.shift_right_logical(total, top),
                  lambda i, c: (copy(0, 0, ROW_ALIGN << top).wait(), c)[1], 0)
    for bit in reversed(range(top)):
        @pl.when((lax.shift_right_logical(total, bit) & 1) == 1)
        def _():
            copy(0, 0, ROW_ALIGN << bit).wait()


def _slot_hits(slot_ref, chunk, rows=SORT_CHUNK):
    tm = slot_ref.shape[1]
    s = lax.broadcasted_iota(I32, (rows, tm), 0) + chunk * rows
    return [slot_ref[k:k + 1, :] == s for k in range(TOP_K)]


def _dispatch_kernel(zb_ref, nz_ref, tail_off_ref, tail_pieces_ref, off_ref, start_ref, pieces_ref,
                     slot_ref, f_ref, buf_ref, sorted_ref, zero_ref, pending_ref, sems, zsem):
    step = pl.program_id(0)
    last_step = pl.num_programs(0) - 1
    buf_slot = lax.rem(step, 2)
    sorted_now = sorted_ref.at[buf_slot]

    def zero_block(j):
        start = pl.multiple_of(zb_ref[j] * EXPERT_BLOCK, EXPERT_BLOCK)
        return _row_copy(zero_ref, buf_ref.at[pl.ds(start, EXPERT_BLOCK)], zsem)

    def zero_piece(local_row, global_row, rows):
        del local_row
        return _row_copy(zero_ref.at[pl.ds(0, rows)], buf_ref.at[pl.ds(global_row, rows)], zsem)

    @pl.when(step == 0)
    def _():
        zero_ref[...] = jnp.zeros_like(zero_ref)
        lax.fori_loop(0, nz_ref[0], lambda j, c: (zero_block(j).start(), c)[1], 0)
        _start_runs(0, tail_off_ref, tail_off_ref, tail_pieces_ref, zero_piece)

    f = f_ref[...]
    for c in range(sorted_now.shape[0] // SORT_CHUNK):
        hits = _slot_hits(slot_ref, c)
        onehot = jnp.where(hits[0] | hits[1] | hits[2] | hits[3], 1.0, 0.0).astype(BF16)
        sorted_now[c * SORT_CHUNK:(c + 1) * SORT_CHUNK, :] = _bdot(onehot, f)

    def copy_from(slot):
        def copy(local_row, global_row, rows):
            return _row_copy(sorted_ref.at[slot, pl.ds(local_row, rows)],
                             buf_ref.at[pl.ds(global_row, rows)], sems.at[slot])
        return copy

    total = _start_runs(step, off_ref, start_ref, pieces_ref, copy_from(buf_slot))

    @pl.when(step > 0)
    def _():
        _wait_runs(pending_ref[0], copy_from(1 - buf_slot))

    pending_ref[0] = total

    @pl.when(step == last_step)
    def _():
        _wait_runs(total, copy_from(buf_slot))
        lax.fori_loop(0, nz_ref[0], lambda j, c: (zero_block(j).wait(), c)[1], 0)
        tail_total = lax.fori_loop(0, N_EXPERTS, lambda e, t: t + tail_pieces_ref[e], 0)
        _wait_runs(tail_total, zero_piece)


def _dispatch(plan, slots, f, n_blocks):
    n, d = f.shape
    tm = TOKEN_TILE
    return pl.pallas_call(
        _dispatch_kernel,
        grid_spec=pltpu.PrefetchScalarGridSpec(
            num_scalar_prefetch=7, grid=(n // tm,),
            in_specs=[pl.BlockSpec((TOP_K, tm), lambda i, *_: (0, i)),
                      pl.BlockSpec((tm, d), lambda i, *_: (i, 0))],
            out_specs=pl.BlockSpec(memory_space=pl.ANY),
            scratch_shapes=[pltpu.VMEM((2, _local_rows(tm), d), F32),
                            pltpu.VMEM((EXPERT_BLOCK, d), F32),
                            pltpu.SMEM((1,), I32),
                            pltpu.SemaphoreType.DMA((2,)), pltpu.SemaphoreType.DMA(())]),
        out_shape=jax.ShapeDtypeStruct((n_blocks * EXPERT_BLOCK, d), F32),
        compiler_params=_ARB(1),
        name="dispatch_rows",
    )(plan["zero_blocks"], plan["n_zero"], plan["tail_off"], plan["tail_pieces"],
      plan["run_off"], plan["run_start"], plan["run_pieces"], slots, f)


def _expert_kernel(be_ref, first_ref, rows_ref, x_ref, w1_ref, b1_ref, w2_ref, b2_ref, o_ref,
                   w1b_ref, w2p_ref, slab_ref):
    del be_ref
    step = pl.program_id(0)
    n_rows = rows_ref[step]
    live = n_rows > 0
    ff, d = w2p_ref.shape
    half = 64

    @pl.when(live & (first_ref[step] == 1))
    def _():
        w1b_ref[...] = w1_ref[0, 0].astype(BF16)
        for c in range(d // 128):
            cs = slice(c * 128, (c + 1) * 128)
            for k in range(ff // 128):
                r0 = k * 128
                slab_ref[pl.ds(r0, half, stride=2), :] = w2_ref[0, 0, r0:r0 + half, cs]
                slab_ref[pl.ds(r0 + 1, half, stride=2), :] = w2_ref[0, 0, r0 + half:r0 + 128, cs]
            w2p_ref[:, cs] = slab_ref[...].astype(BF16)

    def ffn(rs):
        x = x_ref[rs, :].astype(BF16)
        hb = _bdot(x, w1b_ref[...]) + b1_ref[0, 0]
        lane = lax.broadcasted_iota(I32, (x.shape[0], 128), 1)
        even = (lane & 1) == 0
        glu, lin = [], []
        for k in range(ff // 128):
            a = hb[:, 256 * k:256 * k + 128]
            b = hb[:, 256 * k + 128:256 * k + 256]
            glu.append(jnp.where(even, a, pltpu.roll(b, 1, 1)))
            lin.append(jnp.where(even, pltpu.roll(a, 127, 1), b))
        x_glu = jnp.minimum(jnp.concatenate(glu, axis=1), SWIGLU_LIMIT)
        x_lin = jnp.clip(jnp.concatenate(lin, axis=1), -SWIGLU_LIMIT, SWIGLU_LIMIT)
        act = x_glu * _sigmoid(SWIGLU_ALPHA * x_glu) * (x_lin + 1.0)
        o_ref[rs, :] = _bdot(act.astype(BF16), w2p_ref[...]) + b2_ref[0, 0]

    for r0 in range(0, EXPERT_BLOCK, EXPERT_SUB):
        rs = slice(r0, r0 + EXPERT_SUB)
        pl.when(n_rows > r0)(functools.partial(ffn, rs))

        @pl.when(n_rows <= r0)
        def _():
            o_ref[rs, :] = jnp.zeros((EXPERT_SUB, d), F32)


def _experts(layer, block_e, first, block_rows, buf, w1, b1, w2, b2, n_blocks):
    d = buf.shape[1]
    ff = w2.shape[2]
    xmap = lambda i, be, fi, rw: (jnp.where(rw[i] > 0, i, 0), 0)
    wmap = lambda i, be, fi, rw: (layer, be[i], 0, 0)
    return pl.pallas_call(
        _expert_kernel,
        grid_spec=pltpu.PrefetchScalarGridSpec(
            num_scalar_prefetch=3, grid=(n_blocks,),
            in_specs=[pl.BlockSpec((EXPERT_BLOCK, d), xmap),
                      pl.BlockSpec((1, 1, d, 2 * ff), wmap),
                      pl.BlockSpec((1, 1, 1, 2 * ff), wmap),
                      pl.BlockSpec((1, 1, ff, d), wmap),
                      pl.BlockSpec((1, 1, 1, d), wmap)],
            out_specs=pl.BlockSpec((EXPERT_BLOCK, d), lambda i, be, fi, rw: (i, 0)),
            scratch_shapes=[pltpu.VMEM((d, 2 * ff), BF16), pltpu.VMEM((ff, d), BF16),
                            pltpu.VMEM((ff, 128), F32)]),
        out_shape=jax.ShapeDtypeStruct((n_blocks * EXPERT_BLOCK, d), F32),
        compiler_params=_ARB(1),
        name="expert_ffn",
    )(block_e, first, block_rows, buf, w1, b1, w2, b2)


def _combine_kernel(off_ref, start_ref, pieces_ref, slot_ref, gate_ref, h_ref, g2_ref, fin_ref,
                    obuf_ref, out_ref, rows_ref, sems, *, final_norm):
    step = pl.program_id(0)
    buf_slot = lax.rem(step, 2)
    rows_now = rows_ref.at[buf_slot]

    def copy_into(slot):
        def copy(local_row, global_row, rows):
            return _row_copy(obuf_ref.at[pl.ds(global_row, rows)],
                             rows_ref.at[slot, pl.ds(local_row, rows)], sems.at[slot])
        return copy

    @pl.when(step == 0)
    def _():
        rows_ref[...] = jnp.zeros_like(rows_ref)
        _start_runs(0, off_ref, start_ref, pieces_ref, copy_into(0))

    @pl.when(step + 1 < pl.num_programs(0))
    def _():
        _start_runs(step + 1, off_ref, start_ref, pieces_ref, copy_into(1 - buf_slot))

    total = lax.fori_loop(0, N_EXPERTS,
                          lambda e, t: t + pieces_ref[step * N_EXPERTS + e], 0)
    _wait_runs(total, copy_into(buf_slot))

    y = jnp.zeros(h_ref.shape, F32)
    for c in range(rows_now.shape[0] // UNSORT_CHUNK):
        hits = _slot_hits(slot_ref, c, UNSORT_CHUNK)
        weight = jnp.where(hits[0], gate_ref[0:1, :], 0.0)
        for k in range(1, TOP_K):
            weight = weight + jnp.where(hits[k], gate_ref[k:k + 1, :], 0.0)
        rows = rows_now[c * UNSORT_CHUNK:(c + 1) * UNSORT_CHUNK, :].astype(BF16)
        y = y + lax.dot_general(weight.astype(BF16), rows, (((0,), (0,)), ((), ())),
                                preferred_element_type=F32)
    hn = h_ref[...] + g2_ref[0] * y
    if final_norm:
        ms = jnp.mean(hn * hn, axis=-1, keepdims=True)
        hn = hn * lax.rsqrt(ms + RMS_EPS) * fin_ref[...]
    out_ref[...] = hn


def _combine(plan, slots, gate, h, g2, final_g, obuf, seq, final_norm):
    n, d = h.shape
    tm = TOKEN_TILE
    per_b = seq // tm
    return pl.pallas_call(
        functools.partial(_combine_kernel, final_norm=final_norm),
        grid_spec=pltpu.PrefetchScalarGridSpec(
            num_scalar_prefetch=3, grid=(n // tm,),
            in_specs=[pl.BlockSpec((TOP_K, tm), lambda i, *_: (0, i)),
                      pl.BlockSpec((TOP_K, tm), lambda i, *_: (0, i)),
                      pl.BlockSpec((tm, d), lambda i, *_: (i, 0)),
                      pl.BlockSpec((1, 1, d), lambda i, *_: (i // per_b, 0, 0)),
                      pl.BlockSpec((1, d), lambda i, *_: (0, 0)),
                      pl.BlockSpec(memory_space=pl.ANY)],
            out_specs=pl.BlockSpec((tm, d), lambda i, *_: (i, 0)),
            scratch_shapes=[pltpu.VMEM((2, _local_rows(tm), d), F32),
                            pltpu.SemaphoreType.DMA((2,))]),
        out_shape=jax.ShapeDtypeStruct((n, d), F32),
        compiler_params=_ARB(1),
        name="combine_rows",
    )(plan["run_off"], plan["run_start"], plan["run_pieces"], slots, gate, h, g2, final_g, obuf)


def _local_rows(tm):
    rows = tm * TOP_K + N_EXPERTS * (ROW_ALIGN - 1)
    return -(-rows // SORT_CHUNK) * SORT_CHUNK


def _num_blocks(n):
    rows = n * TOP_K + (n // TOKEN_TILE) * N_EXPERTS * (ROW_ALIGN - 1)
    return -(-rows // EXPERT_BLOCK) + N_EXPERTS


def _slot_plan(tile_counts, n):
    blk = EXPERT_BLOCK
    n_blocks = _num_blocks(n)
    cnt = (tile_counts + ROW_ALIGN - 1) // ROW_ALIGN * ROW_ALIGN
    total = jnp.sum(cnt, axis=0)
    padded = (total + blk - 1) // blk * blk
    pad_end = jnp.cumsum(padded)
    pad_start = pad_end - padded
    run_off = pad_start[None, :] + jnp.cumsum(cnt, axis=0) - cnt
    run_start = jnp.cumsum(cnt, axis=1) - cnt
    n_valid = pad_end[-1] // blk
    block_start = jnp.arange(n_blocks, dtype=I32) * blk
    block_e = jnp.sum(pad_end[None, :] <= block_start[:, None], axis=1)
    last_e = jnp.sum(pad_end <= (n_valid - 1) * blk)
    block_e = jnp.where(jnp.arange(n_blocks) < n_valid, block_e, last_e).astype(I32)
    first = jnp.concatenate([jnp.ones((1,), I32),
                             (block_e[1:] != block_e[:-1]).astype(I32)])
    hit = jnp.arange(N_EXPERTS)[None, :] == block_e[:, None]
    data_end = jnp.sum(jnp.where(hit, pad_start + total, 0), axis=1)
    block_rows = jnp.where(jnp.arange(n_blocks) < n_valid,
                           jnp.clip(data_end - block_start, 0, blk), 0).astype(I32)
    flat = lambda a: a.reshape(-1).astype(I32)
    zero_blocks = n_valid + jnp.arange(n_blocks - (n * TOP_K) // blk)
    return dict(block_e=block_e, first=first, block_rows=block_rows,
                zero_blocks=flat(jnp.minimum(zero_blocks, n_blocks - 1)),
                n_zero=(n_blocks - n_valid).reshape(1).astype(I32),
                tail_off=flat(pad_start + total), tail_pieces=flat((padded - total) // ROW_ALIGN),
                run_off=flat(run_off), run_start=flat(run_start),
                run_pieces=flat(cnt // ROW_ALIGN))


def kernel(x, c, positions, w_ada, b_ada, norm1_g, w_in, w_s, b_s, ln_g, ln_b, w_pa, w_pg,
           w_o, norm2_g, w_router, b_router, w1, b1, w2, b2, final_g):
    batch, seq, d = x.shape
    depth = w_ada.shape[0]
    n = batch * seq
    n_blocks = _num_blocks(n)

    mod = _modulation(c, w_ada, b_ada)
    cos, sin = _rope_tables(positions)

    n_mix = 3 * ATTN_W + 2 * GMLP_W
    b1r = b1.reshape(depth, N_EXPERTS, 1, -1)
    b2r = b2.reshape(depth, N_EXPERTS, 1, -1)

    h = x.reshape(n, d)
    for l in range(depth):
        ml = mod[l].reshape(batch, 6, 1, d)
        sh1, sc1, g1, sh2, sc2, g2 = [ml[:, i] for i in range(6)]
        w_mix = w_in[l, :, :n_mix].astype(BF16)
        w_gate = w_in[l, :, n_mix:].astype(BF16)

        *qkvs, gm = _inproj(h, sc1, sh1, norm1_g[l].reshape(1, d), w_mix, cos, sin,
                            w_s[l], b_s[l].T, ln_g[l].reshape(1, -1), ln_b[l].reshape(1, -1), seq)
        outs, lses = zip(*[_attention_group(qkvs[g], g) for g in range(N_GROUPS)])

        h, f, slots, gate, counts = _merge_route(
            h, (sc1, sh1, g1, sc2, sh2), norm1_g[l].reshape(1, d), norm2_g[l].reshape(1, d),
            w_gate, w_pa[l].astype(BF16), w_pg[l].astype(BF16), w_o[l].astype(BF16),
            outs, lses, gm, w_router[l].T, b_router[l].reshape(-1, 1), seq)

        plan = _slot_plan(counts[:, 0].reshape(-1, N_EXPERTS), n)
        buf = _dispatch(plan, slots, f, n_blocks)
        obuf = _experts(l, plan["block_e"], plan["first"], plan["block_rows"], buf,
                        w1, b1r, w2, b2r, n_blocks)

        h = _combine(plan, slots, gate, h, g2, final_g.reshape(1, d), obuf, seq,
                     final_norm=(l == depth - 1))
    return h.reshape(batch, seq, d)
```

```python
import functools

import jax
import jax.numpy as jnp
from jax import lax
from jax.experimental import pallas as pl
from jax.experimental.pallas import tpu as pltpu

F32 = jnp.float32
BF16 = jnp.bfloat16
I32 = jnp.int32

HEAD_DIM = 64
HEADS_PER_GROUP = 4
GROUP_W = HEADS_PER_GROUP * HEAD_DIM
ATTN_GROUPS = ((128, 1), (512, 4), (2048, 16))
N_GROUPS = len(ATTN_GROUPS)
ATTN_W = N_GROUPS * GROUP_W
WIN_BLOCK = 128
ROPE_THETA = 10000.0
GMLP_CHUNK = 128
GMLP_GROUPS = 4
GMLP_W = GMLP_GROUPS * 128
N_EXPERTS = 32
TOP_K = 4
SWIGLU_ALPHA = 1.702
SWIGLU_LIMIT = 7.0
RMS_EPS = 1e-5
LN_EPS = 1e-5
NEG_INF = -1e30

TOKEN_TILE = 512
PROJ_TILE = 1024
ATTN_STEP_ROWS = 4096
EXPERT_BLOCK = 512
EXPERT_SUB = 512
ROW_ALIGN = 8
_MAX_UNITS = max(TOKEN_TILE, EXPERT_BLOCK) // ROW_ALIGN
_UNIT_BITS = _MAX_UNITS.bit_length()
_LOW_BITS = 4
SORT_CHUNK = 256
UNSORT_CHUNK = 256
VMEM_LIMIT = 56 * 1024 * 1024

def _ARB(n, **kw):
    return pltpu.CompilerParams(dimension_semantics=("arbitrary",) * n,
                                vmem_limit_bytes=VMEM_LIMIT, **kw)


def _bdot(a, b):
    return jnp.dot(a, b, preferred_element_type=F32)


def _sigmoid(x):
    return 0.5 * jnp.tanh(0.5 * x) + 0.5


def _n_strided_groups():
    return sum(1 for _, r in ATTN_GROUPS if r > 1)


def _split3(x):
    hi = x.astype(BF16)
    r1 = x - hi.astype(F32)
    mid = r1.astype(BF16)
    lo = (r1 - mid.astype(F32)).astype(BF16)
    return hi, mid, lo


def _dot_f32(a, b):
    a0, a1, a2 = _split3(a)
    b0, b1, b2 = _split3(b)
    small = _bdot(a0, b2) + _bdot(a1, b1) + _bdot(a2, b0)
    mid = _bdot(a0, b1) + _bdot(a1, b0)
    return (small + mid) + _bdot(a0, b0)


def _mod_kernel(c_ref, w_ref, b_ref, o_ref):
    c = c_ref[...]
    s = c * jax.nn.sigmoid(c)
    o_ref[0] = _dot_f32(s, w_ref[0]) + b_ref[0]


def _modulation(c, w_ada, b_ada):
    depth, d, six_d = w_ada.shape
    b = c.shape[0]
    n_col = six_d // d
    return pl.pallas_call(
        _mod_kernel,
        grid=(depth, n_col),
        in_specs=[pl.BlockSpec((b, d), lambda l, j: (0, 0)),
                  pl.BlockSpec((1, d, d), lambda l, j: (l, 0, j)),
                  pl.BlockSpec((1, 1, d), lambda l, j: (l, 0, j))],
        out_specs=pl.BlockSpec((1, b, d), lambda l, j: (l, 0, j)),
        out_shape=jax.ShapeDtypeStruct((depth, b, six_d), F32),
        compiler_params=_ARB(2),
        name="adaln_mod",
    )(c, w_ada, b_ada.reshape(depth, 1, six_d))


def _rope_kernel(pos_ref, freq_ref, sign_ref, cos_ref, sin_ref):
    ang = pos_ref[...].astype(F32) * freq_ref[...]
    cos_ref[...] = jnp.cos(ang)
    sin_ref[...] = jnp.sin(ang) * sign_ref[...]


def _rope_tables(positions):
    n = positions.size
    half = HEAD_DIM // 2
    inv_freq = ROPE_THETA ** (-jnp.arange(half, dtype=F32) / half)
    freq = jnp.tile(inv_freq, 4).reshape(1, 128)
    sign = jnp.tile(jnp.concatenate([-jnp.ones((half,), F32), jnp.ones((half,), F32)]), 2)
    sign = sign.reshape(1, 128)
    tile = 1024
    return pl.pallas_call(
        _rope_kernel,
        grid=(n // tile,),
        in_specs=[pl.BlockSpec((tile, 1), lambda i: (i, 0)),
                  pl.BlockSpec((1, 128), lambda i: (0, 0)),
                  pl.BlockSpec((1, 128), lambda i: (0, 0))],
        out_specs=[pl.BlockSpec((tile, 128), lambda i: (i, 0))] * 2,
        out_shape=[jax.ShapeDtypeStruct((n, 128), F32)] * 2,
        compiler_params=_ARB(1),
        name="rope_tables",
    )(positions.reshape(n, 1), freq, sign)


def _modulated_norm(x, g, sc, sh):
    ms = jnp.mean(x * x, axis=-1, keepdims=True)
    y = x * lax.rsqrt(ms + RMS_EPS) * g
    return y * (1.0 + sc) + sh


def _inproj_kernel(h_ref, sc_ref, sh_ref, g_ref, w_ref, cos_ref, sin_ref,
                   ws_ref, bs_ref, lng_ref, lnb_ref, qkv0_ref, qkv1_ref, qkv2_ref, gm_ref,
                   slab_ref):
    tm = h_ref.shape[0]
    a = _modulated_norm(h_ref[...], g_ref[...], sc_ref[0], sh_ref[0]).astype(BF16)
    cos = cos_ref[...]
    sin = sin_ref[...]
    lane = lax.broadcasted_iota(I32, (tm, 128), 1)
    first_half = (lane % HEAD_DIM) < (HEAD_DIM // 2)

    def rope(x):
        swapped = jnp.where(first_half, pltpu.roll(x, 96, 1), pltpu.roll(x, 32, 1))
        return x * cos + swapped * sin

    slab = 0
    for g, out_ref in enumerate((qkv0_ref, qkv1_ref, qkv2_ref)):
        r = ATTN_GROUPS[g][1]
        for part in range(3):
            c0 = part * ATTN_W + g * GROUP_W
            y = _bdot(a, w_ref[:, c0:c0 + GROUP_W])
            halves = [y[:, :128], y[:, 128:]]
            if part < 2:
                halves = [rope(x) for x in halves]
            if part == 0:
                halves = [x * (HEAD_DIM ** -0.5) for x in halves]
            for c, x in enumerate(halves):
                o0 = part * GROUP_W + c * 128
                if r == 1:
                    out_ref[0, 0, :, o0:o0 + 128] = x.astype(BF16)
                else:
                    slab_ref[slab] = x
                    for j in range(r):
                        rows = slab_ref[slab, pl.ds(j, tm // r, stride=r), :]
                        out_ref[0, j, :, o0:o0 + 128] = rows.astype(BF16)
                    slab += 1

    c0 = 3 * ATTN_W
    u = jax.nn.gelu(_bdot(a, w_ref[:, c0:c0 + GMLP_W]))
    v = jax.nn.gelu(_bdot(a, w_ref[:, c0 + GMLP_W:c0 + 2 * GMLP_W]))
    mu = jnp.mean(v, axis=-1, keepdims=True)
    var = jnp.mean(jnp.square(v - mu), axis=-1, keepdims=True)
    v = ((v - mu) * lax.rsqrt(var + LN_EPS) * lng_ref[...] + lnb_ref[...]).astype(BF16)
    row = lax.broadcasted_iota(I32, (GMLP_CHUNK, GMLP_CHUNK), 0)
    col = lax.broadcasted_iota(I32, (GMLP_CHUNK, GMLP_CHUNK), 1)
    tril = col <= row
    chunks = [slice(c * GMLP_CHUNK, (c + 1) * GMLP_CHUNK) for c in range(tm // GMLP_CHUNK)]
    for g in range(GMLP_GROUPS):
        wsg = jnp.where(tril, ws_ref[g], 0.0).astype(BF16)
        cs = slice(g * 128, (g + 1) * 128)
        mixed = _bdot(wsg, jnp.concatenate([v[rs, cs] for rs in chunks], axis=1))
        mixed = mixed + bs_ref[:, g:g + 1]
        for c, rs in enumerate(chunks):
            gm_ref[rs, cs] = (u[rs, cs] * mixed[:, c * 128:(c + 1) * 128]).astype(BF16)


def _inproj(h, sc, sh, g, w, cos, sin, w_s, b_s_t, ln_g, ln_b, seq):
    n, d = h.shape
    tm = min(PROJ_TILE, seq)
    per_b = seq // tm
    batch = n // seq
    wn = w.shape[1]
    row = lambda i: (i, 0)
    bat = lambda i: (i // per_b, 0, 0)
    full2 = lambda i: (0, 0)
    res = lambda i: (i // per_b, 0, i % per_b, 0)
    qkv_specs = [pl.BlockSpec((1, r, tm // r, 3 * GROUP_W), res) for _, r in ATTN_GROUPS]
    qkv_shapes = [jax.ShapeDtypeStruct((batch, r, seq // r, 3 * GROUP_W), BF16)
                  for _, r in ATTN_GROUPS]
    return pl.pallas_call(
        _inproj_kernel,
        grid=(n // tm,),
        in_specs=[pl.BlockSpec((tm, d), row),
                  pl.BlockSpec((1, 1, d), bat),
                  pl.BlockSpec((1, 1, d), bat),
                  pl.BlockSpec((1, d), full2),
                  pl.BlockSpec((d, wn), full2),
                  pl.BlockSpec((tm, 128), row),
                  pl.BlockSpec((tm, 128), row),
                  pl.BlockSpec((GMLP_GROUPS, GMLP_CHUNK, GMLP_CHUNK), lambda i: (0, 0, 0)),
                  pl.BlockSpec((GMLP_CHUNK, GMLP_GROUPS), full2),
                  pl.BlockSpec((1, GMLP_W), full2),
                  pl.BlockSpec((1, GMLP_W), full2)],
        out_specs=qkv_specs + [pl.BlockSpec((tm, GMLP_W), row)],
        out_shape=qkv_shapes + [jax.ShapeDtypeStruct((n, GMLP_W), BF16)],
        scratch_shapes=[pltpu.VMEM((_n_strided_groups() * 3 * (GROUP_W // 128), tm, 128), F32)],
        compiler_params=_ARB(1),
        name="inproj_rope_gmlp",
    )(h, sc, sh, g, w, cos, sin, w_s, b_s_t, ln_g, ln_b)


def _attn_kernel(q_ref, k_ref, v_ref, kp_ref, vp_ref, o_ref, lse_ref):
    q_ref, k_ref, v_ref, kp_ref, vp_ref, o_ref, lse_ref = (
        x.at[0] for x in (q_ref, k_ref, v_ref, kp_ref, vp_ref, o_ref, lse_ref))
    qb = q_ref.shape[1]
    first_key = jnp.where(pl.program_id(2) == 0, WIN_BLOCK, 0)
    qi = lax.broadcasted_iota(I32, (WIN_BLOCK, 2 * WIN_BLOCK), 0)
    kj = lax.broadcasted_iota(I32, (WIN_BLOCK, 2 * WIN_BLOCK), 1)
    band = (kj >= qi) & (kj <= qi + WIN_BLOCK)
    low_lanes = lax.broadcasted_iota(I32, (WIN_BLOCK, 128), 1) < HEAD_DIM
    ones = jnp.ones((2 * WIN_BLOCK, 128), BF16)
    for res, i in [(res, i) for res in range(q_ref.shape[0]) for i in range(qb // WIN_BLOCK)]:
        rs = slice(i * WIN_BLOCK, (i + 1) * WIN_BLOCK)
        q = q_ref[res, rs, :]
        if i == 0:
            k_prev, v_prev = kp_ref[res], vp_ref[res]
            valid = band & (kj >= first_key)
        else:
            ps = slice((i - 1) * WIN_BLOCK, i * WIN_BLOCK)
            k_prev, v_prev = k_ref[res, ps, :], v_ref[res, ps, :]
            valid = band
        k_cat = jnp.concatenate([k_prev, k_ref[res, rs, :]], axis=0)
        v_cat = jnp.concatenate([v_prev, v_ref[res, rs, :]], axis=0)
        outs, lses = [], []
        for pair in range(GROUP_W // 128):
            cs = slice(pair * 128, (pair + 1) * 128)
            q_pair = q[:, cs].astype(F32)
            k_pair = k_cat[:, cs]
            v_ones = jnp.concatenate([v_cat[:, cs], ones], axis=1)
            o_half, l_half = [], []
            for head_lanes in (low_lanes, jnp.logical_not(low_lanes)):
                q_head = jnp.where(head_lanes, q_pair, 0.0).astype(BF16)
                s = lax.dot_general(q_head, k_pair, (((1,), (1,)), ((), ())),
                                    preferred_element_type=F32)
                s = jnp.where(valid, s, NEG_INF)
                m = jnp.max(s, axis=-1, keepdims=True)
                p = jnp.exp(s - m).astype(BF16)
                pv = _bdot(p, v_ones)
                den = pv[:, 128:129]
                o_half.append(pv[:, :128] / den)
                l_half.append(jnp.broadcast_to(m + jnp.log(den), (WIN_BLOCK, 128)))
            outs.append(jnp.where(low_lanes, o_half[0], o_half[1]))
            lses.append(jnp.where(low_lanes, l_half[0], l_half[1]))
        o_ref[res, rs, :] = jnp.concatenate(outs, axis=1).astype(o_ref.dtype)
        lse_ref[res, rs, :] = jnp.concatenate(lses, axis=1)


def _attention_group(qkv, g):
    batch, r, sub_len, _ = qkv.shape
    qb = min(ATTN_STEP_ROWS, sub_len)
    n_res = min(ATTN_STEP_ROWS // qb, r)
    n_tiles = sub_len // qb
    per_tile = qb // WIN_BLOCK

    def cur(part):
        return pl.BlockSpec((1, n_res, qb, GROUP_W), lambda b, j, n: (b, j, n, part))

    def prev(part):
        return pl.BlockSpec((1, n_res, WIN_BLOCK, GROUP_W),
                            lambda b, j, n: (b, j, jnp.maximum(n * per_tile - 1, 0), part))

    out_spec = pl.BlockSpec((1, n_res, qb, GROUP_W), lambda b, j, n: (b, j, n, 0))
    return pl.pallas_call(
        _attn_kernel,
        grid=(batch, r // n_res, n_tiles),
        in_specs=[cur(0), cur(1), cur(2), prev(1), prev(2)],
        out_specs=[out_spec, out_spec],
        out_shape=[jax.ShapeDtypeStruct((batch, r, sub_len, GROUP_W), BF16),
                   jax.ShapeDtypeStruct((batch, r, sub_len, GROUP_W), F32)],
        compiler_params=_ARB(3),
        name=f"dilated_attn_g{g}",
    )(qkv, qkv, qkv, qkv, qkv)


def _split_bf16(x):
    hi = x.astype(BF16)
    lo = (x - hi.astype(F32)).astype(BF16)
    return hi, lo


def _merge_kernel(h_ref, sc1_ref, sh1_ref, g1_ref, sc2_ref, sh2_ref, n1_ref, n2_ref,
                  wg_ref, wpa_ref, wpg_ref, wo_ref,
                  o0_ref, o1_ref, o2_ref, l0_ref, l1_ref, l2_ref, gm_ref,
                  wr_ref, br_ref,
                  hn_ref, f_ref, slot_ref, gate_ref, cnt_ref, slab_ref):
    tm, d = h_ref.shape
    slabs = iter(range(slab_ref.shape[0]))

    def token_major(ref):
        r = ref.shape[1]
        if r == 1:
            return ref[0, 0].astype(F32)
        halves = []
        for c in range(GROUP_W // 128):
            slab = next(slabs)
            for j in range(r):
                slab_ref[slab, pl.ds(j, tm // r, stride=r), :] = (
                    ref[0, j, :, c * 128:(c + 1) * 128].astype(F32))
            halves.append(slab_ref[slab])
        return jnp.concatenate(halves, axis=1)

    lses = [token_major(x) for x in (l0_ref, l1_ref, l2_ref)]
    outs = [token_major(x) for x in (o0_ref, o1_ref, o2_ref)]

    h = h_ref[...]
    a = _modulated_norm(h, n1_ref[...], sc1_ref[0], sh1_ref[0]).astype(BF16)
    gates = _sigmoid(_bdot(a, wg_ref[...]))

    m = jnp.maximum(jnp.maximum(lses[0], lses[1]), lses[2])
    es = [jnp.exp(l - m) for l in lses]
    den = es[0] + es[1] + es[2]
    attn = (es[0] * outs[0] + es[1] * outs[1] + es[2] * outs[2]) / den

    merged = (gates[:, :d] * _bdot(attn.astype(BF16), wpa_ref[...])
              + gates[:, d:] * _bdot(gm_ref[...], wpg_ref[...]))
    hn = h + g1_ref[0] * _bdot(merged.astype(BF16), wo_ref[...])
    hn_ref[...] = hn

    f = _modulated_norm(hn, n2_ref[...], sc2_ref[0], sh2_ref[0])
    f_ref[...] = f.astype(BF16)
    _route_tile(f, wr_ref, br_ref, slot_ref, gate_ref, cnt_ref)


def _route_tile(f, wr_ref, br_ref, slot_ref, gate_ref, cnt_ref):
    tm = f.shape[0]
    f_hi, f_lo = _split_bf16(f)
    w_hi, w_lo = _split_bf16(wr_ref[...])
    nt = (((1,), (1,)), ((), ()))
    logits = (lax.dot_general(w_hi, f_hi, nt, preferred_element_type=F32)
              + lax.dot_general(w_hi, f_lo, nt, preferred_element_type=F32)
              + lax.dot_general(w_lo, f_hi, nt, preferred_element_type=F32)) + br_ref[...]

    n_e = logits.shape[0]
    erow = lax.broadcasted_iota(I32, (n_e, tm), 0).astype(F32)
    vals, idxs, sels = [], [], []
    cur = logits
    for _ in range(TOP_K):
        mx = jnp.max(cur, axis=0, keepdims=True)
        ix = jnp.min(jnp.where(cur == mx, erow, float(n_e)), axis=0, keepdims=True)
        sel = erow == ix
        vals.append(mx)
        idxs.append(ix)
        sels.append(sel)
        cur = jnp.where(sel, -jnp.inf, cur)
    exps = [jnp.exp(v - vals[0]) for v in vals]
    tot = exps[0] + exps[1] + exps[2] + exps[3]
    gate_ref[...] = jnp.concatenate([e / tot for e in exps], axis=0)

    onehot = (sels[0] | sels[1] | sels[2] | sels[3])
    oh = jnp.where(onehot, 1.0, 0.0)
    t_r = lax.broadcasted_iota(I32, (tm, tm), 0)
    t_c = lax.broadcasted_iota(I32, (tm, tm), 1)
    upper = jnp.where(t_r < t_c, 1.0, 0.0).astype(BF16)
    prefix = _bdot(oh.astype(BF16), upper)
    cnt = jnp.sum(oh, axis=1, keepdims=True)
    cnt_al = jnp.floor((cnt + (ROW_ALIGN - 1)) * (1.0 / ROW_ALIGN)) * ROW_ALIGN
    e_r = lax.broadcasted_iota(I32, (n_e, n_e), 0)
    e_c = lax.broadcasted_iota(I32, (n_e, n_e), 1)
    lower = jnp.where(e_c < e_r, 1.0, 0.0).astype(BF16)
    start = _bdot(lower, jnp.broadcast_to(cnt_al, (n_e, 128)).astype(BF16))[:, 0:1]
    local = prefix + start
    slots = [jnp.sum(jnp.where(s, local, 0.0), axis=0, keepdims=True) for s in sels]
    slot_ref[...] = jnp.concatenate(slots, axis=0).astype(I32)
    cnt_ref[...] = jnp.broadcast_to(cnt, cnt_ref.shape).astype(I32)


def _merge_route(h, mods, n1, n2, wg, wpa, wpg, wo, outs, lses, gm, wr, br, seq):
    n, d = h.shape
    tm = TOKEN_TILE
    per_b = seq // tm
    n_tiles = n // tm
    row = lambda i: (i, 0)
    bat = lambda i: (i // per_b, 0, 0)
    full2 = lambda i: (0, 0)
    col = lambda i: (0, i)
    sc1, sh1, g1, sc2, sh2 = mods
    res = lambda i: (i // per_b, 0, i % per_b, 0)
    attn_specs = [pl.BlockSpec((1, r, tm // r, GROUP_W), res) for _, r in ATTN_GROUPS]
    in_specs = ([pl.BlockSpec((tm, d), row)]
                + [pl.BlockSpec((1, 1, d), bat)] * 5
                + [pl.BlockSpec((1, d), full2)] * 2
                + [pl.BlockSpec(wg.shape, full2), pl.BlockSpec(wpa.shape, full2),
                   pl.BlockSpec(wpg.shape, full2), pl.BlockSpec(wo.shape, full2)]
                + attn_specs * 2
                + [pl.BlockSpec((tm, GMLP_W), row)]
                + [pl.BlockSpec(wr.shape, full2), pl.BlockSpec(br.shape, full2)])
    out_specs = [pl.BlockSpec((tm, d), row), pl.BlockSpec((tm, d), row),
                 pl.BlockSpec((TOP_K, tm), col), pl.BlockSpec((TOP_K, tm), col),
                 pl.BlockSpec((N_EXPERTS, 128), row)]
    out_shape = [jax.ShapeDtypeStruct((n, d), F32), jax.ShapeDtypeStruct((n, d), BF16),
                 jax.ShapeDtypeStruct((TOP_K, n), I32), jax.ShapeDtypeStruct((TOP_K, n), F32),
                 jax.ShapeDtypeStruct((n_tiles * N_EXPERTS, 128), I32)]
    return pl.pallas_call(
        _merge_kernel,
        grid=(n_tiles,),
        in_specs=in_specs,
        out_specs=out_specs,
        out_shape=out_shape,
        scratch_shapes=[pltpu.VMEM((_n_strided_groups() * 2 * (GROUP_W // 128), tm, 128), F32)],
        compiler_params=_ARB(1),
        name="merge_proj_route",
    )(h, sc1, sh1, g1, sc2, sh2, n1, n2, wg, wpa, wpg, wo, *outs, *lses, gm, wr, br)


def _row_copy(src, dst, sem):
    return pltpu.make_async_copy(src, dst, sem)


def _start_runs(tile, off_ref, start_ref, pieces_ref, copy):
    def per_expert(e, total):
        base = tile * N_EXPERTS + e
        local0, global0, n = start_ref[base], off_ref[base], pieces_ref[base]

        def long_piece(i, carry):
            row = pl.multiple_of(i * (ROW_ALIGN << _LOW_BITS), ROW_ALIGN)
            copy(pl.multiple_of(local0 + row, ROW_ALIGN),
                 pl.multiple_of(global0 + row, ROW_ALIGN), ROW_ALIGN << _LOW_BITS).start()
            return carry

        lax.fori_loop(0, lax.shift_right_logical(n, _LOW_BITS), long_piece, 0)
        for bit in reversed(range(_LOW_BITS)):
            rows = ROW_ALIGN << bit
            done = lax.shift_left(lax.shift_right_logical(n, bit + 1), bit + 1) * ROW_ALIGN

            @pl.when((lax.shift_right_logical(n, bit) & 1) == 1)
            def _():
                copy(pl.multiple_of(local0 + done, ROW_ALIGN),
                     pl.multiple_of(global0 + done, ROW_ALIGN), rows).start()

        return total + n

    return lax.fori_loop(0, N_EXPERTS, per_expert, 0, unroll=4)


def _wait_runs(total, copy):
    top = _UNIT_BITS - 1
    lax.fori_loop(0, lax.shift_right_logical(total, top),
                  lambda i, c: (copy(0, 0, ROW_ALIGN << top).wait(), c)[1], 0)
    for bit in reversed(range(top)):
        @pl.when((lax.shift_right_logical(total, bit) & 1) == 1)
        def _():
            copy(0, 0, ROW_ALIGN << bit).wait()


def _slot_hits(slot_ref, chunk, rows=SORT_CHUNK):
    tm = slot_ref.shape[1]
    s = lax.broadcasted_iota(I32, (rows, tm), 0) + chunk * rows
    return [slot_ref[k:k + 1, :] == s for k in range(TOP_K)]


def _dispatch_kernel(zb_ref, nz_ref, tail_off_ref, tail_pieces_ref, off_ref, start_ref, pieces_ref,
                     slot_ref, f_ref, buf_ref, sorted_ref, zero_ref, pending_ref, sems, zsem):
    step = pl.program_id(0)
    last_step = pl.num_programs(0) - 1
    buf_slot = lax.rem(step, 2)
    sorted_now = sorted_ref.at[buf_slot]

    def zero_block(j):
        start = pl.multiple_of(zb_ref[j] * EXPERT_BLOCK, EXPERT_BLOCK)
        return _row_copy(zero_ref, buf_ref.at[pl.ds(start, EXPERT_BLOCK)], zsem)

    def zero_piece(local_row, global_row, rows):
        del local_row
        return _row_copy(zero_ref.at[pl.ds(0, rows)], buf_ref.at[pl.ds(global_row, rows)], zsem)

    @pl.when(step == 0)
    def _():
        zero_ref[...] = jnp.zeros_like(zero_ref)
        lax.fori_loop(0, nz_ref[0], lambda j, c: (zero_block(j).start(), c)[1], 0)
        _start_runs(0, tail_off_ref, tail_off_ref, tail_pieces_ref, zero_piece)

    f = f_ref[...]
    for c in range(sorted_now.shape[0] // SORT_CHUNK):
        hits = _slot_hits(slot_ref, c)
        onehot = jnp.where(hits[0] | hits[1] | hits[2] | hits[3], 1.0, 0.0).astype(BF16)
        sorted_now[c * SORT_CHUNK:(c + 1) * SORT_CHUNK, :] = _bdot(onehot, f)

    def copy_from(slot):
        def copy(local_row, global_row, rows):
            return _row_copy(sorted_ref.at[slot, pl.ds(local_row, rows)],
                             buf_ref.at[pl.ds(global_row, rows)], sems.at[slot])
        return copy

    total = _start_runs(step, off_ref, start_ref, pieces_ref, copy_from(buf_slot))

    @pl.when(step > 0)
    def _():
        _wait_runs(pending_ref[0], copy_from(1 - buf_slot))

    pending_ref[0] = total

    @pl.when(step == last_step)
    def _():
        _wait_runs(total, copy_from(buf_slot))
        lax.fori_loop(0, nz_ref[0], lambda j, c: (zero_block(j).wait(), c)[1], 0)
        tail_total = lax.fori_loop(0, N_EXPERTS, lambda e, t: t + tail_pieces_ref[e], 0)
        _wait_runs(tail_total, zero_piece)


def _dispatch(plan, slots, f, n_blocks):
    n, d = f.shape
    tm = TOKEN_TILE
    return pl.pallas_call(
        _dispatch_kernel,
        grid_spec=pltpu.PrefetchScalarGridSpec(
            num_scalar_prefetch=7, grid=(n // tm,),
            in_specs=[pl.BlockSpec((TOP_K, tm), lambda i, *_: (0, i)),
                      pl.BlockSpec((tm, d), lambda i, *_: (i, 0))],
            out_specs=pl.BlockSpec(memory_space=pl.ANY),
            scratch_shapes=[pltpu.VMEM((2, _local_rows(tm), d), F32),
                            pltpu.VMEM((EXPERT_BLOCK, d), F32),
                            pltpu.SMEM((1,), I32),
                            pltpu.SemaphoreType.DMA((2,)), pltpu.SemaphoreType.DMA(())]),
        out_shape=jax.ShapeDtypeStruct((n_blocks * EXPERT_BLOCK, d), F32),
        compiler_params=_ARB(1),
        name="dispatch_rows",
    )(plan["zero_blocks"], plan["n_zero"], plan["tail_off"], plan["tail_pieces"],
      plan["run_off"], plan["run_start"], plan["run_pieces"], slots, f)


def _expert_kernel(be_ref, first_ref, rows_ref, x_ref, w1_ref, b1_ref, w2_ref, b2_ref, o_ref,
                   w1b_ref, w2p_ref, slab_ref):
    del be_ref
    step = pl.program_id(0)
    n_rows = rows_ref[step]
    live = n_rows > 0
    ff, d = w2p_ref.shape
    half = 64

    @pl.when(live & (first_ref[step] == 1))
    def _():
        w1b_ref[...] = w1_ref[0, 0].astype(BF16)
        for c in range(d // 128):
            cs = slice(c * 128, (c + 1) * 128)
            for k in range(ff // 128):
                r0 = k * 128
                slab_ref[pl.ds(r0, half, stride=2), :] = w2_ref[0, 0, r0:r0 + half, cs]
                slab_ref[pl.ds(r0 + 1, half, stride=2), :] = w2_ref[0, 0, r0 + half:r0 + 128, cs]
            w2p_ref[:, cs] = slab_ref[...].astype(BF16)

    def ffn(rs):
        x = x_ref[rs, :].astype(BF16)
        hb = _bdot(x, w1b_ref[...]) + b1_ref[0, 0]
        lane = lax.broadcasted_iota(I32, (x.shape[0], 128), 1)
        even = (lane & 1) == 0
        glu, lin = [], []
        for k in range(ff // 128):
            a = hb[:, 256 * k:256 * k + 128]
            b = hb[:, 256 * k + 128:256 * k + 256]
            glu.append(jnp.where(even, a, pltpu.roll(b, 1, 1)))
            lin.append(jnp.where(even, pltpu.roll(a, 127, 1), b))
        x_glu = jnp.minimum(jnp.concatenate(glu, axis=1), SWIGLU_LIMIT)
        x_lin = jnp.clip(jnp.concatenate(lin, axis=1), -SWIGLU_LIMIT, SWIGLU_LIMIT)
        act = x_glu * _sigmoid(SWIGLU_ALPHA * x_glu) * (x_lin + 1.0)
        o_ref[rs, :] = _bdot(act.astype(BF16), w2p_ref[...]) + b2_ref[0, 0]

    for r0 in range(0, EXPERT_BLOCK, EXPERT_SUB):
        rs = slice(r0, r0 + EXPERT_SUB)
        pl.when(n_rows > r0)(functools.partial(ffn, rs))

        @pl.when(n_rows <= r0)
        def _():
            o_ref[rs, :] = jnp.zeros((EXPERT_SUB, d), F32)


def _experts(layer, block_e, first, block_rows, buf, w1, b1, w2, b2, n_blocks):
    d = buf.shape[1]
    ff = w2.shape[2]
    xmap = lambda i, be, fi, rw: (jnp.where(rw[i] > 0, i, 0), 0)
    wmap = lambda i, be, fi, rw: (layer, be[i], 0, 0)
    return pl.pallas_call(
        _expert_kernel,
        grid_spec=pltpu.PrefetchScalarGridSpec(
            num_scalar_prefetch=3, grid=(n_blocks,),
            in_specs=[pl.BlockSpec((EXPERT_BLOCK, d), xmap),
                      pl.BlockSpec((1, 1, d, 2 * ff), wmap),
                      pl.BlockSpec((1, 1, 1, 2 * ff), wmap),
                      pl.BlockSpec((1, 1, ff, d), wmap),
                      pl.BlockSpec((1, 1, 1, d), wmap)],
            out_specs=pl.BlockSpec((EXPERT_BLOCK, d), lambda i, be, fi, rw: (i, 0)),
            scratch_shapes=[pltpu.VMEM((d, 2 * ff), BF16), pltpu.VMEM((ff, d), BF16),
                            pltpu.VMEM((ff, 128), F32)]),
        out_shape=jax.ShapeDtypeStruct((n_blocks * EXPERT_BLOCK, d), F32),
        compiler_params=_ARB(1),
        name="expert_ffn",
    )(block_e, first, block_rows, buf, w1, b1, w2, b2)


def _combine_kernel(off_ref, start_ref, pieces_ref, slot_ref, gate_ref, h_ref, g2_ref, fin_ref,
                    obuf_ref, out_ref, rows_ref, sems, *, final_norm):
    step = pl.program_id(0)
    buf_slot = lax.rem(step, 2)
    rows_now = rows_ref.at[buf_slot]

    def copy_into(slot):
        def copy(local_row, global_row, rows):
            return _row_copy(obuf_ref.at[pl.ds(global_row, rows)],
                             rows_ref.at[slot, pl.ds(local_row, rows)], sems.at[slot])
        return copy

    @pl.when(step == 0)
    def _():
        rows_ref[...] = jnp.zeros_like(rows_ref)
        _start_runs(0, off_ref, start_ref, pieces_ref, copy_into(0))

    @pl.when(step + 1 < pl.num_programs(0))
    def _():
        _start_runs(step + 1, off_ref, start_ref, pieces_ref, copy_into(1 - buf_slot))

    total = lax.fori_loop(0, N_EXPERTS,
                          lambda e, t: t + pieces_ref[step * N_EXPERTS + e], 0)
    _wait_runs(total, copy_into(buf_slot))

    y = jnp.zeros(h_ref.shape, F32)
    for c in range(rows_now.shape[0] // UNSORT_CHUNK):
        hits = _slot_hits(slot_ref, c, UNSORT_CHUNK)
        weight = jnp.where(hits[0], gate_ref[0:1, :], 0.0)
        for k in range(1, TOP_K):
            weight = weight + jnp.where(hits[k], gate_ref[k:k + 1, :], 0.0)
        rows = rows_now[c * UNSORT_CHUNK:(c + 1) * UNSORT_CHUNK, :].astype(BF16)
        y = y + lax.dot_general(weight.astype(BF16), rows, (((0,), (0,)), ((), ())),
                                preferred_element_type=F32)
    hn = h_ref[...] + g2_ref[0] * y
    if final_norm:
        ms = jnp.mean(hn * hn, axis=-1, keepdims=True)
        hn = hn * lax.rsqrt(ms + RMS_EPS) * fin_ref[...]
    out_ref[...] = hn


def _combine(plan, slots, gate, h, g2, final_g, obuf, seq, final_norm):
    n, d = h.shape
    tm = TOKEN_TILE
    per_b = seq // tm
    return pl.pallas_call(
        functools.partial(_combine_kernel, final_norm=final_norm),
        grid_spec=pltpu.PrefetchScalarGridSpec(
            num_scalar_prefetch=3, grid=(n // tm,),
            in_specs=[pl.BlockSpec((TOP_K, tm), lambda i, *_: (0, i)),
                      pl.BlockSpec((TOP_K, tm), lambda i, *_: (0, i)),
                      pl.BlockSpec((tm, d), lambda i, *_: (i, 0)),
                      pl.BlockSpec((1, 1, d), lambda i, *_: (i // per_b, 0, 0)),
                      pl.BlockSpec((1, d), lambda i, *_: (0, 0)),
                      pl.BlockSpec(memory_space=pl.ANY)],
            out_specs=pl.BlockSpec((tm, d), lambda i, *_: (i, 0)),
            scratch_shapes=[pltpu.VMEM((2, _local_rows(tm), d), F32),
                            pltpu.SemaphoreType.DMA((2,))]),
        out_shape=jax.ShapeDtypeStruct((n, d), F32),
        compiler_params=_ARB(1),
        name="combine_rows",
    )(plan["run_off"], plan["run_start"], plan["run_pieces"], slots, gate, h, g2, final_g, obuf)


def _local_rows(tm):
    rows = tm * TOP_K + N_EXPERTS * (ROW_ALIGN - 1)
    return -(-rows // SORT_CHUNK) * SORT_CHUNK


def _num_blocks(n):
    rows = n * TOP_K + (n // TOKEN_TILE) * N_EXPERTS * (ROW_ALIGN - 1)
    return -(-rows // EXPERT_BLOCK) + N_EXPERTS


def _slot_plan(tile_counts, n):
    blk = EXPERT_BLOCK
    n_blocks = _num_blocks(n)
    cnt = (tile_counts + ROW_ALIGN - 1) // ROW_ALIGN * ROW_ALIGN
    total = jnp.sum(cnt, axis=0)
    padded = (total + blk - 1) // blk * blk
    pad_end = jnp.cumsum(padded)
    pad_start = pad_end - padded
    run_off = pad_start[None, :] + jnp.cumsum(cnt, axis=0) - cnt
    run_start = jnp.cumsum(cnt, axis=1) - cnt
    n_valid = pad_end[-1] // blk
    block_start = jnp.arange(n_blocks, dtype=I32) * blk
    block_e = jnp.sum(pad_end[None, :] <= block_start[:, None], axis=1)
    last_e = jnp.sum(pad_end <= (n_valid - 1) * blk)
    block_e = jnp.where(jnp.arange(n_blocks) < n_valid, block_e, last_e).astype(I32)
    first = jnp.concatenate([jnp.ones((1,), I32),
                             (block_e[1:] != block_e[:-1]).astype(I32)])
    hit = jnp.arange(N_EXPERTS)[None, :] == block_e[:, None]
    data_end = jnp.sum(jnp.where(hit, pad_start + total, 0), axis=1)
    block_rows = jnp.where(jnp.arange(n_blocks) < n_valid,
                           jnp.clip(data_end - block_start, 0, blk), 0).astype(I32)
    flat = lambda a: a.reshape(-1).astype(I32)
    zero_blocks = n_valid + jnp.arange(n_blocks - (n * TOP_K) // blk)
    return dict(block_e=block_e, first=first, block_rows=block_rows,
                zero_blocks=flat(jnp.minimum(zero_blocks, n_blocks - 1)),
                n_zero=(n_blocks - n_valid).reshape(1).astype(I32),
                tail_off=flat(pad_start + total), tail_pieces=flat((padded - total) // ROW_ALIGN),
                run_off=flat(run_off), run_start=flat(run_start),
                run_pieces=flat(cnt // ROW_ALIGN))


def kernel(x, c, positions, w_ada, b_ada, norm1_g, w_in, w_s, b_s, ln_g, ln_b, w_pa, w_pg,
           w_o, norm2_g, w_router, b_router, w1, b1, w2, b2, final_g):
    batch, seq, d = x.shape
    depth = w_ada.shape[0]
    n = batch * seq
    n_blocks = _num_blocks(n)

    mod = _modulation(c, w_ada, b_ada)
    cos, sin = _rope_tables(positions)

    n_mix = 3 * ATTN_W + 2 * GMLP_W
    b1r = b1.reshape(depth, N_EXPERTS, 1, -1)
    b2r = b2.reshape(depth, N_EXPERTS, 1, -1)

    h = x.reshape(n, d)
    for l in range(depth):
        ml = mod[l].reshape(batch, 6, 1, d)
        sh1, sc1, g1, sh2, sc2, g2 = [ml[:, i] for i in range(6)]
        w_mix = w_in[l, :, :n_mix].astype(BF16)
        w_gate = w_in[l, :, n_mix:].astype(BF16)

        *qkvs, gm = _inproj(h, sc1, sh1, norm1_g[l].reshape(1, d), w_mix, cos, sin,
                            w_s[l], b_s[l].T, ln_g[l].reshape(1, -1), ln_b[l].reshape(1, -1), seq)
        outs, lses = zip(*[_attention_group(qkvs[g], g) for g in range(N_GROUPS)])

        h, f, slots, gate, counts = _merge_route(
            h, (sc1, sh1, g1, sc2, sh2), norm1_g[l].reshape(1, d), norm2_g[l].reshape(1, d),
            w_gate, w_pa[l].astype(BF16), w_pg[l].astype(BF16), w_o[l].astype(BF16),
            outs, lses, gm, w_router[l].T, b_router[l].reshape(-1, 1), seq)

        plan = _slot_plan(counts[:, 0].reshape(-1, N_EXPERTS), n)
        buf = _dispatch(plan, slots, f, n_blocks)
        obuf = _experts(l, plan["block_e"], plan["first"], plan["block_rows"], buf,
                        w1, b1r, w2, b2r, n_blocks)

        h = _combine(plan, slots, gate, h, g2, final_g.reshape(1, d), obuf, seq,
                     final_norm=(l == depth - 1))
    return h.reshape(batch, seq, d)
```

```python
import functools

import jax
import jax.numpy as jnp
from jax import lax
from jax.experimental import pallas as pl
from jax.experimental.pallas import tpu as pltpu

F32 = jnp.float32
BF16 = jnp.bfloat16
I32 = jnp.int32

HEAD_DIM = 64
HEADS_PER_GROUP = 4
GROUP_W = HEADS_PER_GROUP * HEAD_DIM
ATTN_GROUPS = ((128, 1), (512, 4), (2048, 16))
N_GROUPS = len(ATTN_GROUPS)
ATTN_W = N_GROUPS * GROUP_W
WIN_BLOCK = 128
ROPE_THETA = 10000.0
GMLP_CHUNK = 128
GMLP_GROUPS = 4
GMLP_W = GMLP_GROUPS * 128
N_EXPERTS = 32
TOP_K = 4
SWIGLU_ALPHA = 1.702
SWIGLU_LIMIT = 7.0
RMS_EPS = 1e-5
LN_EPS = 1e-5
NEG_INF = -1e30

TOKEN_TILE = 512
PROJ_TILE = 1024
MERGE_COLS = 256
ATTN_STEP_ROWS = 4096
EXPERT_BLOCK = 512
EXPERT_SUB = 512
ROW_ALIGN = 8
_MAX_UNITS = max(TOKEN_TILE, EXPERT_BLOCK) // ROW_ALIGN
_UNIT_BITS = _MAX_UNITS.bit_length()
_LOW_BITS = 4
SORT_CHUNK = 256
UNSORT_CHUNK = 256
VMEM_LIMIT = 56 * 1024 * 1024

def _ARB(n, **kw):
    return pltpu.CompilerParams(dimension_semantics=("arbitrary",) * n,
                                vmem_limit_bytes=VMEM_LIMIT, **kw)


def _bdot(a, b):
    return jnp.dot(a, b, preferred_element_type=F32)


def _sigmoid(x):
    return 0.5 * jnp.tanh(0.5 * x) + 0.5


def _n_strided_groups():
    return sum(1 for _, r in ATTN_GROUPS if r > 1)


def _split3(x):
    hi = x.astype(BF16)
    r1 = x - hi.astype(F32)
    mid = r1.astype(BF16)
    lo = (r1 - mid.astype(F32)).astype(BF16)
    return hi, mid, lo


def _dot_f32(a, b):
    a0, a1, a2 = _split3(a)
    b0, b1, b2 = _split3(b)
    small = _bdot(a0, b2) + _bdot(a1, b1) + _bdot(a2, b0)
    mid = _bdot(a0, b1) + _bdot(a1, b0)
    return (small + mid) + _bdot(a0, b0)


def _mod_kernel(c_ref, w_ref, b_ref, o_ref):
    c = c_ref[...]
    s = c * jax.nn.sigmoid(c)
    o_ref[0] = _dot_f32(s, w_ref[0]) + b_ref[0]


def _modulation(c, w_ada, b_ada):
    depth, d, six_d = w_ada.shape
    b = c.shape[0]
    n_col = six_d // d
    return pl.pallas_call(
        _mod_kernel,
        grid=(depth, n_col),
        in_specs=[pl.BlockSpec((b, d), lambda l, j: (0, 0)),
                  pl.BlockSpec((1, d, d), lambda l, j: (l, 0, j)),
                  pl.BlockSpec((1, 1, d), lambda l, j: (l, 0, j))],
        out_specs=pl.BlockSpec((1, b, d), lambda l, j: (l, 0, j)),
        out_shape=jax.ShapeDtypeStruct((depth, b, six_d), F32),
        compiler_params=_ARB(2),
        name="adaln_mod",
    )(c, w_ada, b_ada.reshape(depth, 1, six_d))


def _rope_kernel(pos_ref, freq_ref, sign_ref, cos_ref, sin_ref):
    ang = pos_ref[...].astype(F32) * freq_ref[...]
    cos_ref[...] = jnp.cos(ang)
    sin_ref[...] = jnp.sin(ang) * sign_ref[...]


def _rope_tables(positions):
    n = positions.size
    half = HEAD_DIM // 2
    inv_freq = ROPE_THETA ** (-jnp.arange(half, dtype=F32) / half)
    freq = jnp.tile(inv_freq, 4).reshape(1, 128)
    sign = jnp.tile(jnp.concatenate([-jnp.ones((half,), F32), jnp.ones((half,), F32)]), 2)
    sign = sign.reshape(1, 128)
    tile = 1024
    return pl.pallas_call(
        _rope_kernel,
        grid=(n // tile,),
        in_specs=[pl.BlockSpec((tile, 1), lambda i: (i, 0)),
                  pl.BlockSpec((1, 128), lambda i: (0, 0)),
                  pl.BlockSpec((1, 128), lambda i: (0, 0))],
        out_specs=[pl.BlockSpec((tile, 128), lambda i: (i, 0))] * 2,
        out_shape=[jax.ShapeDtypeStruct((n, 128), F32)] * 2,
        compiler_params=_ARB(1),
        name="rope_tables",
    )(positions.reshape(n, 1), freq, sign)


def _modulated_norm(x, g, sc, sh):
    ms = jnp.mean(x * x, axis=-1, keepdims=True)
    y = x * lax.rsqrt(ms + RMS_EPS) * g
    return y * (1.0 + sc) + sh


def _inproj_kernel(h_ref, sc_ref, sh_ref, g_ref, w_ref, cos_ref, sin_ref,
                   ws_ref, bs_ref, lng_ref, lnb_ref, qkv0_ref, qkv1_ref, qkv2_ref, gm_ref,
                   slab_ref):
    tm = h_ref.shape[0]
    a = _modulated_norm(h_ref[...], g_ref[...], sc_ref[0], sh_ref[0]).astype(BF16)
    cos = cos_ref[...]
    sin = sin_ref[...]
    lane = lax.broadcasted_iota(I32, (tm, 128), 1)
    first_half = (lane % HEAD_DIM) < (HEAD_DIM // 2)

    def rope(x):
        swapped = jnp.where(first_half, pltpu.roll(x, 96, 1), pltpu.roll(x, 32, 1))
        return x * cos + swapped * sin

    slab = 0
    for g, out_ref in enumerate((qkv0_ref, qkv1_ref, qkv2_ref)):
        r = ATTN_GROUPS[g][1]
        for part in range(3):
            c0 = part * ATTN_W + g * GROUP_W
            y = _bdot(a, w_ref[:, c0:c0 + GROUP_W])
            halves = [y[:, :128], y[:, 128:]]
            if part < 2:
                halves = [rope(x) for x in halves]
            if part == 0:
                halves = [x * (HEAD_DIM ** -0.5) for x in halves]
            for c, x in enumerate(halves):
                o0 = part * GROUP_W + c * 128
                if r == 1:
                    out_ref[0, 0, :, o0:o0 + 128] = x.astype(BF16)
                else:
                    slab_ref[slab] = x
                    for j in range(r):
                        rows = slab_ref[slab, pl.ds(j, tm // r, stride=r), :]
                        out_ref[0, j, :, o0:o0 + 128] = rows.astype(BF16)
                    slab += 1

    c0 = 3 * ATTN_W
    u = jax.nn.gelu(_bdot(a, w_ref[:, c0:c0 + GMLP_W]))
    v = jax.nn.gelu(_bdot(a, w_ref[:, c0 + GMLP_W:c0 + 2 * GMLP_W]))
    mu = jnp.mean(v, axis=-1, keepdims=True)
    var = jnp.mean(jnp.square(v - mu), axis=-1, keepdims=True)
    v = ((v - mu) * lax.rsqrt(var + LN_EPS) * lng_ref[...] + lnb_ref[...]).astype(BF16)
    row = lax.broadcasted_iota(I32, (GMLP_CHUNK, GMLP_CHUNK), 0)
    col = lax.broadcasted_iota(I32, (GMLP_CHUNK, GMLP_CHUNK), 1)
    tril = col <= row
    chunks = [slice(c * GMLP_CHUNK, (c + 1) * GMLP_CHUNK) for c in range(tm // GMLP_CHUNK)]
    for g in range(GMLP_GROUPS):
        wsg = jnp.where(tril, ws_ref[g], 0.0).astype(BF16)
        cs = slice(g * 128, (g + 1) * 128)
        mixed = _bdot(wsg, jnp.concatenate([v[rs, cs] for rs in chunks], axis=1))
        mixed = mixed + bs_ref[:, g:g + 1]
        for c, rs in enumerate(chunks):
            gm_ref[rs, cs] = (u[rs, cs] * mixed[:, c * 128:(c + 1) * 128]).astype(BF16)


def _inproj(h, sc, sh, g, w, cos, sin, w_s, b_s_t, ln_g, ln_b, seq):
    n, d = h.shape
    tm = min(PROJ_TILE, seq)
    per_b = seq // tm
    batch = n // seq
    wn = w.shape[1]
    row = lambda i: (i, 0)
    bat = lambda i: (i // per_b, 0, 0)
    full2 = lambda i: (0, 0)
    res = lambda i: (i // per_b, 0, i % per_b, 0)
    qkv_specs = [pl.BlockSpec((1, r, tm // r, 3 * GROUP_W), res) for _, r in ATTN_GROUPS]
    qkv_shapes = [jax.ShapeDtypeStruct((batch, r, seq // r, 3 * GROUP_W), BF16)
                  for _, r in ATTN_GROUPS]
    return pl.pallas_call(
        _inproj_kernel,
        grid=(n // tm,),
        in_specs=[pl.BlockSpec((tm, d), row),
                  pl.BlockSpec((1, 1, d), bat),
                  pl.BlockSpec((1, 1, d), bat),
                  pl.BlockSpec((1, d), full2),
                  pl.BlockSpec((d, wn), full2),
                  pl.BlockSpec((tm, 128), row),
                  pl.BlockSpec((tm, 128), row),
                  pl.BlockSpec((GMLP_GROUPS, GMLP_CHUNK, GMLP_CHUNK), lambda i: (0, 0, 0)),
                  pl.BlockSpec((GMLP_CHUNK, GMLP_GROUPS), full2),
                  pl.BlockSpec((1, GMLP_W), full2),
                  pl.BlockSpec((1, GMLP_W), full2)],
        out_specs=qkv_specs + [pl.BlockSpec((tm, GMLP_W), row)],
        out_shape=qkv_shapes + [jax.ShapeDtypeStruct((n, GMLP_W), BF16)],
        scratch_shapes=[pltpu.VMEM((_n_strided_groups() * 3 * (GROUP_W // 128), tm, 128), F32)],
        compiler_params=_ARB(1),
        name="inproj_rope_gmlp",
    )(h, sc, sh, g, w, cos, sin, w_s, b_s_t, ln_g, ln_b)


def _attn_kernel(q_ref, k_ref, v_ref, kp_ref, vp_ref, o_ref, lse_ref):
    q_ref, k_ref, v_ref, kp_ref, vp_ref, o_ref, lse_ref = (
        x.at[0] for x in (q_ref, k_ref, v_ref, kp_ref, vp_ref, o_ref, lse_ref))
    qb = q_ref.shape[1]
    first_key = jnp.where(pl.program_id(2) == 0, WIN_BLOCK, 0)
    qi = lax.broadcasted_iota(I32, (WIN_BLOCK, 2 * WIN_BLOCK), 0)
    kj = lax.broadcasted_iota(I32, (WIN_BLOCK, 2 * WIN_BLOCK), 1)
    band = (kj >= qi) & (kj <= qi + WIN_BLOCK)
    low_lanes = lax.broadcasted_iota(I32, (WIN_BLOCK, 128), 1) < HEAD_DIM
    ones = jnp.ones((2 * WIN_BLOCK, 128), BF16)
    for res, i in [(res, i) for res in range(q_ref.shape[0]) for i in range(qb // WIN_BLOCK)]:
        rs = slice(i * WIN_BLOCK, (i + 1) * WIN_BLOCK)
        q = q_ref[res, rs, :]
        if i == 0:
            k_prev, v_prev = kp_ref[res], vp_ref[res]
            valid = band & (kj >= first_key)
        else:
            ps = slice((i - 1) * WIN_BLOCK, i * WIN_BLOCK)
            k_prev, v_prev = k_ref[res, ps, :], v_ref[res, ps, :]
            valid = band
        k_cat = jnp.concatenate([k_prev, k_ref[res, rs, :]], axis=0)
        v_cat = jnp.concatenate([v_prev, v_ref[res, rs, :]], axis=0)
        outs, lses = [], []
        for pair in range(GROUP_W // 128):
            cs = slice(pair * 128, (pair + 1) * 128)
            q_pair = q[:, cs].astype(F32)
            k_pair = k_cat[:, cs]
            v_ones = jnp.concatenate([v_cat[:, cs], ones], axis=1)
            o_half, l_half = [], []
            for head_lanes in (low_lanes, jnp.logical_not(low_lanes)):
                q_head = jnp.where(head_lanes, q_pair, 0.0).astype(BF16)
                s = lax.dot_general(q_head, k_pair, (((1,), (1,)), ((), ())),
                                    preferred_element_type=F32)
                s = jnp.where(valid, s, NEG_INF)
                m = jnp.max(s, axis=-1, keepdims=True)
                p = jnp.exp(s - m).astype(BF16)
                pv = _bdot(p, v_ones)
                den = pv[:, 128:129]
                o_half.append(pv[:, :128] / den)
                l_half.append(jnp.broadcast_to(m + jnp.log(den), (WIN_BLOCK, 128)))
            outs.append(jnp.where(low_lanes, o_half[0], o_half[1]))
            lses.append(jnp.where(low_lanes, l_half[0], l_half[1]))
        o_ref[res, rs, :] = jnp.concatenate(outs, axis=1).astype(o_ref.dtype)
        lse_ref[res, rs, :] = jnp.concatenate(lses, axis=1)


def _attention_group(qkv, g):
    batch, r, sub_len, _ = qkv.shape
    qb = min(ATTN_STEP_ROWS, sub_len)
    n_res = min(ATTN_STEP_ROWS // qb, r)
    n_tiles = sub_len // qb
    per_tile = qb // WIN_BLOCK

    def cur(part):
        return pl.BlockSpec((1, n_res, qb, GROUP_W), lambda b, j, n: (b, j, n, part))

    def prev(part):
        return pl.BlockSpec((1, n_res, WIN_BLOCK, GROUP_W),
                            lambda b, j, n: (b, j, jnp.maximum(n * per_tile - 1, 0), part))

    out_spec = pl.BlockSpec((1, n_res, qb, GROUP_W), lambda b, j, n: (b, j, n, 0))
    return pl.pallas_call(
        _attn_kernel,
        grid=(batch, r // n_res, n_tiles),
        in_specs=[cur(0), cur(1), cur(2), prev(1), prev(2)],
        out_specs=[out_spec, out_spec],
        out_shape=[jax.ShapeDtypeStruct((batch, r, sub_len, GROUP_W), BF16),
                   jax.ShapeDtypeStruct((batch, r, sub_len, GROUP_W), F32)],
        compiler_params=_ARB(3),
        name=f"dilated_attn_g{g}",
    )(qkv, qkv, qkv, qkv, qkv)


def _split_bf16(x):
    hi = x.astype(BF16)
    lo = (x - hi.astype(F32)).astype(BF16)
    return hi, lo


def _merge_kernel(h_ref, sc1_ref, sh1_ref, g1_ref, sc2_ref, sh2_ref, n1_ref, n2_ref,
                  wg_ref, wpa_ref, wpg_ref, wo_ref,
                  o0_ref, o1_ref, o2_ref, l0_ref, l1_ref, l2_ref, gm_ref,
                  wr_ref, br_ref,
                  hn_ref, f_ref, slot_ref, gate_ref, cnt_ref, slab_ref):
    tm, d = h_ref.shape
    slabs = iter(range(slab_ref.shape[0]))

    def token_major(ref):
        r = ref.shape[1]
        if r == 1:
            return ref[0, 0].astype(F32)
        halves = []
        for c in range(GROUP_W // 128):
            slab = next(slabs)
            for j in range(r):
                slab_ref[slab, pl.ds(j, tm // r, stride=r), :] = (
                    ref[0, j, :, c * 128:(c + 1) * 128].astype(F32))
            halves.append(slab_ref[slab])
        return jnp.concatenate(halves, axis=1)

    lses = [token_major(x) for x in (l0_ref, l1_ref, l2_ref)]
    outs = [token_major(x) for x in (o0_ref, o1_ref, o2_ref)]

    h = h_ref[...]
    a = _modulated_norm(h, n1_ref[...], sc1_ref[0], sh1_ref[0]).astype(BF16)

    m = jnp.maximum(jnp.maximum(lses[0], lses[1]), lses[2])
    es = [jnp.exp(l - m) for l in lses]
    den = es[0] + es[1] + es[2]
    attn = ((es[0] * outs[0] + es[1] * outs[1] + es[2] * outs[2]) / den).astype(BF16)
    gm = gm_ref[...]

    merged = []
    for c0 in range(0, d, MERGE_COLS):
        cs = slice(c0, c0 + MERGE_COLS)
        gate_a = _sigmoid(_bdot(a, wg_ref[:, cs]))
        gate_g = _sigmoid(_bdot(a, wg_ref[:, d + c0:d + c0 + MERGE_COLS]))
        merged.append((gate_a * _bdot(attn, wpa_ref[:, cs])
                       + gate_g * _bdot(gm, wpg_ref[:, cs])).astype(BF16))
    hn = h + g1_ref[0] * _bdot(jnp.concatenate(merged, axis=1), wo_ref[...])
    hn_ref[...] = hn

    f = _modulated_norm(hn, n2_ref[...], sc2_ref[0], sh2_ref[0])
    f_ref[...] = f.astype(BF16)
    _route_tile(f, wr_ref, br_ref, slot_ref, gate_ref, cnt_ref)


def _route_tile(f, wr_ref, br_ref, slot_ref, gate_ref, cnt_ref):
    tm = f.shape[0]
    f_hi, f_lo = _split_bf16(f)
    w_hi, w_lo = _split_bf16(wr_ref[...])
    nt = (((1,), (1,)), ((), ()))
    logits = (lax.dot_general(w_hi, f_hi, nt, preferred_element_type=F32)
              + lax.dot_general(w_hi, f_lo, nt, preferred_element_type=F32)
              + lax.dot_general(w_lo, f_hi, nt, preferred_element_type=F32)) + br_ref[...]

    n_e = logits.shape[0]
    erow = lax.broadcasted_iota(I32, (n_e, tm), 0).astype(F32)
    vals, idxs, sels = [], [], []
    cur = logits
    for _ in range(TOP_K):
        mx = jnp.max(cur, axis=0, keepdims=True)
        ix = jnp.min(jnp.where(cur == mx, erow, float(n_e)), axis=0, keepdims=True)
        sel = erow == ix
        vals.append(mx)
        idxs.append(ix)
        sels.append(sel)
        cur = jnp.where(sel, -jnp.inf, cur)
    exps = [jnp.exp(v - vals[0]) for v in vals]
    tot = exps[0] + exps[1] + exps[2] + exps[3]
    gate_ref[...] = jnp.concatenate([e / tot for e in exps], axis=0)

    onehot = (sels[0] | sels[1] | sels[2] | sels[3])
    oh = jnp.where(onehot, 1.0, 0.0)
    t_r = lax.broadcasted_iota(I32, (tm, tm), 0)
    t_c = lax.broadcasted_iota(I32, (tm, tm), 1)
    upper = jnp.where(t_r < t_c, 1.0, 0.0).astype(BF16)
    prefix = _bdot(oh.astype(BF16), upper)
    cnt = jnp.sum(oh, axis=1, keepdims=True)
    cnt_al = jnp.floor((cnt + (ROW_ALIGN - 1)) * (1.0 / ROW_ALIGN)) * ROW_ALIGN
    e_r = lax.broadcasted_iota(I32, (n_e, n_e), 0)
    e_c = lax.broadcasted_iota(I32, (n_e, n_e), 1)
    lower = jnp.where(e_c < e_r, 1.0, 0.0).astype(BF16)
    start = _bdot(lower, jnp.broadcast_to(cnt_al, (n_e, 128)).astype(BF16))[:, 0:1]
    local = prefix + start
    slots = [jnp.sum(jnp.where(s, local, 0.0), axis=0, keepdims=True) for s in sels]
    slot_ref[...] = jnp.concatenate(slots, axis=0).astype(I32)
    cnt_ref[...] = jnp.broadcast_to(cnt, cnt_ref.shape).astype(I32)


def _merge_route(h, mods, n1, n2, wg, wpa, wpg, wo, outs, lses, gm, wr, br, seq):
    n, d = h.shape
    tm = TOKEN_TILE
    per_b = seq // tm
    n_tiles = n // tm
    row = lambda i: (i, 0)
    bat = lambda i: (i // per_b, 0, 0)
    full2 = lambda i: (0, 0)
    col = lambda i: (0, i)
    sc1, sh1, g1, sc2, sh2 = mods
    res = lambda i: (i // per_b, 0, i % per_b, 0)
    attn_specs = [pl.BlockSpec((1, r, tm // r, GROUP_W), res) for _, r in ATTN_GROUPS]
    in_specs = ([pl.BlockSpec((tm, d), row)]
                + [pl.BlockSpec((1, 1, d), bat)] * 5
                + [pl.BlockSpec((1, d), full2)] * 2
                + [pl.BlockSpec(wg.shape, full2), pl.BlockSpec(wpa.shape, full2),
                   pl.BlockSpec(wpg.shape, full2), pl.BlockSpec(wo.shape, full2)]
                + attn_specs * 2
                + [pl.BlockSpec((tm, GMLP_W), row)]
                + [pl.BlockSpec(wr.shape, full2), pl.BlockSpec(br.shape, full2)])
    out_specs = [pl.BlockSpec((tm, d), row), pl.BlockSpec((tm, d), row),
                 pl.BlockSpec((TOP_K, tm), col), pl.BlockSpec((TOP_K, tm), col),
                 pl.BlockSpec((N_EXPERTS, 128), row)]
    out_shape = [jax.ShapeDtypeStruct((n, d), F32), jax.ShapeDtypeStruct((n, d), BF16),
                 jax.ShapeDtypeStruct((TOP_K, n), I32), jax.ShapeDtypeStruct((TOP_K, n), F32),
                 jax.ShapeDtypeStruct((n_tiles * N_EXPERTS, 128), I32)]
    return pl.pallas_call(
        _merge_kernel,
        grid=(n_tiles,),
        in_specs=in_specs,
        out_specs=out_specs,
        out_shape=out_shape,
        scratch_shapes=[pltpu.VMEM((_n_strided_groups() * 2 * (GROUP_W // 128), tm, 128), F32)],
        compiler_params=_ARB(1),
        name="merge_proj_route",
    )(h, sc1, sh1, g1, sc2, sh2, n1, n2, wg, wpa, wpg, wo, *outs, *lses, gm, wr, br)


def _row_copy(src, dst, sem):
    return pltpu.make_async_copy(src, dst, sem)


def _start_runs(tile, off_ref, start_ref, pieces_ref, copy):
    def per_expert(e, total):
        base = tile * N_EXPERTS + e
        local0, global0, n = start_ref[base], off_ref[base], pieces_ref[base]

        def long_piece(i, carry):
            row = pl.multiple_of(i * (ROW_ALIGN << _LOW_BITS), ROW_ALIGN)
            copy(pl.multiple_of(local0 + row, ROW_ALIGN),
                 pl.multiple_of(global0 + row, ROW_ALIGN), ROW_ALIGN << _LOW_BITS).start()
            return carry

        lax.fori_loop(0, lax.shift_right_logical(n, _LOW_BITS), long_piece, 0)
        for bit in reversed(range(_LOW_BITS)):
            rows = ROW_ALIGN << bit
            done = lax.shift_left(lax.shift_right_logical(n, bit + 1), bit + 1) * ROW_ALIGN

            @pl.when((lax.shift_right_logical(n, bit) & 1) == 1)
            def _():
                copy(pl.multiple_of(local0 + done, ROW_ALIGN),
                     pl.multiple_of(global0 + done, ROW_ALIGN), rows).start()

        return total + n

    return lax.fori_loop(0, N_EXPERTS, per_expert, 0, unroll=4)


def _wait_runs(total, copy):
    top = _UNIT_BITS - 1
    lax.fori_loop(0, lax.shift_right_logical(total, top),
                  lambda i, c: (copy(0, 0, ROW_ALIGN << top).wait(), c)[1], 0)
    for bit in reversed(range(top)):
        @pl.when((lax.shift_right_logical(total, bit) & 1) == 1)
        def _():
            copy(0, 0, ROW_ALIGN << bit).wait()


def _slot_hits(slot_ref, chunk, rows=SORT_CHUNK):
    tm = slot_ref.shape[1]
    s = lax.broadcasted_iota(I32, (rows, tm), 0) + chunk * rows
    return [slot_ref[k:k + 1, :] == s for k in range(TOP_K)]


def _dispatch_kernel(zb_ref, nz_ref, tail_off_ref, tail_pieces_ref, off_ref, start_ref, pieces_ref,
                     slot_ref, f_ref, buf_ref, sorted_ref, zero_ref, pending_ref, sems, zsem):
    step = pl.program_id(0)
    last_step = pl.num_programs(0) - 1
    buf_slot = lax.rem(step, 2)
    sorted_now = sorted_ref.at[buf_slot]

    def zero_block(j):
        start = pl.multiple_of(zb_ref[j] * EXPERT_BLOCK, EXPERT_BLOCK)
        return _row_copy(zero_ref, buf_ref.at[pl.ds(start, EXPERT_BLOCK)], zsem)

    def zero_piece(local_row, global_row, rows):
        del local_row
        return _row_copy(zero_ref.at[pl.ds(0, rows)], buf_ref.at[pl.ds(global_row, rows)], zsem)

    @pl.when(step == 0)
    def _():
        zero_ref[...] = jnp.zeros_like(zero_ref)
        lax.fori_loop(0, nz_ref[0], lambda j, c: (zero_block(j).start(), c)[1], 0)
        _start_runs(0, tail_off_ref, tail_off_ref, tail_pieces_ref, zero_piece)

    f = f_ref[...]
    for c in range(sorted_now.shape[0] // SORT_CHUNK):
        hits = _slot_hits(slot_ref, c)
        onehot = jnp.where(hits[0] | hits[1] | hits[2] | hits[3], 1.0, 0.0).astype(BF16)
        sorted_now[c * SORT_CHUNK:(c + 1) * SORT_CHUNK, :] = _bdot(onehot, f)

    def copy_from(slot):
        def copy(local_row, global_row, rows):
            return _row_copy(sorted_ref.at[slot, pl.ds(local_row, rows)],
                             buf_ref.at[pl.ds(global_row, rows)], sems.at[slot])
        return copy

    total = _start_runs(step, off_ref, start_ref, pieces_ref, copy_from(buf_slot))

    @pl.when(step > 0)
    def _():
        _wait_runs(pending_ref[0], copy_from(1 - buf_slot))

    pending_ref[0] = total

    @pl.when(step == last_step)
    def _():
        _wait_runs(total, copy_from(buf_slot))
        lax.fori_loop(0, nz_ref[0], lambda j, c: (zero_block(j).wait(), c)[1], 0)
        tail_total = lax.fori_loop(0, N_EXPERTS, lambda e, t: t + tail_pieces_ref[e], 0)
        _wait_runs(tail_total, zero_piece)


def _dispatch(plan, slots, f, n_blocks):
    n, d = f.shape
    tm = TOKEN_TILE
    return pl.pallas_call(
        _dispatch_kernel,
        grid_spec=pltpu.PrefetchScalarGridSpec(
            num_scalar_prefetch=7, grid=(n // tm,),
            in_specs=[pl.BlockSpec((TOP_K, tm), lambda i, *_: (0, i)),
                      pl.BlockSpec((tm, d), lambda i, *_: (i, 0))],
            out_specs=pl.BlockSpec(memory_space=pl.ANY),
            scratch_shapes=[pltpu.VMEM((2, _local_rows(tm), d), F32),
                            pltpu.VMEM((EXPERT_BLOCK, d), F32),
                            pltpu.SMEM((1,), I32),
                            pltpu.SemaphoreType.DMA((2,)), pltpu.SemaphoreType.DMA(())]),
        out_shape=jax.ShapeDtypeStruct((n_blocks * EXPERT_BLOCK, d), F32),
        compiler_params=_ARB(1),
        name="dispatch_rows",
    )(plan["zero_blocks"], plan["n_zero"], plan["tail_off"], plan["tail_pieces"],
      plan["run_off"], plan["run_start"], plan["run_pieces"], slots, f)


def _expert_kernel(be_ref, first_ref, rows_ref, x_ref, w1_ref, b1_ref, w2_ref, b2_ref, o_ref,
                   w1b_ref, w2p_ref, slab_ref):
    del be_ref
    step = pl.program_id(0)
    n_rows = rows_ref[step]
    live = n_rows > 0
    ff, d = w2p_ref.shape
    half = 64

    @pl.when(live & (first_ref[step] == 1))
    def _():
        w1b_ref[...] = w1_ref[0, 0].astype(BF16)
        for c in range(d // 128):
            cs = slice(c * 128, (c + 1) * 128)
            for k in range(ff // 128):
                r0 = k * 128
                slab_ref[pl.ds(r0, half, stride=2), :] = w2_ref[0, 0, r0:r0 + half, cs]
                slab_ref[pl.ds(r0 + 1, half, stride=2), :] = w2_ref[0, 0, r0 + half:r0 + 128, cs]
            w2p_ref[:, cs] = slab_ref[...].astype(BF16)

    def ffn(rs):
        x = x_ref[rs, :].astype(BF16)
        hb = _bdot(x, w1b_ref[...]) + b1_ref[0, 0]
        lane = lax.broadcasted_iota(I32, (x.shape[0], 128), 1)
        even = (lane & 1) == 0
        glu, lin = [], []
        for k in range(ff // 128):
            a = hb[:, 256 * k:256 * k + 128]
            b = hb[:, 256 * k + 128:256 * k + 256]
            glu.append(jnp.where(even, a, pltpu.roll(b, 1, 1)))
            lin.append(jnp.where(even, pltpu.roll(a, 127, 1), b))
        x_glu = jnp.minimum(jnp.concatenate(glu, axis=1), SWIGLU_LIMIT)
        x_lin = jnp.clip(jnp.concatenate(lin, axis=1), -SWIGLU_LIMIT, SWIGLU_LIMIT)
        act = x_glu * _sigmoid(SWIGLU_ALPHA * x_glu) * (x_lin + 1.0)
        o_ref[rs, :] = _bdot(act.astype(BF16), w2p_ref[...]) + b2_ref[0, 0]

    for r0 in range(0, EXPERT_BLOCK, EXPERT_SUB):
        rs = slice(r0, r0 + EXPERT_SUB)
        pl.when(n_rows > r0)(functools.partial(ffn, rs))

        @pl.when(n_rows <= r0)
        def _():
            o_ref[rs, :] = jnp.zeros((EXPERT_SUB, d), F32)


def _experts(layer, block_e, first, block_rows, buf, w1, b1, w2, b2, n_blocks):
    d = buf.shape[1]
    ff = w2.shape[2]
    xmap = lambda i, be, fi, rw: (jnp.where(rw[i] > 0, i, 0), 0)
    wmap = lambda i, be, fi, rw: (layer, be[i], 0, 0)
    return pl.pallas_call(
        _expert_kernel,
        grid_spec=pltpu.PrefetchScalarGridSpec(
            num_scalar_prefetch=3, grid=(n_blocks,),
            in_specs=[pl.BlockSpec((EXPERT_BLOCK, d), xmap),
                      pl.BlockSpec((1, 1, d, 2 * ff), wmap),
                      pl.BlockSpec((1, 1, 1, 2 * ff), wmap),
                      pl.BlockSpec((1, 1, ff, d), wmap),
                      pl.BlockSpec((1, 1, 1, d), wmap)],
            out_specs=pl.BlockSpec((EXPERT_BLOCK, d), lambda i, be, fi, rw: (i, 0)),
            scratch_shapes=[pltpu.VMEM((d, 2 * ff), BF16), pltpu.VMEM((ff, d), BF16),
                            pltpu.VMEM((ff, 128), F32)]),
        out_shape=jax.ShapeDtypeStruct((n_blocks * EXPERT_BLOCK, d), F32),
        compiler_params=_ARB(1),
        name="expert_ffn",
    )(block_e, first, block_rows, buf, w1, b1, w2, b2)


def _combine_kernel(off_ref, start_ref, pieces_ref, slot_ref, gate_ref, h_ref, g2_ref, fin_ref,
                    obuf_ref, out_ref, rows_ref, sems, *, final_norm):
    step = pl.program_id(0)
    buf_slot = lax.rem(step, 2)
    rows_now = rows_ref.at[buf_slot]

    def copy_into(slot):
        def copy(local_row, global_row, rows):
            return _row_copy(obuf_ref.at[pl.ds(global_row, rows)],
                             rows_ref.at[slot, pl.ds(local_row, rows)], sems.at[slot])
        return copy

    @pl.when(step == 0)
    def _():
        rows_ref[...] = jnp.zeros_like(rows_ref)
        _start_runs(0, off_ref, start_ref, pieces_ref, copy_into(0))

    @pl.when(step + 1 < pl.num_programs(0))
    def _():
        _start_runs(step + 1, off_ref, start_ref, pieces_ref, copy_into(1 - buf_slot))

    total = lax.fori_loop(0, N_EXPERTS,
                          lambda e, t: t + pieces_ref[step * N_EXPERTS + e], 0)
    _wait_runs(total, copy_into(buf_slot))

    y = jnp.zeros(h_ref.shape, F32)
    for c in range(rows_now.shape[0] // UNSORT_CHUNK):
        hits = _slot_hits(slot_ref, c, UNSORT_CHUNK)
        weight = jnp.where(hits[0], gate_ref[0:1, :], 0.0)
        for k in range(1, TOP_K):
            weight = weight + jnp.where(hits[k], gate_ref[k:k + 1, :], 0.0)
        rows = rows_now[c * UNSORT_CHUNK:(c + 1) * UNSORT_CHUNK, :].astype(BF16)
        y = y + lax.dot_general(weight.astype(BF16), rows, (((0,), (0,)), ((), ())),
                                preferred_element_type=F32)
    hn = h_ref[...] + g2_ref[0] * y
    if final_norm:
        ms = jnp.mean(hn * hn, axis=-1, keepdims=True)
        hn = hn * lax.rsqrt(ms + RMS_EPS) * fin_ref[...]
    out_ref[...] = hn


def _combine(plan, slots, gate, h, g2, final_g, obuf, seq, final_norm):
    n, d = h.shape
    tm = TOKEN_TILE
    per_b = seq // tm
    return pl.pallas_call(
        functools.partial(_combine_kernel, final_norm=final_norm),
        grid_spec=pltpu.PrefetchScalarGridSpec(
            num_scalar_prefetch=3, grid=(n // tm,),
            in_specs=[pl.BlockSpec((TOP_K, tm), lambda i, *_: (0, i)),
                      pl.BlockSpec((TOP_K, tm), lambda i, *_: (0, i)),
                      pl.BlockSpec((tm, d), lambda i, *_: (i, 0)),
                      pl.BlockSpec((1, 1, d), lambda i, *_: (i // per_b, 0, 0)),
                      pl.BlockSpec((1, d), lambda i, *_: (0, 0)),
                      pl.BlockSpec(memory_space=pl.ANY)],
            out_specs=pl.BlockSpec((tm, d), lambda i, *_: (i, 0)),
            scratch_shapes=[pltpu.VMEM((2, _local_rows(tm), d), F32),
                            pltpu.SemaphoreType.DMA((2,))]),
        out_shape=jax.ShapeDtypeStruct((n, d), F32),
        compiler_params=_ARB(1),
        name="combine_rows",
    )(plan["run_off"], plan["run_start"], plan["run_pieces"], slots, gate, h, g2, final_g, obuf)


def _local_rows(tm):
    rows = tm * TOP_K + N_EXPERTS * (ROW_ALIGN - 1)
    return -(-rows // SORT_CHUNK) * SORT_CHUNK


def _num_blocks(n):
    rows = n * TOP_K + (n // TOKEN_TILE) * N_EXPERTS * (ROW_ALIGN - 1)
    return -(-rows // EXPERT_BLOCK) + N_EXPERTS


def _slot_plan(tile_counts, n):
    blk = EXPERT_BLOCK
    n_blocks = _num_blocks(n)
    cnt = (tile_counts + ROW_ALIGN - 1) // ROW_ALIGN * ROW_ALIGN
    total = jnp.sum(cnt, axis=0)
    padded = (total + blk - 1) // blk * blk
    pad_end = jnp.cumsum(padded)
    pad_start = pad_end - padded
    run_off = pad_start[None, :] + jnp.cumsum(cnt, axis=0) - cnt
    run_start = jnp.cumsum(cnt, axis=1) - cnt
    n_valid = pad_end[-1] // blk
    block_start = jnp.arange(n_blocks, dtype=I32) * blk
    block_e = jnp.sum(pad_end[None, :] <= block_start[:, None], axis=1)
    last_e = jnp.sum(pad_end <= (n_valid - 1) * blk)
    block_e = jnp.where(jnp.arange(n_blocks) < n_valid, block_e, last_e).astype(I32)
    first = jnp.concatenate([jnp.ones((1,), I32),
                             (block_e[1:] != block_e[:-1]).astype(I32)])
    hit = jnp.arange(N_EXPERTS)[None, :] == block_e[:, None]
    data_end = jnp.sum(jnp.where(hit, pad_start + total, 0), axis=1)
    block_rows = jnp.where(jnp.arange(n_blocks) < n_valid,
                           jnp.clip(data_end - block_start, 0, blk), 0).astype(I32)
    flat = lambda a: a.reshape(-1).astype(I32)
    zero_blocks = n_valid + jnp.arange(n_blocks - (n * TOP_K) // blk)
    return dict(block_e=block_e, first=first, block_rows=block_rows,
                zero_blocks=flat(jnp.minimum(zero_blocks, n_blocks - 1)),
                n_zero=(n_blocks - n_valid).reshape(1).astype(I32),
                tail_off=flat(pad_start + total), tail_pieces=flat((padded - total) // ROW_ALIGN),
                run_off=flat(run_off), run_start=flat(run_start),
                run_pieces=flat(cnt // ROW_ALIGN))


def kernel(x, c, positions, w_ada, b_ada, norm1_g, w_in, w_s, b_s, ln_g, ln_b, w_pa, w_pg,
           w_o, norm2_g, w_router, b_router, w1, b1, w2, b2, final_g):
    batch, seq, d = x.shape
    depth = w_ada.shape[0]
    n = batch * seq
    n_blocks = _num_blocks(n)

    mod = _modulation(c, w_ada, b_ada)
    cos, sin = _rope_tables(positions)

    n_mix = 3 * ATTN_W + 2 * GMLP_W
    b1r = b1.reshape(depth, N_EXPERTS, 1, -1)
    b2r = b2.reshape(depth, N_EXPERTS, 1, -1)

    h = x.reshape(n, d)
    for l in range(depth):
        ml = mod[l].reshape(batch, 6, 1, d)
        sh1, sc1, g1, sh2, sc2, g2 = [ml[:, i] for i in range(6)]
        w_mix = w_in[l, :, :n_mix].astype(BF16)
        w_gate = w_in[l, :, n_mix:].astype(BF16)

        *qkvs, gm = _inproj(h, sc1, sh1, norm1_g[l].reshape(1, d), w_mix, cos, sin,
                            w_s[l], b_s[l].T, ln_g[l].reshape(1, -1), ln_b[l].reshape(1, -1), seq)
        outs, lses = zip(*[_attention_group(qkvs[g], g) for g in range(N_GROUPS)])

        h, f, slots, gate, counts = _merge_route(
            h, (sc1, sh1, g1, sc2, sh2), norm1_g[l].reshape(1, d), norm2_g[l].reshape(1, d),
            w_gate, w_pa[l].astype(BF16), w_pg[l].astype(BF16), w_o[l].astype(BF16),
            outs, lses, gm, w_router[l].T, b_router[l].reshape(-1, 1), seq)

        plan = _slot_plan(counts[:, 0].reshape(-1, N_EXPERTS), n)
        buf = _dispatch(plan, slots, f, n_blocks)
        obuf = _experts(l, plan["block_e"], plan["first"], plan["block_rows"], buf,
                        w1, b1r, w2, b2r, n_blocks)

        h = _combine(plan, slots, gate, h, g2, final_g.reshape(1, d), obuf, seq,
                     final_norm=(l == depth - 1))
    return h.reshape(batch, seq, d)
```
